```python
import jax, jax.numpy as jnp
from jax import lax
import numpy as np

D_MODEL = 1024
BATCH = 4
SEQ = 8192
DEPTH = 4
DEC_BATCH = 32
DEC_SEQ = 32
PAST_LEN = 2048

CHUNK = 64
A_HEAD_DIM = 64
A_WIDTH = D_MODEL // 2
A_HEADS = A_WIDTH // A_HEAD_DIM
A_BACK = 8
A_PAST = A_BACK * CHUNK
A_BAND = (A_BACK + 1) * CHUNK
REL_MAX = 128
REL_SIZE = (CHUNK - 1) + REL_MAX + 1
POOL_WINDOWS = (2, 4, 8, 16)
B_GROUPS = len(POOL_WINDOWS)
B_WIDTH = D_MODEL // 4
B_GROUP_DIM = B_WIDTH // B_GROUPS
POOL_HIST = max(POOL_WINDOWS) - 1
C_HEADS = 4
C_VAL_WIDTH = D_MODEL // 4
C_KEY_WIDTH = D_MODEL // 4
C_KEY_DIM = C_KEY_WIDTH // C_HEADS
C_VAL_DIM = C_VAL_WIDTH // C_HEADS
MIX_WIDTH = A_WIDTH + B_WIDTH + C_VAL_WIDTH
IN_SPLITS = (A_WIDTH, A_WIDTH, A_WIDTH, B_WIDTH, C_KEY_WIDTH, C_KEY_WIDTH, C_VAL_WIDTH, C_VAL_WIDTH)
IN_WIDTH = sum(IN_SPLITS)
FFN_HIDDEN = 2816
PLE_DIM = 256
DEEPNORM_ALPHA = (2 * DEPTH) ** 0.25
DEEPNORM_BETA = (8 * DEPTH) ** -0.25
LN_EPS = 1e-5
RMS_EPS = 1e-6

kernel_name = 'hybrid_streaming_encoder_step'


def layer_norm(x, g, b):
    xf = x.astype(jnp.float32)
    mu = jnp.mean(xf, axis=-1, keepdims=True)
    var = jnp.mean(jnp.square(xf - mu), axis=-1, keepdims=True)
    y = (xf - mu) * lax.rsqrt(var + LN_EPS) * g.astype(jnp.float32) + b.astype(jnp.float32)
    return y.astype(x.dtype)


def swiglu(x, w_gu, w_down):
    gate, up = jnp.split(x @ w_gu, 2, axis=-1)
    return (jax.nn.silu(gate) * up) @ w_down


def rel_position_bias(table, n_q, n_k, offset):
    d = offset + jnp.arange(n_q)[:, None] - jnp.arange(n_k)[None, :]
    idx = jnp.clip(d, -(CHUNK - 1), REL_MAX) + (CHUNK - 1)
    return table[:, idx].astype(jnp.float32)


def band_attention_prompt(q, k, v, rel_bias):
    n, t, h, dh = q.shape
    nc = t // CHUNK
    qc = q.reshape(n, nc, CHUNK, h, dh)
    pad = ((0, 0), (A_PAST, 0), (0, 0), (0, 0))
    kp = jnp.pad(k, pad).reshape(n, nc + A_BACK, CHUNK, h, dh)
    vp = jnp.pad(v, pad).reshape(n, nc + A_BACK, CHUNK, h, dh)
    kb = jnp.concatenate([kp[:, j:j + nc] for j in range(A_BACK + 1)], axis=2)
    vb = jnp.concatenate([vp[:, j:j + nc] for j in range(A_BACK + 1)], axis=2)
    key_chunk = jnp.arange(nc)[:, None] - A_BACK + (jnp.arange(A_BAND) // CHUNK)[None, :]
    valid = key_chunk >= 0
    bias = rel_position_bias(rel_bias, CHUNK, A_BAND, A_PAST)
    s = jnp.einsum('bcqhd,bckhd->bchqk', qc, kb).astype(jnp.float32) * (A_HEAD_DIM ** -0.5)
    s = s + bias[None, None]
    s = jnp.where(valid[None, :, None, None, :], s, -1e30)
    p = jax.nn.softmax(s, axis=-1)
    o = jnp.einsum('bchqk,bckhd->bcqhd', p.astype(v.dtype), vb)
    return o.reshape(n, t, h * dh)


def band_attention_sample(q, k, v, cache_k, cache_v, rel_bias):
    n, t, h, dh = q.shape
    lc = cache_k.shape[1]
    kk = jnp.concatenate([cache_k.astype(k.dtype), k], axis=1)
    vv = jnp.concatenate([cache_v.astype(v.dtype), v], axis=1)
    bias = rel_position_bias(rel_bias, t, lc + t, lc)
    s = jnp.einsum('bqhd,bkhd->bhqk', q, kk).astype(jnp.float32) * (A_HEAD_DIM ** -0.5) + bias[None]
    p = jax.nn.softmax(s, axis=-1)
    o = jnp.einsum('bhqk,bkhd->bqhd', p.astype(vv.dtype), vv)
    return o.reshape(n, t, h * dh)


def pool_mixer(u, hist, pos0, w_grp, scale):
    n, t, c = u.shape
    full = jnp.concatenate([hist.astype(u.dtype), u], axis=1).astype(jnp.float32)
    cs = jnp.concatenate([jnp.zeros((n, 1, c), jnp.float32), jnp.cumsum(full, axis=1)], axis=1)
    pos = pos0 + jnp.arange(t)
    end = cs[:, POOL_HIST + 1:]
    means = []
    for gi, w in enumerate(POOL_WINDOWS):
        cols = slice(gi * B_GROUP_DIM, (gi + 1) * B_GROUP_DIM)
        start = cs[:, POOL_HIST + 1 - w:POOL_HIST + 1 - w + t, cols]
        cnt = jnp.minimum(pos + 1, w).astype(jnp.float32)
        means.append((end[:, :, cols] - start) / cnt[None, :, None])
    d = jnp.concatenate(means, axis=-1) - full[:, POOL_HIST:]
    y = jnp.einsum('ntgc,gce->ntge', d.reshape(n, t, B_GROUPS, B_GROUP_DIM), w_grp.astype(jnp.float32))
    y = y.reshape(n, t, c) * scale.astype(jnp.float32)
    return y, full[:, -POOL_HIST:].astype(u.dtype)


def gla_chunked(q, k, v, log_f, s0, block):
    n, t, h, dk = q.shape
    dv = v.shape[-1]
    nb = t // block

    def blocks(a):
        return a.reshape(n, nb, block, h, a.shape[-1]).transpose(1, 0, 3, 2, 4)

    causal = jnp.tril(jnp.ones((block, block), dtype=bool))

    def step(state, xs):
        qb, kb, vb, lfb = xs
        b = jnp.cumsum(lfb, axis=2)
        rel = jnp.where(causal[:, :, None], b[:, :, :, None, :] - b[:, :, None, :, :], -jnp.inf)
        scores = jnp.einsum('nhtsc,nhsc->nhts', qb[:, :, :, None, :] * jnp.exp(rel), kb)
        out = (jnp.einsum('nhtc,nhcv->nhtv', qb * jnp.exp(b), state)
               + jnp.einsum('nhts,nhsv->nhtv', scores, vb))
        b_last = b[:, :, -1:, :]
        state = (jnp.exp(b_last[:, :, 0, :, None]) * state
                 + jnp.einsum('nhsc,nhsv->nhcv', kb * jnp.exp(b_last - b), vb))
        return state, out

    state, out = lax.scan(step, s0, (blocks(q), blocks(k), blocks(v), blocks(log_f)))
    out = out.transpose(1, 0, 3, 2, 4).reshape(n, t, h, dv)
    return out, state


def hgrn2_mixer(q, f, i, g, lb, norm_w, s0, block):
    n, t, _ = q.shape
    shp = (n, t, C_HEADS, C_KEY_DIM)
    lbh = lb.reshape(C_HEADS, C_KEY_DIM)
    forget = lbh + (1.0 - lbh) * jax.nn.sigmoid(f.astype(jnp.float32)).reshape(shp)
    log_f = jnp.log(forget)
    key_in = 1.0 - forget
    qf = jax.nn.silu(q.astype(jnp.float32)).reshape(shp)
    vf = i.astype(jnp.float32).reshape(n, t, C_HEADS, C_VAL_DIM)
    o, s = gla_chunked(qf, key_in, vf, log_f, s0.astype(jnp.float32), block)
    o = o * lax.rsqrt(jnp.mean(o * o, axis=-1, keepdims=True) + RMS_EPS) * norm_w.astype(jnp.float32)
    o = o.reshape(n, t, C_VAL_WIDTH) * jax.nn.silu(g.astype(jnp.float32))
    return o, s


def token_mixers(x, w_in, rel_bias, pool_w, pool_scale, lb, norm_w, w_out, state):
    n, t, _ = x.shape
    z = x @ w_in
    offs = [int(o) for o in np.cumsum(IN_SPLITS)[:-1]]
    q_a, k_a, v_a, u_b, q_c, f_c, i_c, g_c = jnp.split(z, offs, axis=-1)
    q_a = q_a.reshape(n, t, A_HEADS, A_HEAD_DIM)
    k_a = k_a.reshape(n, t, A_HEADS, A_HEAD_DIM)
    v_a = v_a.reshape(n, t, A_HEADS, A_HEAD_DIM)
    if state is None:
        o_a = band_attention_prompt(q_a, k_a, v_a, rel_bias)
        rows = min(A_PAST, t)
        new_k, new_v = k_a[:, t - rows:], v_a[:, t - rows:]
        pool_hist = jnp.zeros((n, POOL_HIST, B_WIDTH), u_b.dtype)
        pos0 = 0
        s0 = jnp.zeros((n, C_HEADS, C_KEY_DIM, C_VAL_DIM), jnp.float32)
        block = CHUNK
    else:
        cache_k, cache_v, pool_hist, s0 = state
        o_a = band_attention_sample(q_a, k_a, v_a, cache_k, cache_v, rel_bias)
        new_k, new_v = k_a, v_a
        pos0 = PAST_LEN
        block = t
    o_b, new_pool = pool_mixer(u_b, pool_hist, pos0, pool_w, pool_scale)
    o_c, new_s = hgrn2_mixer(q_c, f_c, i_c, g_c, lb, norm_w, s0, block)
    mixed = jnp.concatenate([o_a, o_b.astype(x.dtype), o_c.astype(x.dtype)], axis=-1)
    return mixed @ w_out, (new_k, new_v, new_pool, new_s)


def hgrn_lower_bound_schedule(raw):
    s = jax.nn.softmax(raw.astype(jnp.float32), axis=0)
    return jnp.cumsum(s, axis=0) - s[0:1]


def run_trunk(x, p, states, lb, ffn1_w_gu, ffn1_w_down, w_in, attn_rel_bias, pool_w, pool_scale,
              hgrn_norm_w, w_out, ffn2_w_gu, ffn2_w_down, ple_w_gate, ple_w_proj, ln_g, ln_b):
    collected = ([], [], [], [])
    for i in range(DEPTH):
        x = layer_norm(DEEPNORM_ALPHA * x + 0.5 * swiglu(x, ffn1_w_gu[i], ffn1_w_down[i]), ln_g[i, 0], ln_b[i, 0])
        layer_state = None if states is None else (states[0][i], states[1][i], states[2][i], states[3][i])
        m, new = token_mixers(x, w_in[i], attn_rel_bias[i], pool_w[i], pool_scale[i], lb[i],
                              hgrn_norm_w[i], w_out[i], layer_state)
        x = layer_norm(DEEPNORM_ALPHA * x + m, ln_g[i, 1], ln_b[i, 1])
        x = layer_norm(DEEPNORM_ALPHA * x + 0.5 * swiglu(x, ffn2_w_gu[i], ffn2_w_down[i]), ln_g[i, 2], ln_b[i, 2])
        emb = jax.nn.sigmoid(x @ ple_w_gate[i]) * (p[i].astype(x.dtype) @ ple_w_proj[i])
        x = layer_norm(DEEPNORM_ALPHA * x + emb, ln_g[i, 3], ln_b[i, 3])
        for lst, arr in zip(collected, new):
            lst.append(arr)
    return x, tuple(jnp.stack(lst, axis=0) for lst in collected)


def setup_inputs(seed: int = 0) -> dict:
    key = jax.random.key(seed)
    ks = jax.random.split(key, 24)
    f32 = jnp.float32

    def nrm(k, shape, scale):
        return jax.random.normal(k, shape, f32) * scale

    a_cache = min(A_PAST, PAST_LEN)
    return {
        'x_prompt': nrm(ks[0], (BATCH, SEQ, D_MODEL), 1.0),
        'x_sample': nrm(ks[1], (DEC_BATCH, DEC_SEQ, D_MODEL), 1.0),
        'p_prompt': nrm(ks[2], (DEPTH, BATCH, SEQ, PLE_DIM), 1.0),
        'p_sample': nrm(ks[3], (DEPTH, DEC_BATCH, DEC_SEQ, PLE_DIM), 1.0),
        'cache_attn_k': nrm(ks[4], (DEPTH, DEC_BATCH, a_cache, A_HEADS, A_HEAD_DIM), 1.0),
        'cache_attn_v': nrm(ks[5], (DEPTH, DEC_BATCH, a_cache, A_HEADS, A_HEAD_DIM), 1.0),
        'state_pool': nrm(ks[6], (DEPTH, DEC_BATCH, POOL_HIST, B_WIDTH), 1.0),
        'state_hgrn': nrm(ks[7], (DEPTH, DEC_BATCH, C_HEADS, C_KEY_DIM, C_VAL_DIM), 0.5),
        'ffn1_w_gu': nrm(ks[8], (DEPTH, D_MODEL, 2 * FFN_HIDDEN), D_MODEL ** -0.5),
        'ffn1_w_down': nrm(ks[9], (DEPTH, FFN_HIDDEN, D_MODEL), FFN_HIDDEN ** -0.5 * DEEPNORM_BETA),
        'w_in': nrm(ks[10], (DEPTH, D_MODEL, IN_WIDTH), D_MODEL ** -0.5),
        'attn_rel_bias': nrm(ks[11], (DEPTH, A_HEADS, REL_SIZE), 0.3),
        'pool_w': nrm(ks[12], (DEPTH, B_GROUPS, B_GROUP_DIM, B_GROUP_DIM), B_GROUP_DIM ** -0.5),
        'pool_scale': 1.0 + nrm(ks[13], (DEPTH, B_WIDTH), 0.1),
        'hgrn_lower_bounds': nrm(ks[14], (DEPTH, C_KEY_WIDTH), 0.3),
        'hgrn_norm_w': 1.0 + nrm(ks[15], (DEPTH, C_VAL_DIM), 0.1),
        'w_out': nrm(ks[16], (DEPTH, MIX_WIDTH, D_MODEL), MIX_WIDTH ** -0.5 * DEEPNORM_BETA),
        'ffn2_w_gu': nrm(ks[17], (DEPTH, D_MODEL, 2 * FFN_HIDDEN), D_MODEL ** -0.5),
        'ffn2_w_down': nrm(ks[18], (DEPTH, FFN_HIDDEN, D_MODEL), FFN_HIDDEN ** -0.5 * DEEPNORM_BETA),
        'ple_w_gate': nrm(ks[19], (DEPTH, D_MODEL, D_MODEL), D_MODEL ** -0.5),
        'ple_w_proj': nrm(ks[20], (DEPTH, PLE_DIM, D_MODEL), PLE_DIM ** -0.5 * DEEPNORM_BETA),
        'ln_g': 1.0 + nrm(ks[21], (DEPTH, 4, D_MODEL), 0.1),
        'ln_b': nrm(ks[22], (DEPTH, 4, D_MODEL), 0.02),
    }


def reference(x_prompt, x_sample, p_prompt, p_sample, cache_attn_k, cache_attn_v, state_pool, state_hgrn,
              ffn1_w_gu, ffn1_w_down, w_in, attn_rel_bias, pool_w, pool_scale, hgrn_lower_bounds,
              hgrn_norm_w, w_out, ffn2_w_gu, ffn2_w_down, ple_w_gate, ple_w_proj, ln_g, ln_b):
    lb = hgrn_lower_bound_schedule(hgrn_lower_bounds)
    y_prompt, (pk, pv, ppool, phgrn) = run_trunk(
        x_prompt, p_prompt, None, lb, ffn1_w_gu, ffn1_w_down, w_in, attn_rel_bias, pool_w, pool_scale,
        hgrn_norm_w, w_out, ffn2_w_gu, ffn2_w_down, ple_w_gate, ple_w_proj, ln_g, ln_b)
    y_sample, (sk, sv, spool, shgrn) = run_trunk(
        x_sample, p_sample, (cache_attn_k, cache_attn_v, state_pool, state_hgrn), lb,
        ffn1_w_gu, ffn1_w_down, w_in, attn_rel_bias, pool_w, pool_scale,
        hgrn_norm_w, w_out, ffn2_w_gu, ffn2_w_down, ple_w_gate, ple_w_proj, ln_g, ln_b)
    return (y_prompt, y_sample, pk, pv, ppool, phgrn, sk, sv, spool, shgrn)
```

```python
import functools

import jax
import jax.numpy as jnp
from jax import lax
from jax.experimental import pallas as pl
from jax.experimental.pallas import tpu as pltpu

F32 = jnp.float32
BF16 = jnp.bfloat16

PAST_LEN = 2048
CHUNK = 64
A_BACK = 8
HEAD_DIM = 64
A_HEADS = 8
C_HEADS = 4
REL_MAX = 128
POOL_WINDOWS = (2, 4, 8, 16)
POOL_HIST = max(POOL_WINDOWS) - 1
POOL_PAD = POOL_HIST + 1
SUB_BLOCK = 32
LN_EPS = 1e-5
RMS_EPS = 1e-6
NEG_INF = -1e30

V7X_LANES = 128
V7X_VMEM_LIMIT = 56 * 1024 * 1024


def _dot(a, b):
    return jnp.dot(a, b, preferred_element_type=F32)


def _dot_nt(a, b):
    return lax.dot_general(a, b, (((1,), (1,)), ((), ())), preferred_element_type=F32)


def _dot_tn(a, b):
    return lax.dot_general(a, b, (((0,), (0,)), ((), ())), preferred_element_type=F32)


def _layer_norm(y, g, b):
    mu = jnp.mean(y, axis=-1, keepdims=True)
    d = y - mu
    var = jnp.mean(d * d, axis=-1, keepdims=True)
    return d * lax.rsqrt(var + LN_EPS) * g + b


def _sigmoid(x):
    return 1.0 / (1.0 + jnp.exp(-x))


def _log2(n):
    assert n > 0 and n & (n - 1) == 0, n
    return n.bit_length() - 1


def _div(x, n):
    return x >> _log2(n)


def _swiglu(xb, wgu_ref, wdn_ref, n_split):
    hidden = wdn_ref.shape[0]
    hc = hidden // n_split
    acc = None
    for c in range(n_split):
        gate = _dot(xb, wgu_ref[:, c * hc:(c + 1) * hc])
        up = _dot(xb, wgu_ref[:, hidden + c * hc:hidden + (c + 1) * hc])
        act = (gate * _sigmoid(gate) * up).astype(BF16)
        part = _dot(act, wdn_ref[c * hc:(c + 1) * hc, :])
        acc = part if acc is None else acc + part
    return acc


def _pre_kernel(alpha, n_split, x_ref, wgu_ref, wdn_ref, lng_ref, lnb_ref, win_ref, xo_ref, z_ref):
    x = x_ref[...]
    y = alpha * x + 0.5 * _swiglu(x.astype(BF16), wgu_ref, wdn_ref, n_split)
    xn = _layer_norm(y, lng_ref[0:1, :], lnb_ref[0:1, :])
    xo_ref[...] = xn
    z_ref[...] = _dot(xn.astype(BF16), win_ref[...])


def _post_kernel(alpha, n_split, x_ref, oa_ref, obc_ref, p_ref, wout_ref, wgu_ref, wdn_ref, wg_ref, wp_ref,
                 lng_ref, lnb_ref, xo_ref):
    x = x_ref[...]
    wa = oa_ref.shape[1]
    mixed = _dot(oa_ref[...], wout_ref[0:wa, :]) + _dot(obc_ref[...], wout_ref[wa:, :])
    x = _layer_norm(alpha * x + mixed, lng_ref[1:2, :], lnb_ref[1:2, :])
    y = alpha * x + 0.5 * _swiglu(x.astype(BF16), wgu_ref, wdn_ref, n_split)
    x = _layer_norm(y, lng_ref[2:3, :], lnb_ref[2:3, :])
    emb = _sigmoid(_dot(x.astype(BF16), wg_ref[...])) * _dot(p_ref[...].astype(BF16), wp_ref[...])
    xo_ref[...] = _layer_norm(alpha * x + emb, lng_ref[3:4, :], lnb_ref[3:4, :])


def _const_spec(shape):
    return pl.BlockSpec(shape, lambda *_: (0,) * len(shape), pipeline_mode=pl.Buffered(1))


def _token_tile(n):
    for tm in (256, 128, 64, 32, 16, 8):
        if n % tm == 0:
            return tm
    raise ValueError(f"unsupported token count {n}")


def _pre_call(x, wgu, wdn, lng, lnb, win, alpha):
    n, d = x.shape
    zw = win.shape[1]
    tm = _token_tile(n)
    row = lambda i: (i, 0)
    return pl.pallas_call(
        functools.partial(_pre_kernel, alpha, 2),
        grid=(n // tm,),
        in_specs=[pl.BlockSpec((tm, d), row), _const_spec(wgu.shape), _const_spec(wdn.shape),
                  _const_spec(lng.shape), _const_spec(lnb.shape), _const_spec(win.shape)],
        out_specs=[pl.BlockSpec((tm, d), row), pl.BlockSpec((tm, zw), row)],
        out_shape=[jax.ShapeDtypeStruct((n, d), F32), jax.ShapeDtypeStruct((n, zw), F32)],
        compiler_params=pltpu.CompilerParams(dimension_semantics=("parallel",),
                                             vmem_limit_bytes=V7X_VMEM_LIMIT),
        name="pre_mixer_tokens",
    )(x, wgu, wdn, lng, lnb, win)


def _post_call(x, oa, obc, p, wout, wgu, wdn, wg, wp, lng, lnb, alpha):
    n, d = x.shape
    tm = _token_tile(n)
    row = lambda i: (i, 0)
    return pl.pallas_call(
        functools.partial(_post_kernel, alpha, 2),
        grid=(n // tm,),
        in_specs=[pl.BlockSpec((tm, d), row), pl.BlockSpec((tm, oa.shape[1]), row),
                  pl.BlockSpec((tm, obc.shape[1]), row), pl.BlockSpec((tm, p.shape[1]), row),
                  _const_spec(wout.shape), _const_spec(wgu.shape), _const_spec(wdn.shape),
                  _const_spec(wg.shape), _const_spec(wp.shape), _const_spec(lng.shape), _const_spec(lnb.shape)],
        out_specs=pl.BlockSpec((tm, d), row),
        out_shape=jax.ShapeDtypeStruct((n, d), F32),
        compiler_params=pltpu.CompilerParams(dimension_semantics=("parallel",),
                                             vmem_limit_bytes=V7X_VMEM_LIMIT),
        name="post_mixer_tokens",
    )(x, oa, obc, p, wout, wgu, wdn, wg, wp, lng, lnb)


def _attend_head_pair(q, k_band, v_band, bias, key_valid):
    r = q.shape[0]
    low = lax.broadcasted_iota(jnp.int32, (1, V7X_LANES), 1) < HEAD_DIM
    lhs = jnp.concatenate([jnp.where(low, q, 0.0), jnp.where(low, 0.0, q)], axis=0).astype(BF16)
    s = _dot_nt(lhs, k_band) + bias
    if key_valid is not None:
        s = jnp.where(key_valid, s, NEG_INF)
    e = jnp.exp(s - jnp.max(s, axis=-1, keepdims=True))
    inv = 1.0 / jnp.sum(e, axis=-1, keepdims=True)
    o2 = _dot(e.astype(BF16), v_band) * inv
    return jnp.where(low, o2[:r], o2[r:])


def _attn_prompt_kernel(qb_rows, q_ref, kp_ref, kc_ref, vp_ref, vc_ref, bias_ref, o_ref, kcat, vcat):
    i = pl.program_id(1)
    kcat[0:qb_rows, :] = kp_ref[...].astype(BF16)
    kcat[qb_rows:, :] = kc_ref[...].astype(BF16)
    vcat[0:qb_rows, :] = vp_ref[...].astype(BF16)
    vcat[qb_rows:, :] = vc_ref[...].astype(BF16)
    band = (A_BACK + 1) * CHUNK
    past = A_BACK * CHUNK
    n_pairs = q_ref.shape[1] // V7X_LANES
    scale = HEAD_DIM ** -0.5
    key_idx = lax.broadcasted_iota(jnp.int32, (1, band), 1)

    def chunk_body(ci, carry):
        r0 = pl.multiple_of(ci * CHUNK, CHUNK)
        key_valid = key_idx + (i * qb_rows + ci * CHUNK - past) >= 0
        for hp in range(n_pairs):
            ls = slice(hp * V7X_LANES, (hp + 1) * V7X_LANES)
            q = q_ref[pl.ds(r0, CHUNK), ls] * scale
            o = _attend_head_pair(q, kcat[pl.ds(r0, band), ls], vcat[pl.ds(r0, band), ls], bias_ref[hp], key_valid)
            o_ref[pl.ds(r0, CHUNK), ls] = o.astype(o_ref.dtype)
        return carry

    lax.fori_loop(0, qb_rows // CHUNK, chunk_body, 0)


def _attn_prompt_call(z, bias, batch, seq, a_width):
    qb_rows = A_BACK * CHUNK
    assert seq % qb_rows == 0 and a_width % V7X_LANES == 0
    nq = seq // qb_rows
    cur = lambda col: (lambda b, i: (b * nq + i, col))
    prev = lambda col: (lambda b, i: (b * nq + jnp.maximum(i - 1, 0), col))
    blk = (qb_rows, a_width)
    return pl.pallas_call(
        functools.partial(_attn_prompt_kernel, qb_rows),
        grid=(batch, nq),
        in_specs=[pl.BlockSpec(blk, cur(0)), pl.BlockSpec(blk, prev(1)), pl.BlockSpec(blk, cur(1)),
                  pl.BlockSpec(blk, prev(2)), pl.BlockSpec(blk, cur(2)),
                  pl.BlockSpec(bias.shape, lambda b, i: (0, 0, 0))],
        out_specs=pl.BlockSpec(blk, lambda b, i: (b * nq + i, 0)),
        out_shape=jax.ShapeDtypeStruct((batch * seq, a_width), BF16),
        scratch_shapes=[pltpu.VMEM((2 * qb_rows, a_width), BF16), pltpu.VMEM((2 * qb_rows, a_width), BF16)],
        compiler_params=pltpu.CompilerParams(dimension_semantics=("parallel", "parallel"),
                                             vmem_limit_bytes=V7X_VMEM_LIMIT),
        name="band_attention_prompt",
    )(z, z, z, z, z, bias)


def _attn_sample_kernel(q_ref, k_ref, v_ref, ck_ref, cv_ref, bias_ref, o_ref, kcat, vcat):
    lc = ck_ref.shape[1]
    kcat[0:lc, :] = ck_ref[0].astype(BF16)
    kcat[lc:, :] = k_ref[...].astype(BF16)
    vcat[0:lc, :] = cv_ref[0].astype(BF16)
    vcat[lc:, :] = v_ref[...].astype(BF16)
    scale = HEAD_DIM ** -0.5
    for hp in range(q_ref.shape[1] // V7X_LANES):
        ls = slice(hp * V7X_LANES, (hp + 1) * V7X_LANES)
        o = _attend_head_pair(q_ref[:, ls] * scale, kcat[:, ls], vcat[:, ls], bias_ref[hp], None)
        o_ref[:, ls] = o.astype(o_ref.dtype)


def _attn_sample_call(z, cache_k, cache_v, bias, batch, t, a_width):
    lc = cache_k.shape[1]
    blk = (t, a_width)
    cblk = (1, lc, a_width)
    return pl.pallas_call(
        _attn_sample_kernel,
        grid=(batch,),
        in_specs=[pl.BlockSpec(blk, lambda b: (b, 0)), pl.BlockSpec(blk, lambda b: (b, 1)),
                  pl.BlockSpec(blk, lambda b: (b, 2)),
                  pl.BlockSpec(cblk, lambda b: (b, 0, 0)), pl.BlockSpec(cblk, lambda b: (b, 0, 0)),
                  pl.BlockSpec(bias.shape, lambda b: (0, 0, 0))],
        out_specs=pl.BlockSpec(blk, lambda b: (b, 0)),
        out_shape=jax.ShapeDtypeStruct((batch * t, a_width), BF16),
        scratch_shapes=[pltpu.VMEM((lc + t, a_width), BF16), pltpu.VMEM((lc + t, a_width), BF16)],
        compiler_params=pltpu.CompilerParams(dimension_semantics=("parallel",),
                                             vmem_limit_bytes=V7X_VMEM_LIMIT),
        name="band_attention_sample",
    )(z, z, z, cache_k, cache_v, bias)


def _split3(x):
    a = x.astype(BF16)
    r = x - a.astype(F32)
    b = r.astype(BF16)
    c = (r - b.astype(F32)).astype(BF16)
    return a, b, c


def _split2(x):
    a = x.astype(BF16)
    return a, (x - a.astype(F32)).astype(BF16)


def _recur_kernel(layer, blk, pos0, n_heads, u_ref, q_ref, f_ref, i_ref, g_ref, hist0_ref, st0_ref, lbraw_ref,
                  normw_ref, poolw_ref, pscale_ref, o_ref, st_ref, state, hist):
    t_idx = pl.program_id(1)
    tm, width = u_ref.shape
    hd = width // n_heads
    sb = min(SUB_BLOCK, blk)
    assert blk // sb in (1, 2) and tm % blk == 0

    @pl.when(t_idx == 0)
    def _():
        state[...] = st0_ref[0]
        hist[...] = hist0_ref[0]

    ext = jnp.concatenate([hist[...], u_ref[...]], axis=0)
    hist[...] = ext[tm:, :]
    s2 = ext + pltpu.roll(ext, 1, 0)
    s4 = s2 + pltpu.roll(s2, 2, 0)
    s8 = s4 + pltpu.roll(s4, 4, 0)
    s16 = s8 + pltpu.roll(s8, 8, 0)
    lane = lax.broadcasted_iota(jnp.int32, (1, width), 1)
    grp = _div(lane, width // len(POOL_WINDOWS))
    wsum = jnp.where(grp == 0, s2, jnp.where(grp == 1, s4, jnp.where(grp == 2, s8, s16)))
    wlen = jnp.where(grp == 0, 2.0, jnp.where(grp == 1, 4.0, jnp.where(grp == 2, 8.0, 16.0)))
    row = lax.broadcasted_iota(jnp.int32, (POOL_PAD + tm, 1), 0)
    pos = (row + (pos0 - POOL_PAD) + t_idx * tm).astype(F32)
    cnt = jnp.maximum(jnp.minimum(pos + 1.0, wlen), 1.0)
    dev = (wsum / cnt - ext)[POOL_PAD:, :]
    o_pool = _dot(dev.astype(BF16), poolw_ref[...]) * pscale_ref[...]
    o_ref[:, 0:width] = o_pool.astype(o_ref.dtype)

    raw = lbraw_ref[...]
    sm = jnp.exp(raw - jnp.max(raw, axis=0, keepdims=True))
    sm = sm / jnp.sum(sm, axis=0, keepdims=True)
    lb = jnp.zeros((1, width), F32)
    for j in range(1, layer + 1):
        lb = lb + sm[j:j + 1, :]
    forget = lb + (1.0 - lb) * _sigmoid(f_ref[...])
    log_f = jnp.log(forget)
    k_in = 1.0 - forget
    qx = q_ref[...]
    qf = qx * _sigmoid(qx)
    vb = i_ref[...].astype(BF16)

    ri = lax.broadcasted_iota(jnp.int32, (tm, tm), 0)
    ci = lax.broadcasted_iota(jnp.int32, (tm, tm), 1)
    same_sb = _div(ri, sb) == _div(ci, sb)
    same_blk = _div(ri, blk) == _div(ci, blk)
    lf3 = jnp.concatenate(_split3(log_f), axis=1)

    def decay_sum(mask):
        r = _dot(jnp.where(mask, 1.0, 0.0).astype(BF16), lf3)
        return r[:, 0:width] + r[:, width:2 * width] + r[:, 2 * width:]

    b_rel = decay_sum(same_sb & (ci <= ri))
    sb_tot = decay_sum(same_sb)
    blk_tot = decay_sum(same_blk)
    if blk > sb:
        b = b_rel + decay_sum(same_blk & (_div(ci, sb) < _div(ri, sb)))
    else:
        b = b_rel

    q_rel = qf * jnp.exp(b_rel)
    k_rel = (k_in * jnp.exp(-b_rel)).astype(BF16)
    k_end = (k_in * jnp.exp(sb_tot - b_rel)).astype(BF16)
    q_abs = (qf * jnp.exp(b)).astype(BF16)
    k_tail = (k_in * jnp.exp(blk_tot - b)).astype(BF16)

    hb = n_heads * blk
    stack_head = _div(lax.broadcasted_iota(jnp.int32, (hb, 1), 0), blk) == _div(lane, hd)
    tq = lax.broadcasted_iota(jnp.int32, (hb, blk), 0) & (blk - 1)
    ts = lax.broadcasted_iota(jnp.int32, (hb, blk), 1)
    m_intra = (_div(tq, sb) == _div(ts, sb)) & (ts <= tq)
    m_cross = _div(ts, sb) < _div(tq, sb)
    bd = (_div(lax.broadcasted_iota(jnp.int32, (width, width), 0), hd)
          == _div(lax.broadcasted_iota(jnp.int32, (width, width), 1), hd))

    outs = []
    for c in range(tm // blk):
        rs = slice(c * blk, (c + 1) * blk)
        lhs = jnp.where(stack_head, jnp.concatenate([q_rel[rs]] * n_heads, axis=0), 0.0).astype(BF16)
        a = jnp.where(m_intra, _dot_nt(lhs, k_rel[rs]), 0.0)
        if blk > sb:
            a = a + jnp.where(m_cross, _dot_nt(lhs, k_end[rs]), 0.0)
        stacked = jnp.where(stack_head, _dot(a.astype(BF16), vb[rs]), 0.0)
        o_c = stacked[0:blk]
        for h in range(1, n_heads):
            o_c = o_c + stacked[h * blk:(h + 1) * blk]
        st = state[...]
        outs.append(o_c + _dot_nt(q_abs[rs], st.astype(BF16)))
        decay_end = jnp.exp(blk_tot[c * blk:c * blk + 1, :])
        state[...] = st * decay_end + jnp.where(bd, _dot_tn(vb[rs], k_tail[rs]), 0.0)
    o = outs[0] if len(outs) == 1 else jnp.concatenate(outs, axis=0)

    ones_bd = jnp.where(bd, 1.0, 0.0).astype(BF16)
    sq_hi, sq_lo = _split2(o * o)
    ms = (_dot(sq_hi, ones_bd) + _dot(sq_lo, ones_bd)) * (1.0 / hd)
    gx = g_ref[...]
    o = o * lax.rsqrt(ms + RMS_EPS) * normw_ref[...] * (gx * _sigmoid(gx))
    o_ref[:, width:] = o.astype(o_ref.dtype)

    @pl.when(t_idx == pl.num_programs(1) - 1)
    def _():
        st_ref[0] = state[...]


def _recur_call(z, hist0, st0, lbraw, normw, poolw, pscale, layer, batch, seq, blk, pos0, col0):
    width = st0.shape[-1]
    tm = min(256, seq)
    assert seq % tm == 0 and tm % blk == 0
    nt = seq // tm
    col = lambda c: (lambda b, t: (b * nt + t, c))
    per_batch = lambda b, t: (b, 0, 0)
    return pl.pallas_call(
        functools.partial(_recur_kernel, layer, blk, pos0, C_HEADS),
        grid=(batch, nt),
        in_specs=[pl.BlockSpec((tm, width), col(col0 + j)) for j in range(5)]
        + [pl.BlockSpec((1, POOL_PAD, width), per_batch), pl.BlockSpec((1, width, width), per_batch),
           _const_spec(lbraw.shape), _const_spec(normw.shape), _const_spec(poolw.shape), _const_spec(pscale.shape)],
        out_specs=[pl.BlockSpec((tm, 2 * width), lambda b, t: (b * nt + t, 0)),
                   pl.BlockSpec((1, width, width), per_batch)],
        out_shape=[jax.ShapeDtypeStruct((batch * seq, 2 * width), BF16),
                   jax.ShapeDtypeStruct((batch, width, width), F32)],
        scratch_shapes=[pltpu.VMEM((width, width), F32), pltpu.VMEM((POOL_PAD, width), F32)],
        compiler_params=pltpu.CompilerParams(dimension_semantics=("parallel", "arbitrary"),
                                             vmem_limit_bytes=V7X_VMEM_LIMIT),
        name="pool_hgrn_mixer",
    )(z, z, z, z, z, hist0, st0, lbraw, normw, poolw, pscale)


def _rel_bias_pairs(table, n_q, n_k, offset):
    d = offset + jnp.arange(n_q)[:, None] - jnp.arange(n_k)[None, :]
    idx = jnp.clip(d, -(CHUNK - 1), REL_MAX) + (CHUNK - 1)
    h = table.shape[0]
    return table[:, idx].astype(F32).reshape(h // 2, 2 * n_q, n_k)


def _block_diag(blocks):
    g, a, b = blocks.shape
    eye = jnp.eye(g, dtype=blocks.dtype)
    return (eye[:, None, :, None] * blocks[:, :, None, :]).reshape(g * a, g * b)


def _state_to_kernel(s):
    return jax.vmap(_block_diag)(jnp.swapaxes(s, -1, -2))


def _state_from_kernel(st, n_heads):
    b, w, _ = st.shape
    hd = w // n_heads
    blocks = jnp.stack([st[:, h * hd:(h + 1) * hd, h * hd:(h + 1) * hd] for h in range(n_heads)], axis=1)
    return jnp.swapaxes(blocks, -1, -2)


def kernel(x_prompt, x_sample, p_prompt, p_sample, cache_attn_k, cache_attn_v, state_pool, state_hgrn, ffn1_w_gu, ffn1_w_down, w_in, attn_rel_bias, pool_w, pool_scale, hgrn_lower_bounds, hgrn_norm_w, w_out, ffn2_w_gu, ffn2_w_down, ple_w_gate, ple_w_proj, ln_g, ln_b):
    depth = w_in.shape[0]
    alpha = float((2 * depth) ** 0.25)
    nb, seq, d = x_prompt.shape
    db, dt, _ = x_sample.shape
    a_width = A_HEADS * HEAD_DIM
    c_width = C_HEADS * HEAD_DIM
    in_width = w_in.shape[-1]
    col0 = 3 * a_width // c_width
    lc = cache_attn_k.shape[2]
    past_rows = min(A_BACK * CHUNK, seq)
    assert dt >= POOL_HIST and seq >= POOL_HIST and PAST_LEN >= POOL_HIST

    bf = lambda w: w.astype(BF16)
    w1gu, w1dn, win, wout = bf(ffn1_w_gu), bf(ffn1_w_down), bf(w_in), bf(w_out)
    w2gu, w2dn, wg, wp = bf(ffn2_w_gu), bf(ffn2_w_down), bf(ple_w_gate), bf(ple_w_proj)

    xp = x_prompt.reshape(nb * seq, d)
    xs = x_sample.reshape(db * dt, d)
    zero_hist = jnp.zeros((nb, POOL_PAD, c_width), F32)
    zero_state = jnp.zeros((nb, c_width, c_width), F32)
    lbraw = hgrn_lower_bounds.astype(F32)
    outs = [[] for _ in range(8)]

    for i in range(depth):
        poolw = bf(_block_diag(pool_w[i]))
        pscale = pool_scale[i][None, :]
        normw = jnp.tile(hgrn_norm_w[i], C_HEADS)[None, :]
        bias_p = _rel_bias_pairs(attn_rel_bias[i], CHUNK, (A_BACK + 1) * CHUNK, A_BACK * CHUNK)
        bias_s = _rel_bias_pairs(attn_rel_bias[i], dt, lc + dt, lc)

        xp, zp = _pre_call(xp, w1gu[i], w1dn[i], ln_g[i], ln_b[i], win[i], alpha)
        oa = _attn_prompt_call(zp, bias_p, nb, seq, a_width)
        obc, st = _recur_call(zp, zero_hist, zero_state, lbraw, normw, poolw, pscale, i, nb, seq, CHUNK, 0, col0)
        xp = _post_call(xp, oa, obc, p_prompt[i].reshape(nb * seq, -1), wout[i], w2gu[i], w2dn[i], wg[i], wp[i],
                        ln_g[i], ln_b[i], alpha)
        zp3 = zp.reshape(nb, seq, in_width)
        outs[0].append(zp3[:, seq - past_rows:, a_width:2 * a_width].reshape(nb, past_rows, A_HEADS, HEAD_DIM))
        outs[1].append(zp3[:, seq - past_rows:, 2 * a_width:3 * a_width].reshape(nb, past_rows, A_HEADS, HEAD_DIM))
        outs[2].append(zp3[:, seq - POOL_HIST:, 3 * a_width:3 * a_width + c_width])
        outs[3].append(_state_from_kernel(st, C_HEADS))

        xs, zs = _pre_call(xs, w1gu[i], w1dn[i], ln_g[i], ln_b[i], win[i], alpha)
        oa = _attn_sample_call(zs, cache_attn_k[i].reshape(db, lc, a_width), cache_attn_v[i].reshape(db, lc, a_width),
                               bias_s, db, dt, a_width)
        hist0 = jnp.pad(state_pool[i].astype(F32), ((0, 0), (POOL_PAD - POOL_HIST, 0), (0, 0)))
        obc, st = _recur_call(zs, hist0, _state_to_kernel(state_hgrn[i].astype(F32)), lbraw, normw, poolw, pscale,
                              i, db, dt, dt, PAST_LEN, col0)
        xs = _post_call(xs, oa, obc, p_sample[i].reshape(db * dt, -1), wout[i], w2gu[i], w2dn[i], wg[i], wp[i],
                        ln_g[i], ln_b[i], alpha)
        zs3 = zs.reshape(db, dt, in_width)
        outs[4].append(zs3[:, :, a_width:2 * a_width].reshape(db, dt, A_HEADS, HEAD_DIM))
        outs[5].append(zs3[:, :, 2 * a_width:3 * a_width].reshape(db, dt, A_HEADS, HEAD_DIM))
        outs[6].append(zs3[:, dt - POOL_HIST:, 3 * a_width:3 * a_width + c_width])
        outs[7].append(_state_from_kernel(st, C_HEADS))

    stacked = [jnp.stack(o, axis=0) for o in outs]
    return (xp.reshape(nb, seq, d), xs.reshape(db, dt, d), *stacked)
```

```python
import functools

import jax
import jax.numpy as jnp
from jax import lax
from jax.experimental import pallas as pl
from jax.experimental.pallas import tpu as pltpu

F32 = jnp.float32
BF16 = jnp.bfloat16

PAST_LEN = 2048
CHUNK = 64
A_BACK = 8
HEAD_DIM = 64
A_HEADS = 8
C_HEADS = 4
REL_MAX = 128
POOL_WINDOWS = (2, 4, 8, 16)
POOL_HIST = max(POOL_WINDOWS) - 1
POOL_PAD = POOL_HIST + 1
SUB_BLOCK = 32
SOFTMAX_ROWS = 32
TOKEN_SUB_ROWS = 256
LN_EPS = 1e-5
RMS_EPS = 1e-6
NEG_INF = -1e30
LOG2E = 1.4426950408889634

V7X_LANES = 128
V7X_VMEM_LIMIT = 56 * 1024 * 1024


def _dot(a, b):
    return jnp.dot(a, b, preferred_element_type=F32)


def _dot_nt(a, b):
    return lax.dot_general(a, b, (((1,), (1,)), ((), ())), preferred_element_type=F32)


def _dot_tn(a, b):
    return lax.dot_general(a, b, (((0,), (0,)), ((), ())), preferred_element_type=F32)


def _layer_norm(y, g, b):
    mu = jnp.mean(y, axis=-1, keepdims=True)
    d = y - mu
    var = jnp.mean(d * d, axis=-1, keepdims=True)
    return d * lax.rsqrt(var + LN_EPS) * g + b


def _sigmoid(x):
    return 1.0 / (1.0 + jnp.exp(-x))


def _log2(n):
    assert n > 0 and n & (n - 1) == 0, n
    return n.bit_length() - 1


def _div(x, n):
    return x >> _log2(n)


def _swiglu_stages(xb, wgu_ref, wdn_ref, n_split):
    hidden = wdn_ref.shape[0]
    hc = hidden // n_split
    gate_up = []
    for c in range(n_split):
        gate_up.append((_dot(xb, wgu_ref[:, c * hc:(c + 1) * hc]),
                        _dot(xb, wgu_ref[:, hidden + c * hc:hidden + (c + 1) * hc])))
        yield
    acc = None
    for c, (gate, up) in enumerate(gate_up):
        act = (gate * _sigmoid(gate) * up).astype(BF16)
        part = _dot(act, wdn_ref[c * hc:(c + 1) * hc, :])
        acc = part if acc is None else acc + part
        yield
    return acc


def _run_interleaved(chains):
    live = list(chains)
    while live:
        for ch in list(live):
            try:
                next(ch)
            except StopIteration:
                live.remove(ch)


def _sub_tiles(n_rows):
    sub = min(TOKEN_SUB_ROWS, n_rows)
    return [slice(r, r + sub) for r in range(0, n_rows, sub)]


def _pre_kernel(alpha, n_split, x_ref, wgu_ref, wdn_ref, lng_ref, lnb_ref, win_ref, xo_ref, z_ref):
    def chain(rows):
        x = x_ref[rows, :]
        ffn = yield from _swiglu_stages(x.astype(BF16), wgu_ref, wdn_ref, n_split)
        xn = _layer_norm(alpha * x + 0.5 * ffn, lng_ref[0:1, :], lnb_ref[0:1, :])
        xo_ref[rows, :] = xn
        z_ref[rows, :] = _dot(xn.astype(BF16), win_ref[...])
        yield

    _run_interleaved([chain(rows) for rows in _sub_tiles(x_ref.shape[0])])


def _post_kernel(alpha, n_split, x_ref, oa_ref, obc_ref, p_ref, wout_ref, wgu_ref, wdn_ref, wg_ref, wp_ref,
                 lng_ref, lnb_ref, xo_ref):
    wa = oa_ref.shape[1]

    def chain(rows):
        mixed = _dot(oa_ref[rows, :], wout_ref[0:wa, :]) + _dot(obc_ref[rows, :], wout_ref[wa:, :])
        proj = _dot(p_ref[rows, :].astype(BF16), wp_ref[...])
        yield
        x = _layer_norm(alpha * x_ref[rows, :] + mixed, lng_ref[1:2, :], lnb_ref[1:2, :])
        ffn = yield from _swiglu_stages(x.astype(BF16), wgu_ref, wdn_ref, n_split)
        x = _layer_norm(alpha * x + 0.5 * ffn, lng_ref[2:3, :], lnb_ref[2:3, :])
        emb = _sigmoid(_dot(x.astype(BF16), wg_ref[...])) * proj
        yield
        xo_ref[rows, :] = _layer_norm(alpha * x + emb, lng_ref[3:4, :], lnb_ref[3:4, :])

    _run_interleaved([chain(rows) for rows in _sub_tiles(x_ref.shape[0])])


def _const_spec(shape):
    return pl.BlockSpec(shape, lambda *_: (0,) * len(shape), pipeline_mode=pl.Buffered(1))


def _token_tile(n):
    for tm in (2 * TOKEN_SUB_ROWS, TOKEN_SUB_ROWS, 128, 64, 32, 16, 8):
        if n % tm == 0:
            return tm
    raise ValueError(f"unsupported token count {n}")


def _pre_call(x, wgu, wdn, lng, lnb, win, alpha):
    n, d = x.shape
    zw = win.shape[1]
    tm = _token_tile(n)
    row = lambda i: (i, 0)
    return pl.pallas_call(
        functools.partial(_pre_kernel, alpha, 2),
        grid=(n // tm,),
        in_specs=[pl.BlockSpec((tm, d), row), _const_spec(wgu.shape), _const_spec(wdn.shape),
                  _const_spec(lng.shape), _const_spec(lnb.shape), _const_spec(win.shape)],
        out_specs=[pl.BlockSpec((tm, d), row), pl.BlockSpec((tm, zw), row)],
        out_shape=[jax.ShapeDtypeStruct((n, d), F32), jax.ShapeDtypeStruct((n, zw), F32)],
        compiler_params=pltpu.CompilerParams(dimension_semantics=("parallel",),
                                             vmem_limit_bytes=V7X_VMEM_LIMIT),
        name="pre_mixer_tokens",
    )(x, wgu, wdn, lng, lnb, win)


def _post_call(x, oa, obc, p, wout, wgu, wdn, wg, wp, lng, lnb, alpha):
    n, d = x.shape
    tm = _token_tile(n)
    row = lambda i: (i, 0)
    return pl.pallas_call(
        functools.partial(_post_kernel, alpha, 2),
        grid=(n // tm,),
        in_specs=[pl.BlockSpec((tm, d), row), pl.BlockSpec((tm, oa.shape[1]), row),
                  pl.BlockSpec((tm, obc.shape[1]), row), pl.BlockSpec((tm, p.shape[1]), row),
                  _const_spec(wout.shape), _const_spec(wgu.shape), _const_spec(wdn.shape),
                  _const_spec(wg.shape), _const_spec(wp.shape), _const_spec(lng.shape), _const_spec(lnb.shape)],
        out_specs=pl.BlockSpec((tm, d), row),
        out_shape=jax.ShapeDtypeStruct((n, d), F32),
        compiler_params=pltpu.CompilerParams(dimension_semantics=("parallel",),
                                             vmem_limit_bytes=V7X_VMEM_LIMIT),
        name="post_mixer_tokens",
    )(x, oa, obc, p, wout, wgu, wdn, wg, wp, lng, lnb)


def _qk_scores(q_ref, kcat, s_out, r0, n_q, k0, n_k):
    scale = HEAD_DIM ** -0.5 * LOG2E
    low = lax.broadcasted_iota(jnp.int32, (1, V7X_LANES), 1) < HEAD_DIM
    for hp in range(q_ref.shape[1] // V7X_LANES):
        ls = slice(hp * V7X_LANES, (hp + 1) * V7X_LANES)
        q = q_ref[r0:r0 + n_q, ls] * scale
        lhs = jnp.concatenate([jnp.where(low, q, 0.0), jnp.where(low, 0.0, q)], axis=0).astype(BF16)
        s_out[2 * hp * n_q:2 * (hp + 1) * n_q, :] = _dot_nt(lhs, kcat[k0:k0 + n_k, ls])


def _softmax_weights(s_in, bias_ref, key_valid, p_out, inv_out):
    n_rows = s_in.shape[0]
    rb = SOFTMAX_ROWS
    for r in range(0, n_rows, rb):
        s = s_in[r:r + rb, :] + bias_ref[r:r + rb, :]
        if key_valid is not None:
            s = jnp.where(key_valid, s, NEG_INF)
        e = jnp.exp2(s - jnp.max(s, axis=-1, keepdims=True))
        p_out[r:r + rb, :] = e.astype(BF16)
        inv = 1.0 / jnp.sum(e, axis=-1, keepdims=True)
        inv_out[r:r + rb, :] = jnp.broadcast_to(inv, (rb, inv_out.shape[1]))


def _weighted_values(p_in, inv_in, vcat, o_ref, r0, n_q, k0, n_k):
    low = lax.broadcasted_iota(jnp.int32, (1, V7X_LANES), 1) < HEAD_DIM
    for hp in range(o_ref.shape[1] // V7X_LANES):
        ls = slice(hp * V7X_LANES, (hp + 1) * V7X_LANES)
        rows = slice(2 * hp * n_q, 2 * (hp + 1) * n_q)
        o2 = _dot(p_in[rows, :], vcat[k0:k0 + n_k, ls]) * inv_in[rows, :]
        o_ref[r0:r0 + n_q, ls] = jnp.where(low, o2[:n_q], o2[n_q:]).astype(o_ref.dtype)


def _build_rel_bias(table_ref, bias_ref, n_q, n_k, offset):
    n_heads = table_ref.shape[0]
    clip = lambda d: min(max(d, -(CHUNK - 1)), REL_MAX) + (CHUNK - 1)
    heads_per_pass = 4
    for j0 in range(0, n_k, V7X_LANES):
        jw = min(V7X_LANES, n_k - j0)
        r_lo, r_hi = clip(offset - (j0 + jw - 1)), clip(offset + n_q - 1 - j0)
        d = (offset - j0 + lax.broadcasted_iota(jnp.int32, (n_q, jw), 0)
             - lax.broadcasted_iota(jnp.int32, (n_q, jw), 1))
        idx = jnp.clip(d, -(CHUNK - 1), REL_MAX) + (CHUNK - 1)
        for h0 in range(0, n_heads, heads_per_pass):
            heads = range(h0, min(h0 + heads_per_pass, n_heads))
            if r_lo == r_hi:
                vals = [jnp.full((n_q, jw), table_ref[h, r_lo], F32) for h in heads]
            else:
                def pick(r, acc, heads=heads, idx=idx):
                    hit = idx == r
                    return tuple(jnp.where(hit, table_ref[h, r], a) for h, a in zip(heads, acc))
                vals = lax.fori_loop(r_lo, r_hi + 1, pick, tuple(jnp.zeros((n_q, jw), F32) for _ in heads))
            for h, v in zip(heads, vals):
                bias_ref[h * n_q:(h + 1) * n_q, j0:j0 + jw] = v * LOG2E


def _attn_prompt_kernel(qb_rows, table_ref, q_ref, kp_ref, kc_ref, vp_ref, vc_ref, o_ref,
                        kcat, vcat, bias_ref, s_buf, p_buf, inv_buf):
    i = pl.program_id(1)
    band = (A_BACK + 1) * CHUNK
    past = A_BACK * CHUNK
    n_chunks = qb_rows // CHUNK

    @pl.when((pl.program_id(0) == 0) & (i == 0))
    def _():
        _build_rel_bias(table_ref, bias_ref, CHUNK, band, past)

    def cast_rows(j, carry):
        r = pl.multiple_of(j * CHUNK, CHUNK)
        kcat[pl.ds(r, CHUNK), :] = kp_ref[pl.ds(r, CHUNK), :].astype(BF16)
        kcat[pl.ds(qb_rows + r, CHUNK), :] = kc_ref[pl.ds(r, CHUNK), :].astype(BF16)
        vcat[pl.ds(r, CHUNK), :] = vp_ref[pl.ds(r, CHUNK), :].astype(BF16)
        vcat[pl.ds(qb_rows + r, CHUNK), :] = vc_ref[pl.ds(r, CHUNK), :].astype(BF16)
        return carry

    lax.fori_loop(0, n_chunks, cast_rows, 0)
    key_idx = lax.broadcasted_iota(jnp.int32, (1, band), 1)

    _qk_scores(q_ref, kcat, s_buf.at[0], 0, CHUNK, 0, band)
    for c in range(n_chunks + 1):
        if c + 1 < n_chunks:
            _qk_scores(q_ref, kcat, s_buf.at[(c + 1) % 2], (c + 1) * CHUNK, CHUNK, (c + 1) * CHUNK, band)
        if c < n_chunks:
            key_valid = key_idx >= past - c * CHUNK - i * qb_rows
            _softmax_weights(s_buf.at[c % 2], bias_ref, key_valid, p_buf.at[c % 2], inv_buf.at[c % 2])
        if c >= 1:
            _weighted_values(p_buf.at[(c - 1) % 2], inv_buf.at[(c - 1) % 2], vcat, o_ref,
                             (c - 1) * CHUNK, CHUNK, (c - 1) * CHUNK, band)


def _attn_prompt_call(z, table, batch, seq, a_width):
    qb_rows = A_BACK * CHUNK
    assert seq % qb_rows == 0 and a_width % V7X_LANES == 0
    nq = seq // qb_rows
    band = (A_BACK + 1) * CHUNK
    rows = A_HEADS * CHUNK
    cur = lambda col: (lambda b, i: (b * nq + i, col))
    prev = lambda col: (lambda b, i: (b * nq + jnp.maximum(i - 1, 0), col))
    blk = (qb_rows, a_width)
    return pl.pallas_call(
        functools.partial(_attn_prompt_kernel, qb_rows),
        grid=(batch, nq),
        in_specs=[pl.BlockSpec(memory_space=pltpu.SMEM),
                  pl.BlockSpec(blk, cur(0)), pl.BlockSpec(blk, prev(1)), pl.BlockSpec(blk, cur(1)),
                  pl.BlockSpec(blk, prev(2)), pl.BlockSpec(blk, cur(2))],
        out_specs=pl.BlockSpec(blk, lambda b, i: (b * nq + i, 0)),
        out_shape=jax.ShapeDtypeStruct((batch * seq, a_width), BF16),
        scratch_shapes=[pltpu.VMEM((2 * qb_rows, a_width), BF16), pltpu.VMEM((2 * qb_rows, a_width), BF16),
                        pltpu.VMEM((rows, band), F32), pltpu.VMEM((2, rows, band), F32),
                        pltpu.VMEM((2, rows, band), BF16), pltpu.VMEM((2, rows, V7X_LANES), F32)],
        compiler_params=pltpu.CompilerParams(dimension_semantics=("arbitrary", "arbitrary"),
                                             vmem_limit_bytes=V7X_VMEM_LIMIT),
        name="band_attention_prompt",
    )(table, z, z, z, z, z)


def _attn_sample_kernel(table_ref, q_ref, k_ref, v_ref, ck_ref, cv_ref, o_ref, kcat, vcat, bias_ref,
                        s_buf, p_buf, inv_buf):
    lc = ck_ref.shape[1]
    t = q_ref.shape[0]

    @pl.when(pl.program_id(0) == 0)
    def _():
        _build_rel_bias(table_ref, bias_ref, t, lc + t, lc)

    kcat[0:lc, :] = ck_ref[0].astype(BF16)
    kcat[lc:, :] = k_ref[...].astype(BF16)
    vcat[0:lc, :] = cv_ref[0].astype(BF16)
    vcat[lc:, :] = v_ref[...].astype(BF16)
    _qk_scores(q_ref, kcat, s_buf, 0, t, 0, lc + t)
    _softmax_weights(s_buf, bias_ref, None, p_buf, inv_buf)
    _weighted_values(p_buf, inv_buf, vcat, o_ref, 0, t, 0, lc + t)


def _attn_sample_call(z, cache_k, cache_v, table, batch, t, a_width):
    lc = cache_k.shape[1]
    rows = A_HEADS * t
    blk = (t, a_width)
    cblk = (1, lc, a_width)
    return pl.pallas_call(
        _attn_sample_kernel,
        grid=(batch,),
        in_specs=[pl.BlockSpec(memory_space=pltpu.SMEM),
                  pl.BlockSpec(blk, lambda b: (b, 0)), pl.BlockSpec(blk, lambda b: (b, 1)),
                  pl.BlockSpec(blk, lambda b: (b, 2)),
                  pl.BlockSpec(cblk, lambda b: (b, 0, 0)), pl.BlockSpec(cblk, lambda b: (b, 0, 0))],
        out_specs=pl.BlockSpec(blk, lambda b: (b, 0)),
        out_shape=jax.ShapeDtypeStruct((batch * t, a_width), BF16),
        scratch_shapes=[pltpu.VMEM((lc + t, a_width), BF16), pltpu.VMEM((lc + t, a_width), BF16),
                        pltpu.VMEM((rows, lc + t), F32), pltpu.VMEM((rows, lc + t), F32),
                        pltpu.VMEM((rows, lc + t), BF16), pltpu.VMEM((rows, V7X_LANES), F32)],
        compiler_params=pltpu.CompilerParams(dimension_semantics=("arbitrary",),
                                             vmem_limit_bytes=V7X_VMEM_LIMIT),
        name="band_attention_sample",
    )(table, z, z, z, cache_k, cache_v)


def _split2(x):
    a = x.astype(BF16)
    return a, (x - a.astype(F32)).astype(BF16)


def _recur_kernel(layer, blk, pos0, n_heads, u_ref, q_ref, f_ref, i_ref, g_ref, hist0_ref, st0_ref, lbraw_ref,
                  normw_ref, poolw_ref, pscale_ref, o_ref, st_ref, state, hist, dmask):
    t_idx = pl.program_id(1)
    tm, width = u_ref.shape
    hd = width // n_heads
    sb = min(SUB_BLOCK, blk)
    assert blk // sb in (1, 2) and tm % blk == 0

    @pl.when(t_idx == 0)
    def _():
        state[...] = st0_ref[0]
        hist[...] = hist0_ref[0]
        ri = lax.broadcasted_iota(jnp.int32, (tm, tm), 0)
        ci = lax.broadcasted_iota(jnp.int32, (tm, tm), 1)
        same_sb = _div(ri, sb) == _div(ci, sb)
        dmask[0] = jnp.where(same_sb & (ci <= ri), 1.0, 0.0).astype(BF16)
        dmask[1] = jnp.where(same_sb, 1.0, 0.0).astype(BF16)

    ext = jnp.concatenate([hist[...], u_ref[...]], axis=0)
    hist[...] = ext[tm:, :]
    s2 = ext + pltpu.roll(ext, 1, 0)
    s4 = s2 + pltpu.roll(s2, 2, 0)
    s8 = s4 + pltpu.roll(s4, 4, 0)
    s16 = s8 + pltpu.roll(s8, 8, 0)
    lane = lax.broadcasted_iota(jnp.int32, (1, width), 1)
    grp = _div(lane, width // len(POOL_WINDOWS))
    wsum = jnp.where(grp == 0, s2, jnp.where(grp == 1, s4, jnp.where(grp == 2, s8, s16)))
    wlen = jnp.where(grp == 0, 2.0, jnp.where(grp == 1, 4.0, jnp.where(grp == 2, 8.0, 16.0)))
    row = lax.broadcasted_iota(jnp.int32, (POOL_PAD + tm, 1), 0)
    pos = (row + (pos0 - POOL_PAD) + t_idx * tm).astype(F32)
    cnt = jnp.maximum(jnp.minimum(pos + 1.0, wlen), 1.0)
    dev = (wsum / cnt - ext)[POOL_PAD:, :]
    o_pool = _dot(dev.astype(BF16), poolw_ref[...]) * pscale_ref[...]
    o_ref[:, 0:width] = o_pool.astype(o_ref.dtype)

    raw = lbraw_ref[...]
    sm = jnp.exp(raw - jnp.max(raw, axis=0, keepdims=True))
    sm = sm / jnp.sum(sm, axis=0, keepdims=True)
    lb = jnp.zeros((1, width), F32)
    for j in range(1, layer + 1):
        lb = lb + sm[j:j + 1, :]
    forget = lb + (1.0 - lb) * _sigmoid(f_ref[...])
    log_f = jnp.log(forget)
    k_in = 1.0 - forget
    qx = q_ref[...]
    qf = qx * _sigmoid(qx)
    vb = i_ref[...].astype(BF16)

    lf2 = jnp.concatenate(_split2(log_f), axis=1)

    def decay_sum(mask01):
        r = _dot(mask01, lf2)
        return r[:, 0:width] + r[:, width:]

    b_rel = decay_sum(dmask[0])
    sb_tot = decay_sum(dmask[1])
    if blk > sb:
        second = (lax.broadcasted_iota(jnp.int32, (tm, 1), 0) & (blk - 1)) >= sb
        prev_tot = pltpu.roll(sb_tot, sb, 0)
        next_tot = pltpu.roll(sb_tot, tm - sb, 0)
        b = b_rel + jnp.where(second, prev_tot, 0.0)
        blk_tot = sb_tot + jnp.where(second, prev_tot, next_tot)
    else:
        b, blk_tot = b_rel, sb_tot

    q_rel = qf * jnp.exp(b_rel)
    k_rel = (k_in * jnp.exp(-b_rel)).astype(BF16)
    k_end = (k_in * jnp.exp(sb_tot - b_rel)).astype(BF16)
    q_abs = (qf * jnp.exp(b)).astype(BF16)
    k_tail = (k_in * jnp.exp(blk_tot - b)).astype(BF16)

    hb = n_heads * blk
    stack_head = _div(lax.broadcasted_iota(jnp.int32, (hb, 1), 0), blk) == _div(lane, hd)
    tq = lax.broadcasted_iota(jnp.int32, (hb, blk), 0) & (blk - 1)
    ts = lax.broadcasted_iota(jnp.int32, (hb, blk), 1)
    m_intra = (_div(tq, sb) == _div(ts, sb)) & (ts <= tq)
    m_cross = _div(ts, sb) < _div(tq, sb)
    bd = (_div(lax.broadcasted_iota(jnp.int32, (width, width), 0), hd)
          == _div(lax.broadcasted_iota(jnp.int32, (width, width), 1), hd))

    outs = []
    for c in range(tm // blk):
        rs = slice(c * blk, (c + 1) * blk)
        lhs = jnp.where(stack_head, jnp.concatenate([q_rel[rs]] * n_heads, axis=0), 0.0).astype(BF16)
        a = jnp.where(m_intra, _dot_nt(lhs, k_rel[rs]), 0.0)
        if blk > sb:
            a = a + jnp.where(m_cross, _dot_nt(lhs, k_end[rs]), 0.0)
        stacked = jnp.where(stack_head, _dot(a.astype(BF16), vb[rs]), 0.0)
        o_c = stacked[0:blk]
        for h in range(1, n_heads):
            o_c = o_c + stacked[h * blk:(h + 1) * blk]
        st = state[...]
        outs.append(o_c + _dot_nt(q_abs[rs], st.astype(BF16)))
        decay_end = jnp.exp(blk_tot[c * blk:c * blk + 1, :])
        state[...] = st * decay_end + jnp.where(bd, _dot_tn(vb[rs], k_tail[rs]), 0.0)
    o = outs[0] if len(outs) == 1 else jnp.concatenate(outs, axis=0)

    ones_bd = jnp.where(bd, 1.0, 0.0).astype(BF16)
    sq_hi, sq_lo = _split2(o * o)
    ms = (_dot(sq_hi, ones_bd) + _dot(sq_lo, ones_bd)) * (1.0 / hd)
    gx = g_ref[...]
    o = o * lax.rsqrt(ms + RMS_EPS) * normw_ref[...] * (gx * _sigmoid(gx))
    o_ref[:, width:] = o.astype(o_ref.dtype)

    @pl.when(t_idx == pl.num_programs(1) - 1)
    def _():
        st_ref[0] = state[...]


def _recur_call(z, hist0, st0, lbraw, normw, poolw, pscale, layer, batch, seq, blk, pos0, col0):
    width = st0.shape[-1]
    tm = min(256, seq)
    assert seq % tm == 0 and tm % blk == 0
    nt = seq // tm
    col = lambda c: (lambda b, t: (b * nt + t, c))
    per_batch = lambda b, t: (b, 0, 0)
    return pl.pallas_call(
        functools.partial(_recur_kernel, layer, blk, pos0, C_HEADS),
        grid=(batch, nt),
        in_specs=[pl.BlockSpec((tm, width), col(col0 + j)) for j in range(5)]
        + [pl.BlockSpec((1, POOL_PAD, width), per_batch), pl.BlockSpec((1, width, width), per_batch),
           _const_spec(lbraw.shape), _const_spec(normw.shape), _const_spec(poolw.shape), _const_spec(pscale.shape)],
        out_specs=[pl.BlockSpec((tm, 2 * width), lambda b, t: (b * nt + t, 0)),
                   pl.BlockSpec((1, width, width), per_batch)],
        out_shape=[jax.ShapeDtypeStruct((batch * seq, 2 * width), BF16),
                   jax.ShapeDtypeStruct((batch, width, width), F32)],
        scratch_shapes=[pltpu.VMEM((width, width), F32), pltpu.VMEM((POOL_PAD, width), F32),
                        pltpu.VMEM((2, tm, tm), BF16)],
        compiler_params=pltpu.CompilerParams(dimension_semantics=("parallel", "arbitrary"),
                                             vmem_limit_bytes=V7X_VMEM_LIMIT),
        name="pool_hgrn_mixer",
    )(z, z, z, z, z, hist0, st0, lbraw, normw, poolw, pscale)


def _block_diag(blocks):
    g, a, b = blocks.shape
    eye = jnp.eye(g, dtype=blocks.dtype)
    return (eye[:, None, :, None] * blocks[:, :, None, :]).reshape(g * a, g * b)


def _state_to_kernel(s):
    return jax.vmap(_block_diag)(jnp.swapaxes(s, -1, -2))


def _state_from_kernel(st, n_heads):
    b, w, _ = st.shape
    hd = w // n_heads
    blocks = jnp.stack([st[:, h * hd:(h + 1) * hd, h * hd:(h + 1) * hd] for h in range(n_heads)], axis=1)
    return jnp.swapaxes(blocks, -1, -2)


def kernel(x_prompt, x_sample, p_prompt, p_sample, cache_attn_k, cache_attn_v, state_pool, state_hgrn, ffn1_w_gu, ffn1_w_down, w_in, attn_rel_bias, pool_w, pool_scale, hgrn_lower_bounds, hgrn_norm_w, w_out, ffn2_w_gu, ffn2_w_down, ple_w_gate, ple_w_proj, ln_g, ln_b):
    depth = w_in.shape[0]
    alpha = float((2 * depth) ** 0.25)
    nb, seq, d = x_prompt.shape
    db, dt, _ = x_sample.shape
    a_width = A_HEADS * HEAD_DIM
    c_width = C_HEADS * HEAD_DIM
    in_width = w_in.shape[-1]
    col0 = 3 * a_width // c_width
    lc = cache_attn_k.shape[2]
    past_rows = min(A_BACK * CHUNK, seq)
    assert dt >= POOL_HIST and seq >= POOL_HIST and PAST_LEN >= POOL_HIST

    bf = lambda w: w.astype(BF16)
    w1gu, w1dn, win, wout = bf(ffn1_w_gu), bf(ffn1_w_down), bf(w_in), bf(w_out)
    w2gu, w2dn, wg, wp = bf(ffn2_w_gu), bf(ffn2_w_down), bf(ple_w_gate), bf(ple_w_proj)

    xp = x_prompt.reshape(nb * seq, d)
    xs = x_sample.reshape(db * dt, d)
    zero_hist = jnp.zeros((nb, POOL_PAD, c_width), F32)
    zero_state = jnp.zeros((nb, c_width, c_width), F32)
    lbraw = hgrn_lower_bounds.astype(F32)
    outs = [[] for _ in range(8)]

    for i in range(depth):
        poolw = bf(_block_diag(pool_w[i]))
        pscale = pool_scale[i][None, :]
        normw = jnp.tile(hgrn_norm_w[i], C_HEADS)[None, :]
        rel_table = attn_rel_bias[i].astype(F32)

        xp, zp = _pre_call(xp, w1gu[i], w1dn[i], ln_g[i], ln_b[i], win[i], alpha)
        oa = _attn_prompt_call(zp, rel_table, nb, seq, a_width)
        obc, st = _recur_call(zp, zero_hist, zero_state, lbraw, normw, poolw, pscale, i, nb, seq, CHUNK, 0, col0)
        xp = _post_call(xp, oa, obc, p_prompt[i].reshape(nb * seq, -1), wout[i], w2gu[i], w2dn[i], wg[i], wp[i],
                        ln_g[i], ln_b[i], alpha)
        zp3 = zp.reshape(nb, seq, in_width)
        outs[0].append(zp3[:, seq - past_rows:, a_width:2 * a_width].reshape(nb, past_rows, A_HEADS, HEAD_DIM))
        outs[1].append(zp3[:, seq - past_rows:, 2 * a_width:3 * a_width].reshape(nb, past_rows, A_HEADS, HEAD_DIM))
        outs[2].append(zp3[:, seq - POOL_HIST:, 3 * a_width:3 * a_width + c_width])
        outs[3].append(_state_from_kernel(st, C_HEADS))

        xs, zs = _pre_call(xs, w1gu[i], w1dn[i], ln_g[i], ln_b[i], win[i], alpha)
        oa = _attn_sample_call(zs, cache_attn_k[i].reshape(db, lc, a_width), cache_attn_v[i].reshape(db, lc, a_width),
                               rel_table, db, dt, a_width)
        hist0 = jnp.pad(state_pool[i].astype(F32), ((0, 0), (POOL_PAD - POOL_HIST, 0), (0, 0)))
        obc, st = _recur_call(zs, hist0, _state_to_kernel(state_hgrn[i].astype(F32)), lbraw, normw, poolw, pscale,
                              i, db, dt, dt, PAST_LEN, col0)
        xs = _post_call(xs, oa, obc, p_sample[i].reshape(db * dt, -1), wout[i], w2gu[i], w2dn[i], wg[i], wp[i],
                        ln_g[i], ln_b[i], alpha)
        zs3 = zs.reshape(db, dt, in_width)
        outs[4].append(zs3[:, :, a_width:2 * a_width].reshape(db, dt, A_HEADS, HEAD_DIM))
        outs[5].append(zs3[:, :, 2 * a_width:3 * a_width].reshape(db, dt, A_HEADS, HEAD_DIM))
        outs[6].append(zs3[:, dt - POOL_HIST:, 3 * a_width:3 * a_width + c_width])
        outs[7].append(_state_from_kernel(st, C_HEADS))

    stacked = [jnp.stack(o, axis=0) for o in outs]
    return (xp.reshape(nb, seq, d), xs.reshape(db, dt, d), *stacked)
```

```python
import functools

import jax
import jax.numpy as jnp
from jax import lax
from jax.experimental import pallas as pl
from jax.experimental.pallas import tpu as pltpu

F32 = jnp.float32
BF16 = jnp.bfloat16

PAST_LEN = 2048
CHUNK = 64
A_BACK = 8
HEAD_DIM = 64
A_HEADS = 8
C_HEADS = 4
REL_MAX = 128
POOL_WINDOWS = (2, 4, 8, 16)
POOL_HIST = max(POOL_WINDOWS) - 1
POOL_PAD = POOL_HIST + 1
SUB_BLOCK = 32
SOFTMAX_ROWS = 32
TOKEN_SUB_ROWS = 256
LN_EPS = 1e-5
RMS_EPS = 1e-6
NEG_INF = -1e30
LOG2E = 1.4426950408889634

V7X_LANES = 128
V7X_VMEM_LIMIT = 56 * 1024 * 1024


def _dot(a, b):
    return jnp.dot(a, b, preferred_element_type=F32)


def _dot_nt(a, b):
    return lax.dot_general(a, b, (((1,), (1,)), ((), ())), preferred_element_type=F32)


def _dot_tn(a, b):
    return lax.dot_general(a, b, (((0,), (0,)), ((), ())), preferred_element_type=F32)


def _layer_norm(y, g, b):
    mu = jnp.mean(y, axis=-1, keepdims=True)
    d = y - mu
    var = jnp.mean(d * d, axis=-1, keepdims=True)
    return d * lax.rsqrt(var + LN_EPS) * g + b


def _sigmoid(x):
    return 1.0 / (1.0 + jnp.exp(-x))


def _log2(n):
    assert n > 0 and n & (n - 1) == 0, n
    return n.bit_length() - 1


def _div(x, n):
    return x >> _log2(n)


def _swiglu_stages(xb, wgu_ref, wdn_ref, n_split):
    hidden = wdn_ref.shape[0]
    hc = hidden // n_split
    gate_up = []
    for c in range(n_split):
        gate_up.append((_dot(xb, wgu_ref[:, c * hc:(c + 1) * hc]),
                        _dot(xb, wgu_ref[:, hidden + c * hc:hidden + (c + 1) * hc])))
        yield
    acc = None
    for c, (gate, up) in enumerate(gate_up):
        act = (gate * _sigmoid(gate) * up).astype(BF16)
        part = _dot(act, wdn_ref[c * hc:(c + 1) * hc, :])
        acc = part if acc is None else acc + part
        yield
    return acc


def _run_interleaved(chains):
    live = list(chains)
    while live:
        for ch in list(live):
            try:
                next(ch)
            except StopIteration:
                live.remove(ch)


def _sub_tiles(n_rows):
    sub = min(TOKEN_SUB_ROWS, n_rows)
    return [slice(r, r + sub) for r in range(0, n_rows, sub)]


def _pre_kernel(alpha, n_split, x_ref, wgu_ref, wdn_ref, lng_ref, lnb_ref, win_ref, xo_ref, z_ref):
    def chain(rows):
        x = x_ref[rows, :]
        ffn = yield from _swiglu_stages(x.astype(BF16), wgu_ref, wdn_ref, n_split)
        xn = _layer_norm(alpha * x + 0.5 * ffn, lng_ref[0:1, :], lnb_ref[0:1, :])
        xo_ref[rows, :] = xn
        z_ref[rows, :] = _dot(xn.astype(BF16), win_ref[...])
        yield

    _run_interleaved([chain(rows) for rows in _sub_tiles(x_ref.shape[0])])


def _post_kernel(alpha, n_split, x_ref, oa_ref, obc_ref, p_ref, wout_ref, wgu_ref, wdn_ref, wg_ref, wp_ref,
                 lng_ref, lnb_ref, xo_ref):
    wa = oa_ref.shape[1]

    def chain(rows):
        mixed = _dot(oa_ref[rows, :], wout_ref[0:wa, :]) + _dot(obc_ref[rows, :], wout_ref[wa:, :])
        proj = _dot(p_ref[rows, :].astype(BF16), wp_ref[...])
        yield
        x = _layer_norm(alpha * x_ref[rows, :] + mixed, lng_ref[1:2, :], lnb_ref[1:2, :])
        ffn = yield from _swiglu_stages(x.astype(BF16), wgu_ref, wdn_ref, n_split)
        x = _layer_norm(alpha * x + 0.5 * ffn, lng_ref[2:3, :], lnb_ref[2:3, :])
        emb = _sigmoid(_dot(x.astype(BF16), wg_ref[...])) * proj
        yield
        xo_ref[rows, :] = _layer_norm(alpha * x + emb, lng_ref[3:4, :], lnb_ref[3:4, :])

    _run_interleaved([chain(rows) for rows in _sub_tiles(x_ref.shape[0])])


def _layer_spec(stacked, layer):
    rest = stacked.shape[1:]
    return pl.BlockSpec((None,) + rest, lambda *_: (layer,) + (0,) * len(rest), pipeline_mode=pl.Buffered(1))


def _token_tile(n):
    for tm in (2 * TOKEN_SUB_ROWS, TOKEN_SUB_ROWS, 128, 64, 32, 16, 8):
        if n % tm == 0:
            return tm
    raise ValueError(f"unsupported token count {n}")


def _pre_call(x, wgu, wdn, lng, lnb, win, layer, alpha):
    n, d = x.shape
    zw = win.shape[-1]
    tm = _token_tile(n)
    row = lambda i: (i, 0)
    consts = (wgu, wdn, lng, lnb, win)
    return pl.pallas_call(
        functools.partial(_pre_kernel, alpha, 2),
        grid=(n // tm,),
        in_specs=[pl.BlockSpec((tm, d), row)] + [_layer_spec(c, layer) for c in consts],
        out_specs=[pl.BlockSpec((tm, d), row), pl.BlockSpec((tm, zw), row)],
        out_shape=[jax.ShapeDtypeStruct((n, d), F32), jax.ShapeDtypeStruct((n, zw), F32)],
        compiler_params=pltpu.CompilerParams(dimension_semantics=("parallel",),
                                             vmem_limit_bytes=V7X_VMEM_LIMIT),
        name="pre_mixer_tokens",
    )(x, *consts)


def _post_call(x, oa, obc, p, wout, wgu, wdn, wg, wp, lng, lnb, layer, alpha):
    n, d = x.shape
    tm = _token_tile(n)
    row = lambda i: (i, 0)
    consts = (wout, wgu, wdn, wg, wp, lng, lnb)
    return pl.pallas_call(
        functools.partial(_post_kernel, alpha, 2),
        grid=(n // tm,),
        in_specs=[pl.BlockSpec((tm, d), row), pl.BlockSpec((tm, oa.shape[1]), row),
                  pl.BlockSpec((tm, obc.shape[1]), row),
                  pl.BlockSpec((None, tm, p.shape[-1]), lambda i: (layer, i, 0))]
        + [_layer_spec(c, layer) for c in consts],
        out_specs=pl.BlockSpec((tm, d), row),
        out_shape=jax.ShapeDtypeStruct((n, d), F32),
        compiler_params=pltpu.CompilerParams(dimension_semantics=("parallel",),
                                             vmem_limit_bytes=V7X_VMEM_LIMIT),
        name="post_mixer_tokens",
    )(x, oa, obc, p, *consts)


def _qk_scores(q_ref, kcat, s_out, r0, n_q, k0, n_k):
    scale = HEAD_DIM ** -0.5 * LOG2E
    low = lax.broadcasted_iota(jnp.int32, (1, V7X_LANES), 1) < HEAD_DIM
    for hp in range(q_ref.shape[1] // V7X_LANES):
        ls = slice(hp * V7X_LANES, (hp + 1) * V7X_LANES)
        q = q_ref[r0:r0 + n_q, ls] * scale
        lhs = jnp.concatenate([jnp.where(low, q, 0.0), jnp.where(low, 0.0, q)], axis=0).astype(BF16)
        s_out[2 * hp * n_q:2 * (hp + 1) * n_q, :] = _dot_nt(lhs, kcat[k0:k0 + n_k, ls])


def _softmax_weights(s_in, bias_ref, key_valid, p_out, inv_out):
    n_rows = s_in.shape[0]
    rb = SOFTMAX_ROWS
    for r in range(0, n_rows, rb):
        s = s_in[r:r + rb, :] + bias_ref[r:r + rb, :]
        if key_valid is not None:
            s = jnp.where(key_valid, s, NEG_INF)
        e = jnp.exp2(s - jnp.max(s, axis=-1, keepdims=True))
        p_out[r:r + rb, :] = e.astype(BF16)
        inv = 1.0 / jnp.sum(e, axis=-1, keepdims=True)
        inv_out[r:r + rb, :] = jnp.broadcast_to(inv, (rb, inv_out.shape[1]))


def _weighted_values(p_in, inv_in, vcat, o_ref, r0, n_q, k0, n_k):
    low = lax.broadcasted_iota(jnp.int32, (1, V7X_LANES), 1) < HEAD_DIM
    for hp in range(o_ref.shape[1] // V7X_LANES):
        ls = slice(hp * V7X_LANES, (hp + 1) * V7X_LANES)
        rows = slice(2 * hp * n_q, 2 * (hp + 1) * n_q)
        o2 = _dot(p_in[rows, :], vcat[k0:k0 + n_k, ls]) * inv_in[rows, :]
        o_ref[r0:r0 + n_q, ls] = jnp.where(low, o2[:n_q], o2[n_q:]).astype(o_ref.dtype)


def _build_rel_bias(table_ref, bias_ref, n_q, n_k, offset):
    n_heads = table_ref.shape[0]
    clip = lambda d: min(max(d, -(CHUNK - 1)), REL_MAX) + (CHUNK - 1)
    heads_per_pass = 4
    for j0 in range(0, n_k, V7X_LANES):
        jw = min(V7X_LANES, n_k - j0)
        r_lo, r_hi = clip(offset - (j0 + jw - 1)), clip(offset + n_q - 1 - j0)
        d = (offset - j0 + lax.broadcasted_iota(jnp.int32, (n_q, jw), 0)
             - lax.broadcasted_iota(jnp.int32, (n_q, jw), 1))
        idx = jnp.clip(d, -(CHUNK - 1), REL_MAX) + (CHUNK - 1)
        for h0 in range(0, n_heads, heads_per_pass):
            heads = range(h0, min(h0 + heads_per_pass, n_heads))
            if r_lo == r_hi:
                vals = [jnp.full((n_q, jw), table_ref[h, r_lo], F32) for h in heads]
            else:
                def pick(r, acc, heads=heads, idx=idx):
                    hit = idx == r
                    return tuple(jnp.where(hit, table_ref[h, r], a) for h, a in zip(heads, acc))
                vals = lax.fori_loop(r_lo, r_hi + 1, pick, tuple(jnp.zeros((n_q, jw), F32) for _ in heads))
            for h, v in zip(heads, vals):
                bias_ref[h * n_q:(h + 1) * n_q, j0:j0 + jw] = v * LOG2E


def _attn_prompt_kernel(qb_rows, table_ref, q_ref, kp_ref, kc_ref, vp_ref, vc_ref, o_ref,
                        kcat, vcat, bias_ref, s_buf, p_buf, inv_buf):
    i = pl.program_id(1)
    band = (A_BACK + 1) * CHUNK
    past = A_BACK * CHUNK
    n_chunks = qb_rows // CHUNK

    @pl.when((pl.program_id(0) == 0) & (i == 0))
    def _():
        _build_rel_bias(table_ref, bias_ref, CHUNK, band, past)

    def cast_rows(j, carry):
        r = pl.multiple_of(j * CHUNK, CHUNK)
        kcat[pl.ds(r, CHUNK), :] = kp_ref[pl.ds(r, CHUNK), :].astype(BF16)
        kcat[pl.ds(qb_rows + r, CHUNK), :] = kc_ref[pl.ds(r, CHUNK), :].astype(BF16)
        vcat[pl.ds(r, CHUNK), :] = vp_ref[pl.ds(r, CHUNK), :].astype(BF16)
        vcat[pl.ds(qb_rows + r, CHUNK), :] = vc_ref[pl.ds(r, CHUNK), :].astype(BF16)
        return carry

    lax.fori_loop(0, n_chunks, cast_rows, 0)
    key_idx = lax.broadcasted_iota(jnp.int32, (1, band), 1)

    _qk_scores(q_ref, kcat, s_buf.at[0], 0, CHUNK, 0, band)
    for c in range(n_chunks + 1):
        if c + 1 < n_chunks:
            _qk_scores(q_ref, kcat, s_buf.at[(c + 1) % 2], (c + 1) * CHUNK, CHUNK, (c + 1) * CHUNK, band)
        if c < n_chunks:
            key_valid = key_idx >= past - c * CHUNK - i * qb_rows
            _softmax_weights(s_buf.at[c % 2], bias_ref, key_valid, p_buf.at[c % 2], inv_buf.at[c % 2])
        if c >= 1:
            _weighted_values(p_buf.at[(c - 1) % 2], inv_buf.at[(c - 1) % 2], vcat, o_ref,
                             (c - 1) * CHUNK, CHUNK, (c - 1) * CHUNK, band)


def _attn_prompt_call(z, table, batch, seq, a_width):
    qb_rows = A_BACK * CHUNK
    assert seq % qb_rows == 0 and a_width % V7X_LANES == 0
    nq = seq // qb_rows
    band = (A_BACK + 1) * CHUNK
    rows = A_HEADS * CHUNK
    cur = lambda col: (lambda b, i: (b * nq + i, col))
    prev = lambda col: (lambda b, i: (b * nq + jnp.maximum(i - 1, 0), col))
    blk = (qb_rows, a_width)
    return pl.pallas_call(
        functools.partial(_attn_prompt_kernel, qb_rows),
        grid=(batch, nq),
        in_specs=[pl.BlockSpec(memory_space=pltpu.SMEM),
                  pl.BlockSpec(blk, cur(0)), pl.BlockSpec(blk, prev(1)), pl.BlockSpec(blk, cur(1)),
                  pl.BlockSpec(blk, prev(2)), pl.BlockSpec(blk, cur(2))],
        out_specs=pl.BlockSpec(blk, lambda b, i: (b * nq + i, 0)),
        out_shape=jax.ShapeDtypeStruct((batch * seq, a_width), BF16),
        scratch_shapes=[pltpu.VMEM((2 * qb_rows, a_width), BF16), pltpu.VMEM((2 * qb_rows, a_width), BF16),
                        pltpu.VMEM((rows, band), F32), pltpu.VMEM((2, rows, band), F32),
                        pltpu.VMEM((2, rows, band), BF16), pltpu.VMEM((2, rows, V7X_LANES), F32)],
        compiler_params=pltpu.CompilerParams(dimension_semantics=("arbitrary", "arbitrary"),
                                             vmem_limit_bytes=V7X_VMEM_LIMIT),
        name="band_attention_prompt",
    )(table, z, z, z, z, z)


def _attn_sample_kernel(table_ref, q_ref, k_ref, v_ref, ck_ref, cv_ref, o_ref, kcat, vcat, bias_ref,
                        s_buf, p_buf, inv_buf):
    lc = ck_ref.shape[0]
    t = q_ref.shape[0]

    @pl.when(pl.program_id(0) == 0)
    def _():
        _build_rel_bias(table_ref, bias_ref, t, lc + t, lc)

    kcat[0:lc, :] = ck_ref[...].astype(BF16)
    kcat[lc:, :] = k_ref[...].astype(BF16)
    vcat[0:lc, :] = cv_ref[...].astype(BF16)
    vcat[lc:, :] = v_ref[...].astype(BF16)
    _qk_scores(q_ref, kcat, s_buf, 0, t, 0, lc + t)
    _softmax_weights(s_buf, bias_ref, None, p_buf, inv_buf)
    _weighted_values(p_buf, inv_buf, vcat, o_ref, 0, t, 0, lc + t)


def _attn_sample_call(z, cache_k, cache_v, table, layer, batch, t, a_width):
    lc = cache_k.shape[2]
    rows = A_HEADS * t
    blk = (t, a_width)
    cblk = (None, None, lc, a_width)
    cidx = lambda b: (layer, b, 0, 0)
    return pl.pallas_call(
        _attn_sample_kernel,
        grid=(batch,),
        in_specs=[pl.BlockSpec(memory_space=pltpu.SMEM),
                  pl.BlockSpec(blk, lambda b: (b, 0)), pl.BlockSpec(blk, lambda b: (b, 1)),
                  pl.BlockSpec(blk, lambda b: (b, 2)),
                  pl.BlockSpec(cblk, cidx), pl.BlockSpec(cblk, cidx)],
        out_specs=pl.BlockSpec(blk, lambda b: (b, 0)),
        out_shape=jax.ShapeDtypeStruct((batch * t, a_width), BF16),
        scratch_shapes=[pltpu.VMEM((lc + t, a_width), BF16), pltpu.VMEM((lc + t, a_width), BF16),
                        pltpu.VMEM((rows, lc + t), F32), pltpu.VMEM((rows, lc + t), F32),
                        pltpu.VMEM((rows, lc + t), BF16), pltpu.VMEM((rows, V7X_LANES), F32)],
        compiler_params=pltpu.CompilerParams(dimension_semantics=("arbitrary",),
                                             vmem_limit_bytes=V7X_VMEM_LIMIT),
        name="band_attention_sample",
    )(table, z, z, z, cache_k, cache_v)


def _split2(x):
    a = x.astype(BF16)
    return a, (x - a.astype(F32)).astype(BF16)


def _recur_kernel(layer, blk, pos0, n_heads, u_ref, q_ref, f_ref, i_ref, g_ref, hist0_ref, st0_ref, lbraw_ref,
                  normw_ref, poolw_ref, pscale_ref, o_ref, st_ref, state, hist, dmask):
    t_idx = pl.program_id(1)
    tm, width = u_ref.shape
    hd = width // n_heads
    sb = min(SUB_BLOCK, blk)
    assert blk // sb in (1, 2) and tm % blk == 0

    @pl.when(t_idx == 0)
    def _():
        state[...] = st0_ref[...]
        hist[...] = hist0_ref[...]
        ri = lax.broadcasted_iota(jnp.int32, (tm, tm), 0)
        ci = lax.broadcasted_iota(jnp.int32, (tm, tm), 1)
        same_sb = _div(ri, sb) == _div(ci, sb)
        dmask[0] = jnp.where(same_sb & (ci <= ri), 1.0, 0.0).astype(BF16)
        dmask[1] = jnp.where(same_sb, 1.0, 0.0).astype(BF16)

    ext = jnp.concatenate([hist[...], u_ref[...]], axis=0)
    hist[...] = ext[tm:, :]
    s2 = ext + pltpu.roll(ext, 1, 0)
    s4 = s2 + pltpu.roll(s2, 2, 0)
    s8 = s4 + pltpu.roll(s4, 4, 0)
    s16 = s8 + pltpu.roll(s8, 8, 0)
    lane = lax.broadcasted_iota(jnp.int32, (1, width), 1)
    grp = _div(lane, width // len(POOL_WINDOWS))
    wsum = jnp.where(grp == 0, s2, jnp.where(grp == 1, s4, jnp.where(grp == 2, s8, s16)))
    wlen = jnp.where(grp == 0, 2.0, jnp.where(grp == 1, 4.0, jnp.where(grp == 2, 8.0, 16.0)))
    row = lax.broadcasted_iota(jnp.int32, (POOL_PAD + tm, 1), 0)
    pos = (row + (pos0 - POOL_PAD) + t_idx * tm).astype(F32)
    cnt = jnp.maximum(jnp.minimum(pos + 1.0, wlen), 1.0)
    dev = (wsum / cnt - ext)[POOL_PAD:, :]
    o_pool = _dot(dev.astype(BF16), poolw_ref[...]) * pscale_ref[...]
    o_ref[:, 0:width] = o_pool.astype(o_ref.dtype)

    raw = lbraw_ref[...]
    sm = jnp.exp(raw - jnp.max(raw, axis=0, keepdims=True))
    sm = sm / jnp.sum(sm, axis=0, keepdims=True)
    lb = jnp.zeros((1, width), F32)
    for j in range(1, layer + 1):
        lb = lb + sm[j:j + 1, :]
    forget = lb + (1.0 - lb) * _sigmoid(f_ref[...])
    log_f = jnp.log(forget)
    k_in = 1.0 - forget
    qx = q_ref[...]
    qf = qx * _sigmoid(qx)
    vb = i_ref[...].astype(BF16)

    lf2 = jnp.concatenate(_split2(log_f), axis=1)

    def decay_sum(mask01):
        r = _dot(mask01, lf2)
        return r[:, 0:width] + r[:, width:]

    b_rel = decay_sum(dmask[0])
    sb_tot = decay_sum(dmask[1])
    if blk > sb:
        second = (lax.broadcasted_iota(jnp.int32, (tm, 1), 0) & (blk - 1)) >= sb
        prev_tot = pltpu.roll(sb_tot, sb, 0)
        next_tot = pltpu.roll(sb_tot, tm - sb, 0)
        b = b_rel + jnp.where(second, prev_tot, 0.0)
        blk_tot = sb_tot + jnp.where(second, prev_tot, next_tot)
    else:
        b, blk_tot = b_rel, sb_tot

    q_rel = qf * jnp.exp(b_rel)
    k_rel = (k_in * jnp.exp(-b_rel)).astype(BF16)
    k_end = (k_in * jnp.exp(sb_tot - b_rel)).astype(BF16)
    q_abs = (qf * jnp.exp(b)).astype(BF16)
    k_tail = (k_in * jnp.exp(blk_tot - b)).astype(BF16)

    hb = n_heads * blk
    stack_head = _div(lax.broadcasted_iota(jnp.int32, (hb, 1), 0), blk) == _div(lane, hd)
    tq = lax.broadcasted_iota(jnp.int32, (hb, blk), 0) & (blk - 1)
    ts = lax.broadcasted_iota(jnp.int32, (hb, blk), 1)
    m_intra = (_div(tq, sb) == _div(ts, sb)) & (ts <= tq)
    m_cross = _div(ts, sb) < _div(tq, sb)
    bd = (_div(lax.broadcasted_iota(jnp.int32, (width, width), 0), hd)
          == _div(lax.broadcasted_iota(jnp.int32, (width, width), 1), hd))

    outs = []
    for c in range(tm // blk):
        rs = slice(c * blk, (c + 1) * blk)
        lhs = jnp.where(stack_head, jnp.concatenate([q_rel[rs]] * n_heads, axis=0), 0.0).astype(BF16)
        a = jnp.where(m_intra, _dot_nt(lhs, k_rel[rs]), 0.0)
        if blk > sb:
            a = a + jnp.where(m_cross, _dot_nt(lhs, k_end[rs]), 0.0)
        stacked = jnp.where(stack_head, _dot(a.astype(BF16), vb[rs]), 0.0)
        o_c = stacked[0:blk]
        for h in range(1, n_heads):
            o_c = o_c + stacked[h * blk:(h + 1) * blk]
        st = state[...]
        outs.append(o_c + _dot_nt(q_abs[rs], st.astype(BF16)))
        decay_end = jnp.exp(blk_tot[c * blk:c * blk + 1, :])
        state[...] = st * decay_end + jnp.where(bd, _dot_tn(vb[rs], k_tail[rs]), 0.0)
    o = outs[0] if len(outs) == 1 else jnp.concatenate(outs, axis=0)

    ones_bd = jnp.where(bd, 1.0, 0.0).astype(BF16)
    sq_hi, sq_lo = _split2(o * o)
    ms = (_dot(sq_hi, ones_bd) + _dot(sq_lo, ones_bd)) * (1.0 / hd)
    gx = g_ref[...]
    o = o * lax.rsqrt(ms + RMS_EPS) * normw_ref[...] * (gx * _sigmoid(gx))
    o_ref[:, width:] = o.astype(o_ref.dtype)

    @pl.when(t_idx == pl.num_programs(1) - 1)
    def _():
        st_ref[...] = state[...]


def _recur_call(z, hist0, st0, lbraw, normw, poolw, pscale, layer, state_layer, batch, seq, blk, pos0, col0):
    width = st0.shape[-1]
    tm = min(256, seq)
    assert seq % tm == 0 and tm % blk == 0
    nt = seq // tm
    col = lambda c: (lambda b, t: (b * nt + t, c))
    per_batch = lambda b, t: (state_layer, b, 0, 0)
    consts = (normw, poolw, pscale)
    return pl.pallas_call(
        functools.partial(_recur_kernel, layer, blk, pos0, C_HEADS),
        grid=(batch, nt),
        in_specs=[pl.BlockSpec((tm, width), col(col0 + j)) for j in range(5)]
        + [pl.BlockSpec((None, None, POOL_PAD, width), per_batch),
           pl.BlockSpec((None, None, width, width), per_batch),
           pl.BlockSpec(lbraw.shape, lambda b, t: (0, 0), pipeline_mode=pl.Buffered(1))]
        + [_layer_spec(c, layer) for c in consts],
        out_specs=[pl.BlockSpec((tm, 2 * width), lambda b, t: (b * nt + t, 0)),
                   pl.BlockSpec((None, width, width), lambda b, t: (b, 0, 0))],
        out_shape=[jax.ShapeDtypeStruct((batch * seq, 2 * width), BF16),
                   jax.ShapeDtypeStruct((batch, width, width), F32)],
        scratch_shapes=[pltpu.VMEM((width, width), F32), pltpu.VMEM((POOL_PAD, width), F32),
                        pltpu.VMEM((2, tm, tm), BF16)],
        compiler_params=pltpu.CompilerParams(dimension_semantics=("parallel", "arbitrary"),
                                             vmem_limit_bytes=V7X_VMEM_LIMIT),
        name="pool_hgrn_mixer",
    )(z, z, z, z, z, hist0, st0, lbraw, *consts)


def _block_diag(blocks):
    g = blocks.shape[-3]
    zero = jnp.zeros_like(blocks[..., 0, :, :])
    rows = [jnp.concatenate([blocks[..., h, :, :] if j == h else zero for j in range(g)], axis=-1)
            for h in range(g)]
    return jnp.concatenate(rows, axis=-2)


def _state_from_kernel(st, n_heads):
    b, w, _ = st.shape
    hd = w // n_heads
    blocks = jnp.stack([st[:, h * hd:(h + 1) * hd, h * hd:(h + 1) * hd] for h in range(n_heads)], axis=1)
    return jnp.swapaxes(blocks, -1, -2)


def kernel(x_prompt, x_sample, p_prompt, p_sample, cache_attn_k, cache_attn_v, state_pool, state_hgrn, ffn1_w_gu, ffn1_w_down, w_in, attn_rel_bias, pool_w, pool_scale, hgrn_lower_bounds, hgrn_norm_w, w_out, ffn2_w_gu, ffn2_w_down, ple_w_gate, ple_w_proj, ln_g, ln_b):
    depth = w_in.shape[0]
    alpha = float((2 * depth) ** 0.25)
    nb, seq, d = x_prompt.shape
    db, dt, _ = x_sample.shape
    a_width = A_HEADS * HEAD_DIM
    c_width = C_HEADS * HEAD_DIM
    in_width = w_in.shape[-1]
    col0 = 3 * a_width // c_width
    past_rows = min(A_BACK * CHUNK, seq)
    assert dt >= POOL_HIST and seq >= POOL_HIST and PAST_LEN >= POOL_HIST

    bf = lambda w: w.astype(BF16)
    w1gu, w1dn, win, wout = bf(ffn1_w_gu), bf(ffn1_w_down), bf(w_in), bf(w_out)
    w2gu, w2dn, wg, wp = bf(ffn2_w_gu), bf(ffn2_w_down), bf(ple_w_gate), bf(ple_w_proj)
    poolw = bf(_block_diag(pool_w))
    pscale = pool_scale.astype(F32)[:, None, :]
    normw = jnp.tile(hgrn_norm_w.astype(F32), (1, C_HEADS))[:, None, :]
    lbraw = hgrn_lower_bounds.astype(F32)
    pp = p_prompt.reshape(depth, nb * seq, -1)
    ps = p_sample.reshape(depth, db * dt, -1)

    zero_hist = jnp.zeros((1, nb, POOL_PAD, c_width), F32)
    zero_state = jnp.zeros((1, nb, c_width, c_width), F32)
    hist_s = jnp.pad(state_pool.astype(F32), ((0, 0), (0, 0), (POOL_PAD - POOL_HIST, 0), (0, 0)))
    state_s = _block_diag(jnp.swapaxes(state_hgrn.astype(F32), -1, -2))

    lc = cache_attn_k.shape[2]
    cache_k = cache_attn_k.reshape(depth, db, lc, a_width)
    cache_v = cache_attn_v.reshape(depth, db, lc, a_width)

    xp = x_prompt.reshape(nb * seq, d)
    xs = x_sample.reshape(db * dt, d)
    outs = [[] for _ in range(8)]

    for i in range(depth):
        rel_table = attn_rel_bias[i].astype(F32)

        xp, zp = _pre_call(xp, w1gu, w1dn, ln_g, ln_b, win, i, alpha)
        oa = _attn_prompt_call(zp, rel_table, nb, seq, a_width)
        obc, st = _recur_call(zp, zero_hist, zero_state, lbraw, normw, poolw, pscale, i, 0, nb, seq, CHUNK, 0, col0)
        xp = _post_call(xp, oa, obc, pp, wout, w2gu, w2dn, wg, wp, ln_g, ln_b, i, alpha)
        zp3 = zp.reshape(nb, seq, in_width)
        outs[0].append(zp3[:, seq - past_rows:, a_width:2 * a_width].reshape(nb, past_rows, A_HEADS, HEAD_DIM))
        outs[1].append(zp3[:, seq - past_rows:, 2 * a_width:3 * a_width].reshape(nb, past_rows, A_HEADS, HEAD_DIM))
        outs[2].append(zp3[:, seq - POOL_HIST:, 3 * a_width:3 * a_width + c_width])
        outs[3].append(_state_from_kernel(st, C_HEADS))

        xs, zs = _pre_call(xs, w1gu, w1dn, ln_g, ln_b, win, i, alpha)
        oa = _attn_sample_call(zs, cache_k, cache_v, rel_table, i, db, dt, a_width)
        obc, st = _recur_call(zs, hist_s, state_s, lbraw, normw, poolw, pscale, i, i, db, dt, dt, PAST_LEN, col0)
        xs = _post_call(xs, oa, obc, ps, wout, w2gu, w2dn, wg, wp, ln_g, ln_b, i, alpha)
        zs3 = zs.reshape(db, dt, in_width)
        outs[4].append(zs3[:, :, a_width:2 * a_width].reshape(db, dt, A_HEADS, HEAD_DIM))
        outs[5].append(zs3[:, :, 2 * a_width:3 * a_width].reshape(db, dt, A_HEADS, HEAD_DIM))
        outs[6].append(zs3[:, dt - POOL_HIST:, 3 * a_width:3 * a_width + c_width])
        outs[7].append(_state_from_kernel(st, C_HEADS))

    stacked = [jnp.stack(o, axis=0) for o in outs]
    return (xp.reshape(nb, seq, d), xs.reshape(db, dt, d), *stacked)
```

```python
import functools

import jax
import jax.numpy as jnp
from jax import lax
from jax.experimental import pallas as pl
from jax.experimental.pallas import tpu as pltpu

F32 = jnp.float32
BF16 = jnp.bfloat16

PAST_LEN = 2048
CHUNK = 64
A_BACK = 8
HEAD_DIM = 64
A_HEADS = 8
C_HEADS = 4
REL_MAX = 128
POOL_WINDOWS = (2, 4, 8, 16)
POOL_HIST = max(POOL_WINDOWS) - 1
POOL_PAD = POOL_HIST + 1
SUB_BLOCK = 32
DECAY_SAFE_LOG = -60.0
SOFTMAX_ROWS = 32
TOKEN_SUB_ROWS = 256
LN_EPS = 1e-5
RMS_EPS = 1e-6
NEG_INF = -1e30
LOG2E = 1.4426950408889634

V7X_LANES = 128
V7X_MXU_WIDTH = 256
V7X_VMEM_LIMIT = 56 * 1024 * 1024


def _dot(a, b):
    return jnp.dot(a, b, preferred_element_type=F32)


def _dot_nt(a, b):
    return lax.dot_general(a, b, (((1,), (1,)), ((), ())), preferred_element_type=F32)


def _dot_tn(a, b):
    return lax.dot_general(a, b, (((0,), (0,)), ((), ())), preferred_element_type=F32)


def _layer_norm(y, g, b):
    mu = jnp.mean(y, axis=-1, keepdims=True)
    d = y - mu
    var = jnp.mean(d * d, axis=-1, keepdims=True)
    return d * lax.rsqrt(var + LN_EPS) * g + b


def _sigmoid(x):
    return 1.0 / (1.0 + jnp.exp(-x))


def _log2(n):
    assert n > 0 and n & (n - 1) == 0, n
    return n.bit_length() - 1


def _div(x, n):
    return x >> _log2(n)


def _swiglu_stages(xb, wgu_ref, wdn_ref, n_split):
    hidden = wdn_ref.shape[0]
    groups = _hidden_groups(hidden, n_split)
    gate_up = []
    for lo, hi in groups:
        gate_up.append((_dot(xb, wgu_ref[:, lo:hi]), _dot(xb, wgu_ref[:, hidden + lo:hidden + hi])))
        yield
    acc = None
    for (lo, hi), (gate, up) in zip(groups, gate_up):
        act = (gate * _sigmoid(gate) * up).astype(BF16)
        part = _dot(act, wdn_ref[lo:hi, :])
        acc = part if acc is None else acc + part
        yield
    return acc


def _hidden_groups(hidden, n_split):
    if hidden % V7X_MXU_WIDTH:
        return [(0, hidden)]
    tiles = hidden // V7X_MXU_WIDTH
    bounds = [V7X_MXU_WIDTH * ((tiles * g + n_split - 1) // n_split) for g in range(n_split + 1)]
    return [(lo, hi) for lo, hi in zip(bounds[:-1], bounds[1:]) if hi > lo]


def _run_interleaved(chains):
    live = list(chains)
    while live:
        for ch in list(live):
            try:
                next(ch)
            except StopIteration:
                live.remove(ch)


def _sub_tiles(n_rows):
    sub = min(TOKEN_SUB_ROWS, n_rows)
    return [slice(r, r + sub) for r in range(0, n_rows, sub)]


def _pre_kernel(alpha, n_split, x_ref, wgu_ref, wdn_ref, lng_ref, lnb_ref, win_ref, xo_ref, z_ref):
    def chain(rows):
        x = x_ref[rows, :]
        ffn = yield from _swiglu_stages(x.astype(BF16), wgu_ref, wdn_ref, n_split)
        xn = _layer_norm(alpha * x + 0.5 * ffn, lng_ref[0:1, :], lnb_ref[0:1, :])
        xo_ref[rows, :] = xn
        z_ref[rows, :] = _dot(xn.astype(BF16), win_ref[...])
        yield

    _run_interleaved([chain(rows) for rows in _sub_tiles(x_ref.shape[0])])


def _post_kernel(alpha, n_split, x_ref, oa_ref, obc_ref, p_ref, wout_ref, wgu_ref, wdn_ref, wg_ref, wp_ref,
                 lng_ref, lnb_ref, xo_ref):
    wa = oa_ref.shape[1]

    def chain(rows):
        mixed = _dot(oa_ref[rows, :], wout_ref[0:wa, :]) + _dot(obc_ref[rows, :], wout_ref[wa:, :])
        proj = _dot(p_ref[rows, :].astype(BF16), wp_ref[...])
        yield
        x = _layer_norm(alpha * x_ref[rows, :] + mixed, lng_ref[1:2, :], lnb_ref[1:2, :])
        ffn = yield from _swiglu_stages(x.astype(BF16), wgu_ref, wdn_ref, n_split)
        x = _layer_norm(alpha * x + 0.5 * ffn, lng_ref[2:3, :], lnb_ref[2:3, :])
        emb = _sigmoid(_dot(x.astype(BF16), wg_ref[...])) * proj
        yield
        xo_ref[rows, :] = _layer_norm(alpha * x + emb, lng_ref[3:4, :], lnb_ref[3:4, :])

    _run_interleaved([chain(rows) for rows in _sub_tiles(x_ref.shape[0])])


def _layer_spec(stacked, layer):
    rest = stacked.shape[1:]
    return pl.BlockSpec((None,) + rest, lambda *_: (layer,) + (0,) * len(rest), pipeline_mode=pl.Buffered(1))


def _token_tile(n):
    for tm in (2 * TOKEN_SUB_ROWS, TOKEN_SUB_ROWS, 128, 64, 32, 16, 8):
        if n % tm == 0:
            return tm
    raise ValueError(f"unsupported token count {n}")


def _pre_call(x, wgu, wdn, lng, lnb, win, layer, alpha):
    n, d = x.shape
    zw = win.shape[-1]
    tm = _token_tile(n)
    row = lambda i: (i, 0)
    consts = (wgu, wdn, lng, lnb, win)
    return pl.pallas_call(
        functools.partial(_pre_kernel, alpha, 2),
        grid=(n // tm,),
        in_specs=[pl.BlockSpec((tm, d), row)] + [_layer_spec(c, layer) for c in consts],
        out_specs=[pl.BlockSpec((tm, d), row), pl.BlockSpec((tm, zw), row)],
        out_shape=[jax.ShapeDtypeStruct((n, d), F32), jax.ShapeDtypeStruct((n, zw), F32)],
        compiler_params=pltpu.CompilerParams(dimension_semantics=("parallel",),
                                             vmem_limit_bytes=V7X_VMEM_LIMIT),
        name="pre_mixer_tokens",
    )(x, *consts)


def _post_call(x, oa, obc, p, wout, wgu, wdn, wg, wp, lng, lnb, layer, alpha):
    n, d = x.shape
    tm = _token_tile(n)
    row = lambda i: (i, 0)
    consts = (wout, wgu, wdn, wg, wp, lng, lnb)
    return pl.pallas_call(
        functools.partial(_post_kernel, alpha, 2),
        grid=(n // tm,),
        in_specs=[pl.BlockSpec((tm, d), row), pl.BlockSpec((tm, oa.shape[1]), row),
                  pl.BlockSpec((tm, obc.shape[1]), row),
                  pl.BlockSpec((None, tm, p.shape[-1]), lambda i: (layer, i, 0))]
        + [_layer_spec(c, layer) for c in consts],
        out_specs=pl.BlockSpec((tm, d), row),
        out_shape=jax.ShapeDtypeStruct((n, d), F32),
        compiler_params=pltpu.CompilerParams(dimension_semantics=("parallel",),
                                             vmem_limit_bytes=V7X_VMEM_LIMIT),
        name="post_mixer_tokens",
    )(x, oa, obc, p, *consts)


def _qk_scores(q_ref, kcat, s_out, r0, n_q, k0, n_k):
    scale = HEAD_DIM ** -0.5 * LOG2E
    low = lax.broadcasted_iota(jnp.int32, (1, V7X_LANES), 1) < HEAD_DIM
    for hp in range(q_ref.shape[1] // V7X_LANES):
        ls = slice(hp * V7X_LANES, (hp + 1) * V7X_LANES)
        q = q_ref[r0:r0 + n_q, ls] * scale
        lhs = jnp.concatenate([jnp.where(low, q, 0.0), jnp.where(low, 0.0, q)], axis=0).astype(BF16)
        s_out[2 * hp * n_q:2 * (hp + 1) * n_q, :] = _dot_nt(lhs, kcat[k0:k0 + n_k, ls])


def _softmax_weights(s_in, bias_ref, key_valid, p_out, inv_out):
    n_rows = s_in.shape[0]
    rb = SOFTMAX_ROWS
    for r in range(0, n_rows, rb):
        s = s_in[r:r + rb, :] + bias_ref[r:r + rb, :]
        if key_valid is not None:
            s = jnp.where(key_valid, s, NEG_INF)
        e = jnp.exp2(s - jnp.max(s, axis=-1, keepdims=True))
        p_out[r:r + rb, :] = e.astype(BF16)
        inv = 1.0 / jnp.sum(e, axis=-1, keepdims=True)
        inv_out[r:r + rb, :] = jnp.broadcast_to(inv, (rb, inv_out.shape[1]))


def _weighted_values(p_in, inv_in, vcat, o_ref, r0, n_q, k0, n_k):
    low = lax.broadcasted_iota(jnp.int32, (1, V7X_LANES), 1) < HEAD_DIM
    for hp in range(o_ref.shape[1] // V7X_LANES):
        ls = slice(hp * V7X_LANES, (hp + 1) * V7X_LANES)
        rows = slice(2 * hp * n_q, 2 * (hp + 1) * n_q)
        o2 = _dot(p_in[rows, :], vcat[k0:k0 + n_k, ls]) * inv_in[rows, :]
        o_ref[r0:r0 + n_q, ls] = jnp.where(low, o2[:n_q], o2[n_q:]).astype(o_ref.dtype)


def _build_rel_bias(table_ref, bias_ref, n_q, n_k, offset):
    n_heads = table_ref.shape[0]
    clip = lambda d: min(max(d, -(CHUNK - 1)), REL_MAX) + (CHUNK - 1)
    heads_per_pass = 4
    for j0 in range(0, n_k, V7X_LANES):
        jw = min(V7X_LANES, n_k - j0)
        r_lo, r_hi = clip(offset - (j0 + jw - 1)), clip(offset + n_q - 1 - j0)
        d = (offset - j0 + lax.broadcasted_iota(jnp.int32, (n_q, jw), 0)
             - lax.broadcasted_iota(jnp.int32, (n_q, jw), 1))
        idx = jnp.clip(d, -(CHUNK - 1), REL_MAX) + (CHUNK - 1)
        for h0 in range(0, n_heads, heads_per_pass):
            heads = range(h0, min(h0 + heads_per_pass, n_heads))
            if r_lo == r_hi:
                vals = [jnp.full((n_q, jw), table_ref[h, r_lo], F32) for h in heads]
            else:
                def pick(r, acc, heads=heads, idx=idx):
                    hit = idx == r
                    return tuple(jnp.where(hit, table_ref[h, r], a) for h, a in zip(heads, acc))
                vals = lax.fori_loop(r_lo, r_hi + 1, pick, tuple(jnp.zeros((n_q, jw), F32) for _ in heads))
            for h, v in zip(heads, vals):
                bias_ref[h * n_q:(h + 1) * n_q, j0:j0 + jw] = v * LOG2E


def _attn_prompt_kernel(qb_rows, table_ref, q_ref, kp_ref, kc_ref, vp_ref, vc_ref, o_ref,
                        kcat, vcat, bias_ref, s_buf, p_buf, inv_buf):
    i = pl.program_id(1)
    band = (A_BACK + 1) * CHUNK
    past = A_BACK * CHUNK
    n_chunks = qb_rows // CHUNK

    @pl.when((pl.program_id(0) == 0) & (i == 0))
    def _():
        _build_rel_bias(table_ref, bias_ref, CHUNK, band, past)

    def cast_rows(j, carry):
        r = pl.multiple_of(j * CHUNK, CHUNK)
        kcat[pl.ds(r, CHUNK), :] = kp_ref[pl.ds(r, CHUNK), :].astype(BF16)
        kcat[pl.ds(qb_rows + r, CHUNK), :] = kc_ref[pl.ds(r, CHUNK), :].astype(BF16)
        vcat[pl.ds(r, CHUNK), :] = vp_ref[pl.ds(r, CHUNK), :].astype(BF16)
        vcat[pl.ds(qb_rows + r, CHUNK), :] = vc_ref[pl.ds(r, CHUNK), :].astype(BF16)
        return carry

    lax.fori_loop(0, n_chunks, cast_rows, 0)
    key_idx = lax.broadcasted_iota(jnp.int32, (1, band), 1)

    _qk_scores(q_ref, kcat, s_buf.at[0], 0, CHUNK, 0, band)
    for c in range(n_chunks + 1):
        if c + 1 < n_chunks:
            _qk_scores(q_ref, kcat, s_buf.at[(c + 1) % 2], (c + 1) * CHUNK, CHUNK, (c + 1) * CHUNK, band)
        if c < n_chunks:
            key_valid = key_idx >= past - c * CHUNK - i * qb_rows
            _softmax_weights(s_buf.at[c % 2], bias_ref, key_valid, p_buf.at[c % 2], inv_buf.at[c % 2])
        if c >= 1:
            _weighted_values(p_buf.at[(c - 1) % 2], inv_buf.at[(c - 1) % 2], vcat, o_ref,
                             (c - 1) * CHUNK, CHUNK, (c - 1) * CHUNK, band)


def _attn_prompt_call(z, table, batch, seq, a_width):
    qb_rows = A_BACK * CHUNK
    assert seq % qb_rows == 0 and a_width % V7X_LANES == 0
    nq = seq // qb_rows
    band = (A_BACK + 1) * CHUNK
    rows = A_HEADS * CHUNK
    cur = lambda col: (lambda b, i: (b * nq + i, col))
    prev = lambda col: (lambda b, i: (b * nq + jnp.maximum(i - 1, 0), col))
    blk = (qb_rows, a_width)
    return pl.pallas_call(
        functools.partial(_attn_prompt_kernel, qb_rows),
        grid=(batch, nq),
        in_specs=[pl.BlockSpec(memory_space=pltpu.SMEM),
                  pl.BlockSpec(blk, cur(0)), pl.BlockSpec(blk, prev(1)), pl.BlockSpec(blk, cur(1)),
                  pl.BlockSpec(blk, prev(2)), pl.BlockSpec(blk, cur(2))],
        out_specs=pl.BlockSpec(blk, lambda b, i: (b * nq + i, 0)),
        out_shape=jax.ShapeDtypeStruct((batch * seq, a_width), BF16),
        scratch_shapes=[pltpu.VMEM((2 * qb_rows, a_width), BF16), pltpu.VMEM((2 * qb_rows, a_width), BF16),
                        pltpu.VMEM((rows, band), F32), pltpu.VMEM((2, rows, band), F32),
                        pltpu.VMEM((2, rows, band), BF16), pltpu.VMEM((2, rows, V7X_LANES), F32)],
        compiler_params=pltpu.CompilerParams(dimension_semantics=("arbitrary", "arbitrary"),
                                             vmem_limit_bytes=V7X_VMEM_LIMIT),
        name="band_attention_prompt",
    )(table, z, z, z, z, z)


def _attn_sample_kernel(table_ref, q_ref, k_ref, v_ref, ck_ref, cv_ref, o_ref, kcat, vcat, bias_ref,
                        s_buf, p_buf, inv_buf):
    lc = ck_ref.shape[0]
    t = q_ref.shape[0]

    @pl.when(pl.program_id(0) == 0)
    def _():
        _build_rel_bias(table_ref, bias_ref, t, lc + t, lc)

    kcat[0:lc, :] = ck_ref[...].astype(BF16)
    kcat[lc:, :] = k_ref[...].astype(BF16)
    vcat[0:lc, :] = cv_ref[...].astype(BF16)
    vcat[lc:, :] = v_ref[...].astype(BF16)
    _qk_scores(q_ref, kcat, s_buf, 0, t, 0, lc + t)
    _softmax_weights(s_buf, bias_ref, None, p_buf, inv_buf)
    _weighted_values(p_buf, inv_buf, vcat, o_ref, 0, t, 0, lc + t)


def _attn_sample_call(z, cache_k, cache_v, table, layer, batch, t, a_width):
    lc = cache_k.shape[2]
    rows = A_HEADS * t
    blk = (t, a_width)
    cblk = (None, None, lc, a_width)
    cidx = lambda b: (layer, b, 0, 0)
    return pl.pallas_call(
        _attn_sample_kernel,
        grid=(batch,),
        in_specs=[pl.BlockSpec(memory_space=pltpu.SMEM),
                  pl.BlockSpec(blk, lambda b: (b, 0)), pl.BlockSpec(blk, lambda b: (b, 1)),
                  pl.BlockSpec(blk, lambda b: (b, 2)),
                  pl.BlockSpec(cblk, cidx), pl.BlockSpec(cblk, cidx)],
        out_specs=pl.BlockSpec(blk, lambda b: (b, 0)),
        out_shape=jax.ShapeDtypeStruct((batch * t, a_width), BF16),
        scratch_shapes=[pltpu.VMEM((lc + t, a_width), BF16), pltpu.VMEM((lc + t, a_width), BF16),
                        pltpu.VMEM((rows, lc + t), F32), pltpu.VMEM((rows, lc + t), F32),
                        pltpu.VMEM((rows, lc + t), BF16), pltpu.VMEM((rows, V7X_LANES), F32)],
        compiler_params=pltpu.CompilerParams(dimension_semantics=("arbitrary",),
                                             vmem_limit_bytes=V7X_VMEM_LIMIT),
        name="band_attention_sample",
    )(table, z, z, z, cache_k, cache_v)


def _split2(x):
    a = x.astype(BF16)
    return a, (x - a.astype(F32)).astype(BF16)


def _recur_kernel(layer, blk, pos0, n_heads, u_ref, q_ref, f_ref, i_ref, g_ref, hist0_ref, st0_ref, lbraw_ref,
                  normw_ref, poolw_ref, pscale_ref, o_ref, st_ref, state, hist, dmask, inter, rows_b, rows_k):
    t_idx = pl.program_id(1)
    tm, width = u_ref.shape
    hd = width // n_heads
    sb = min(SUB_BLOCK, blk)
    assert blk // sb in (1, 2) and tm % blk == 0

    @pl.when(t_idx == 0)
    def _():
        state[...] = st0_ref[...]
        hist[...] = hist0_ref[...]
        ri = lax.broadcasted_iota(jnp.int32, (tm, tm), 0)
        ci = lax.broadcasted_iota(jnp.int32, (tm, tm), 1)
        same_sb = _div(ri, sb) == _div(ci, sb)
        dmask[0] = jnp.where(same_sb & (ci <= ri), 1.0, 0.0).astype(BF16)
        dmask[1] = jnp.where(same_sb, 1.0, 0.0).astype(BF16)

    ext = jnp.concatenate([hist[...], u_ref[...]], axis=0)
    hist[...] = ext[tm:, :]
    s2 = ext + pltpu.roll(ext, 1, 0)
    s4 = s2 + pltpu.roll(s2, 2, 0)
    s8 = s4 + pltpu.roll(s4, 4, 0)
    s16 = s8 + pltpu.roll(s8, 8, 0)
    lane = lax.broadcasted_iota(jnp.int32, (1, width), 1)
    grp = _div(lane, width // len(POOL_WINDOWS))
    wsum = jnp.where(grp == 0, s2, jnp.where(grp == 1, s4, jnp.where(grp == 2, s8, s16)))
    wlen = jnp.where(grp == 0, 2.0, jnp.where(grp == 1, 4.0, jnp.where(grp == 2, 8.0, 16.0)))
    row = lax.broadcasted_iota(jnp.int32, (POOL_PAD + tm, 1), 0)
    pos = (row + (pos0 - POOL_PAD) + t_idx * tm).astype(F32)
    cnt = jnp.maximum(jnp.minimum(pos + 1.0, wlen), 1.0)
    dev = (wsum / cnt - ext)[POOL_PAD:, :]
    o_pool = _dot(dev.astype(BF16), poolw_ref[...]) * pscale_ref[...]
    o_ref[:, 0:width] = o_pool.astype(o_ref.dtype)

    raw = lbraw_ref[...]
    sm = jnp.exp(raw - jnp.max(raw, axis=0, keepdims=True))
    sm = sm / jnp.sum(sm, axis=0, keepdims=True)
    lb = jnp.zeros((1, width), F32)
    for j in range(1, layer + 1):
        lb = lb + sm[j:j + 1, :]
    forget = lb + (1.0 - lb) * _sigmoid(f_ref[...])
    log_f = jnp.log(forget)
    k_in = 1.0 - forget
    qx = q_ref[...]
    qf = qx * _sigmoid(qx)
    vb = i_ref[...].astype(BF16)

    lf2 = jnp.concatenate(_split2(log_f), axis=1)

    def decay_sum(mask01):
        r = _dot(mask01, lf2)
        return r[:, 0:width] + r[:, width:]

    b_rel = decay_sum(dmask[0])
    sb_tot = decay_sum(dmask[1])
    if blk > sb:
        second = (lax.broadcasted_iota(jnp.int32, (tm, 1), 0) & (blk - 1)) >= sb
        prev_tot = pltpu.roll(sb_tot, sb, 0)
        next_tot = pltpu.roll(sb_tot, tm - sb, 0)
        b = b_rel + jnp.where(second, prev_tot, 0.0)
        blk_tot = sb_tot + jnp.where(second, prev_tot, next_tot)
    else:
        b, blk_tot = b_rel, sb_tot

    q_rel = qf * jnp.exp(b_rel)
    k_rel = (k_in * jnp.exp(-b_rel)).astype(BF16)
    k_end = (k_in * jnp.exp(sb_tot - b_rel)).astype(BF16)
    q_abs = (qf * jnp.exp(b)).astype(BF16)
    k_tail = (k_in * jnp.exp(blk_tot - b)).astype(BF16)

    hb = n_heads * blk
    n_chunks = tm // blk
    stack_head = _div(lax.broadcasted_iota(jnp.int32, (hb, 1), 0), blk) == _div(lane, hd)
    tq = lax.broadcasted_iota(jnp.int32, (hb, blk), 0) & (blk - 1)
    ts = lax.broadcasted_iota(jnp.int32, (hb, blk), 1)
    m_intra = (_div(tq, sb) == _div(ts, sb)) & (ts <= tq)
    m_cross = _div(ts, sb) < _div(tq, sb)
    bd = (_div(lax.broadcasted_iota(jnp.int32, (width, width), 0), hd)
          == _div(lax.broadcasted_iota(jnp.int32, (width, width), 1), hd))
    ones_bd = jnp.where(bd, 1.0, 0.0).astype(BF16)
    gx = g_ref[...]
    out_gate = normw_ref[...] * (gx * _sigmoid(gx))

    def write_output(o):
        sq_hi, sq_lo = _split2(o * o)
        ms = (_dot(sq_hi, ones_bd) + _dot(sq_lo, ones_bd)) * (1.0 / hd)
        o_ref[:, width:] = (o * lax.rsqrt(ms + RMS_EPS) * out_gate).astype(o_ref.dtype)

    outs = []
    for c in range(n_chunks):
        rs = slice(c * blk, (c + 1) * blk)
        lhs = jnp.where(stack_head, jnp.concatenate([q_rel[rs]] * n_heads, axis=0), 0.0).astype(BF16)
        a = jnp.where(m_intra, _dot_nt(lhs, k_rel[rs]), 0.0)
        if blk > sb:
            a = a + jnp.where(m_cross, _dot_nt(lhs, k_end[rs]), 0.0)
        stacked = jnp.where(stack_head, _dot(a.astype(BF16), vb[rs]), 0.0)
        o_c = stacked[0:blk]
        for h in range(1, n_heads):
            o_c = o_c + stacked[h * blk:(h + 1) * blk]
        st = state[...]
        from_state = _dot_nt(q_abs[rs], st.astype(BF16))
        inter[rs, :] = from_state
        outs.append(o_c + from_state)
        decay_end = jnp.exp(blk_tot[c * blk:c * blk + 1, :])
        state[...] = st * decay_end + jnp.where(bd, _dot_tn(vb[rs], k_tail[rs]), 0.0)
    write_output(outs[0] if len(outs) == 1 else jnp.concatenate(outs, axis=0))

    @pl.when(jnp.min(sb_tot) <= DECAY_SAFE_LOG)
    def _():
        rows_b[...] = b
        rows_k[...] = k_in
        t_row = lax.broadcasted_iota(jnp.int32, (blk, 1), 0)

        def add_key_row(s, acc):
            parts = []
            for c in range(n_chunks):
                rs = slice(c * blk, (c + 1) * blk)
                r = c * blk + s
                rel = jnp.minimum(b[rs] - rows_b[pl.ds(r, 1), :], 0.0)
                term = jnp.where(t_row >= s, qf[rs] * jnp.exp(rel) * rows_k[pl.ds(r, 1), :], 0.0)
                parts.append(_dot(term.astype(BF16), ones_bd) * i_ref[pl.ds(r, 1), :])
            return acc + (parts[0] if n_chunks == 1 else jnp.concatenate(parts, axis=0))

        write_output(lax.fori_loop(0, blk, add_key_row, inter[...]))

    @pl.when(t_idx == pl.num_programs(1) - 1)
    def _():
        st_ref[...] = state[...]


def _recur_call(z, hist0, st0, lbraw, normw, poolw, pscale, layer, state_layer, batch, seq, blk, pos0, col0):
    width = st0.shape[-1]
    tm = min(256, seq)
    assert seq % tm == 0 and tm % blk == 0
    nt = seq // tm
    col = lambda c: (lambda b, t: (b * nt + t, c))
    per_batch = lambda b, t: (state_layer, b, 0, 0)
    consts = (normw, poolw, pscale)
    return pl.pallas_call(
        functools.partial(_recur_kernel, layer, blk, pos0, C_HEADS),
        grid=(batch, nt),
        in_specs=[pl.BlockSpec((tm, width), col(col0 + j)) for j in range(5)]
        + [pl.BlockSpec((None, None, POOL_PAD, width), per_batch),
           pl.BlockSpec((None, None, width, width), per_batch),
           pl.BlockSpec(lbraw.shape, lambda b, t: (0, 0), pipeline_mode=pl.Buffered(1))]
        + [_layer_spec(c, layer) for c in consts],
        out_specs=[pl.BlockSpec((tm, 2 * width), lambda b, t: (b * nt + t, 0)),
                   pl.BlockSpec((None, width, width), lambda b, t: (b, 0, 0))],
        out_shape=[jax.ShapeDtypeStruct((batch * seq, 2 * width), BF16),
                   jax.ShapeDtypeStruct((batch, width, width), F32)],
        scratch_shapes=[pltpu.VMEM((width, width), F32), pltpu.VMEM((POOL_PAD, width), F32),
                        pltpu.VMEM((2, tm, tm), BF16), pltpu.VMEM((tm, width), F32),
                        pltpu.VMEM((tm, width), F32), pltpu.VMEM((tm, width), F32)],
        compiler_params=pltpu.CompilerParams(dimension_semantics=("parallel", "arbitrary"),
                                             vmem_limit_bytes=V7X_VMEM_LIMIT),
        name="pool_hgrn_mixer",
    )(z, z, z, z, z, hist0, st0, lbraw, *consts)


def _block_diag(blocks):
    g = blocks.shape[-3]
    zero = jnp.zeros_like(blocks[..., 0, :, :])
    rows = [jnp.concatenate([blocks[..., h, :, :] if j == h else zero for j in range(g)], axis=-1)
            for h in range(g)]
    return jnp.concatenate(rows, axis=-2)


def _state_from_kernel(st, n_heads):
    b, w, _ = st.shape
    hd = w // n_heads
    blocks = jnp.stack([st[:, h * hd:(h + 1) * hd, h * hd:(h + 1) * hd] for h in range(n_heads)], axis=1)
    return jnp.swapaxes(blocks, -1, -2)


def kernel(x_prompt, x_sample, p_prompt, p_sample, cache_attn_k, cache_attn_v, state_pool, state_hgrn, ffn1_w_gu, ffn1_w_down, w_in, attn_rel_bias, pool_w, pool_scale, hgrn_lower_bounds, hgrn_norm_w, w_out, ffn2_w_gu, ffn2_w_down, ple_w_gate, ple_w_proj, ln_g, ln_b):
    depth = w_in.shape[0]
    alpha = float((2 * depth) ** 0.25)
    nb, seq, d = x_prompt.shape
    db, dt, _ = x_sample.shape
    a_width = A_HEADS * HEAD_DIM
    c_width = C_HEADS * HEAD_DIM
    in_width = w_in.shape[-1]
    col0 = 3 * a_width // c_width
    past_rows = min(A_BACK * CHUNK, seq)
    assert dt >= POOL_HIST and seq >= POOL_HIST and PAST_LEN >= POOL_HIST

    bf = lambda w: w.astype(BF16)
    w1gu, w1dn, win, wout = bf(ffn1_w_gu), bf(ffn1_w_down), bf(w_in), bf(w_out)
    w2gu, w2dn, wg, wp = bf(ffn2_w_gu), bf(ffn2_w_down), bf(ple_w_gate), bf(ple_w_proj)
    poolw = bf(_block_diag(pool_w))
    pscale = pool_scale.astype(F32)[:, None, :]
    normw = jnp.tile(hgrn_norm_w.astype(F32), (1, C_HEADS))[:, None, :]
    lbraw = hgrn_lower_bounds.astype(F32)
    pp = p_prompt.reshape(depth, nb * seq, -1)
    ps = p_sample.reshape(depth, db * dt, -1)

    zero_hist = jnp.zeros((1, nb, POOL_PAD, c_width), F32)
    zero_state = jnp.zeros((1, nb, c_width, c_width), F32)
    hist_s = jnp.pad(state_pool.astype(F32), ((0, 0), (0, 0), (POOL_PAD - POOL_HIST, 0), (0, 0)))
    state_s = _block_diag(jnp.swapaxes(state_hgrn.astype(F32), -1, -2))

    lc = cache_attn_k.shape[2]
    cache_k = cache_attn_k.reshape(depth, db, lc, a_width)
    cache_v = cache_attn_v.reshape(depth, db, lc, a_width)

    xp = x_prompt.reshape(nb * seq, d)
    xs = x_sample.reshape(db * dt, d)
    outs = [[] for _ in range(8)]

    for i in range(depth):
        rel_table = attn_rel_bias[i].astype(F32)

        xp, zp = _pre_call(xp, w1gu, w1dn, ln_g, ln_b, win, i, alpha)
        oa = _attn_prompt_call(zp, rel_table, nb, seq, a_width)
        obc, st = _recur_call(zp, zero_hist, zero_state, lbraw, normw, poolw, pscale, i, 0, nb, seq, CHUNK, 0, col0)
        xp = _post_call(xp, oa, obc, pp, wout, w2gu, w2dn, wg, wp, ln_g, ln_b, i, alpha)
        zp3 = zp.reshape(nb, seq, in_width)
        outs[0].append(zp3[:, seq - past_rows:, a_width:2 * a_width].reshape(nb, past_rows, A_HEADS, HEAD_DIM))
        outs[1].append(zp3[:, seq - past_rows:, 2 * a_width:3 * a_width].reshape(nb, past_rows, A_HEADS, HEAD_DIM))
        outs[2].append(zp3[:, seq - POOL_HIST:, 3 * a_width:3 * a_width + c_width])
        outs[3].append(_state_from_kernel(st, C_HEADS))

        xs, zs = _pre_call(xs, w1gu, w1dn, ln_g, ln_b, win, i, alpha)
        oa = _attn_sample_call(zs, cache_k, cache_v, rel_table, i, db, dt, a_width)
        obc, st = _recur_call(zs, hist_s, state_s, lbraw, normw, poolw, pscale, i, i, db, dt, dt, PAST_LEN, col0)
        xs = _post_call(xs, oa, obc, ps, wout, w2gu, w2dn, wg, wp, ln_g, ln_b, i, alpha)
        zs3 = zs.reshape(db, dt, in_width)
        outs[4].append(zs3[:, :, a_width:2 * a_width].reshape(db, dt, A_HEADS, HEAD_DIM))
        outs[5].append(zs3[:, :, 2 * a_width:3 * a_width].reshape(db, dt, A_HEADS, HEAD_DIM))
        outs[6].append(zs3[:, dt - POOL_HIST:, 3 * a_width:3 * a_width + c_width])
        outs[7].append(_state_from_kernel(st, C_HEADS))

    stacked = [jnp.stack(o, axis=0) for o in outs]
    return (xp.reshape(nb, seq, d), xs.reshape(db, dt, d), *stacked)
```

```python
import functools

import jax
import jax.numpy as jnp
from jax import lax
from jax.experimental import pallas as pl
from jax.experimental.pallas import tpu as pltpu

F32 = jnp.float32
BF16 = jnp.bfloat16

PAST_LEN = 2048
CHUNK = 64
A_BACK = 8
HEAD_DIM = 64
A_HEADS = 8
C_HEADS = 4
REL_MAX = 128
POOL_WINDOWS = (2, 4, 8, 16)
POOL_HIST = max(POOL_WINDOWS) - 1
POOL_PAD = POOL_HIST + 1
SUB_BLOCK = 32
DECAY_SAFE_LOG = -60.0
TOKEN_SUB_ROWS = 256
LN_EPS = 1e-5
RMS_EPS = 1e-6
NEG_INF = -1e30
LOG2E = 1.4426950408889634

V7X_LANES = 128
V7X_MXU_WIDTH = 256
V7X_VMEM_LIMIT = 56 * 1024 * 1024


def _dot(a, b):
    return jnp.dot(a, b, preferred_element_type=F32)


def _dot_nt(a, b):
    return lax.dot_general(a, b, (((1,), (1,)), ((), ())), preferred_element_type=F32)


def _dot_tn(a, b):
    return lax.dot_general(a, b, (((0,), (0,)), ((), ())), preferred_element_type=F32)


def _layer_norm(y, g, b):
    mu = jnp.mean(y, axis=-1, keepdims=True)
    d = y - mu
    var = jnp.mean(d * d, axis=-1, keepdims=True)
    return d * lax.rsqrt(var + LN_EPS) * g + b


def _sigmoid(x):
    return 1.0 / (1.0 + jnp.exp(-x))


def _log2(n):
    assert n > 0 and n & (n - 1) == 0, n
    return n.bit_length() - 1


def _div(x, n):
    return x >> _log2(n)


def _swiglu_stages(xb, wgu_ref, wdn_ref, n_split):
    hidden = wdn_ref.shape[0]
    groups = _hidden_groups(hidden, n_split)
    gate_up = []
    for lo, hi in groups:
        gate_up.append((_dot(xb, wgu_ref[:, lo:hi]), _dot(xb, wgu_ref[:, hidden + lo:hidden + hi])))
        yield
    acc = None
    for (lo, hi), (gate, up) in zip(groups, gate_up):
        act = (gate * _sigmoid(gate) * up).astype(BF16)
        part = _dot(act, wdn_ref[lo:hi, :])
        acc = part if acc is None else acc + part
        yield
    return acc


def _hidden_groups(hidden, n_split):
    if hidden % V7X_MXU_WIDTH:
        return [(0, hidden)]
    tiles = hidden // V7X_MXU_WIDTH
    bounds = [V7X_MXU_WIDTH * ((tiles * g + n_split - 1) // n_split) for g in range(n_split + 1)]
    return [(lo, hi) for lo, hi in zip(bounds[:-1], bounds[1:]) if hi > lo]


def _run_interleaved(chains):
    live = list(chains)
    while live:
        for ch in list(live):
            try:
                next(ch)
            except StopIteration:
                live.remove(ch)


def _sub_tiles(n_rows):
    sub = min(TOKEN_SUB_ROWS, n_rows)
    return [slice(r, r + sub) for r in range(0, n_rows, sub)]


def _pre_kernel(alpha, n_split, x_ref, wgu_ref, wdn_ref, lng_ref, lnb_ref, win_ref, xo_ref, z_ref):
    def chain(rows):
        x = x_ref[rows, :]
        ffn = yield from _swiglu_stages(x.astype(BF16), wgu_ref, wdn_ref, n_split)
        xn = _layer_norm(alpha * x + 0.5 * ffn, lng_ref[0:1, :], lnb_ref[0:1, :])
        xo_ref[rows, :] = xn
        z_ref[rows, :] = _dot(xn.astype(BF16), win_ref[...])
        yield

    _run_interleaved([chain(rows) for rows in _sub_tiles(x_ref.shape[0])])


def _post_kernel(alpha, n_split, x_ref, oa_ref, obc_ref, p_ref, wout_ref, wgu_ref, wdn_ref, wg_ref, wp_ref,
                 lng_ref, lnb_ref, xo_ref):
    wa = oa_ref.shape[1]

    def chain(rows):
        mixed = _dot(oa_ref[rows, :], wout_ref[0:wa, :]) + _dot(obc_ref[rows, :], wout_ref[wa:, :])
        proj = _dot(p_ref[rows, :].astype(BF16), wp_ref[...])
        yield
        x = _layer_norm(alpha * x_ref[rows, :] + mixed, lng_ref[1:2, :], lnb_ref[1:2, :])
        ffn = yield from _swiglu_stages(x.astype(BF16), wgu_ref, wdn_ref, n_split)
        x = _layer_norm(alpha * x + 0.5 * ffn, lng_ref[2:3, :], lnb_ref[2:3, :])
        emb = _sigmoid(_dot(x.astype(BF16), wg_ref[...])) * proj
        yield
        xo_ref[rows, :] = _layer_norm(alpha * x + emb, lng_ref[3:4, :], lnb_ref[3:4, :])

    _run_interleaved([chain(rows) for rows in _sub_tiles(x_ref.shape[0])])


def _layer_spec(stacked, layer):
    rest = stacked.shape[1:]
    return pl.BlockSpec((None,) + rest, lambda *_: (layer,) + (0,) * len(rest), pipeline_mode=pl.Buffered(1))


def _token_tile(n):
    for tm in (2 * TOKEN_SUB_ROWS, TOKEN_SUB_ROWS, 128, 64, 32, 16, 8):
        if n % tm == 0:
            return tm
    raise ValueError(f"unsupported token count {n}")


def _pre_call(x, wgu, wdn, lng, lnb, win, layer, alpha):
    n, d = x.shape
    zw = win.shape[-1]
    tm = _token_tile(n)
    row = lambda i: (i, 0)
    consts = (wgu, wdn, lng, lnb, win)
    return pl.pallas_call(
        functools.partial(_pre_kernel, alpha, 2),
        grid=(n // tm,),
        in_specs=[pl.BlockSpec((tm, d), row)] + [_layer_spec(c, layer) for c in consts],
        out_specs=[pl.BlockSpec((tm, d), row), pl.BlockSpec((tm, zw), row)],
        out_shape=[jax.ShapeDtypeStruct((n, d), F32), jax.ShapeDtypeStruct((n, zw), F32)],
        compiler_params=pltpu.CompilerParams(dimension_semantics=("parallel",),
                                             vmem_limit_bytes=V7X_VMEM_LIMIT),
        name="pre_mixer_tokens",
    )(x, *consts)


def _post_call(x, oa, obc, p, wout, wgu, wdn, wg, wp, lng, lnb, layer, alpha):
    n, d = x.shape
    tm = _token_tile(n)
    row = lambda i: (i, 0)
    consts = (wout, wgu, wdn, wg, wp, lng, lnb)
    return pl.pallas_call(
        functools.partial(_post_kernel, alpha, 2),
        grid=(n // tm,),
        in_specs=[pl.BlockSpec((tm, d), row), pl.BlockSpec((tm, oa.shape[1]), row),
                  pl.BlockSpec((tm, obc.shape[1]), row),
                  pl.BlockSpec((None, tm, p.shape[-1]), lambda i: (layer, i, 0))]
        + [_layer_spec(c, layer) for c in consts],
        out_specs=pl.BlockSpec((tm, d), row),
        out_shape=jax.ShapeDtypeStruct((n, d), F32),
        compiler_params=pltpu.CompilerParams(dimension_semantics=("parallel",),
                                             vmem_limit_bytes=V7X_VMEM_LIMIT),
        name="post_mixer_tokens",
    )(x, oa, obc, p, *consts)


def _qk_scores(q_ref, kcat, s_out, r0, n_q, k0, n_k):
    scale = HEAD_DIM ** -0.5 * LOG2E
    low = lax.broadcasted_iota(jnp.int32, (1, V7X_LANES), 1) < HEAD_DIM
    for hp in range(q_ref.shape[1] // V7X_LANES):
        ls = slice(hp * V7X_LANES, (hp + 1) * V7X_LANES)
        q = q_ref[r0:r0 + n_q, ls] * scale
        lhs = jnp.concatenate([jnp.where(low, q, 0.0), jnp.where(low, 0.0, q)], axis=0).astype(BF16)
        s_out[2 * hp * n_q:2 * (hp + 1) * n_q, :] = _dot_nt(lhs, kcat[k0:k0 + n_k, ls])


def _softmax_weights(s_in, bias_ref, key_valid, p_out):
    s = s_in[...] + bias_ref[...]
    if key_valid is not None:
        s = jnp.where(key_valid, s, NEG_INF)
    p_out[...] = jnp.exp2(s - jnp.max(s, axis=-1, keepdims=True)).astype(BF16)


def _weighted_values(p_in, vcat, o_ref, r0, n_q, k0, n_k):
    low = lax.broadcasted_iota(jnp.int32, (1, V7X_LANES), 1) < HEAD_DIM
    ones = jnp.ones((n_k, V7X_LANES), BF16)
    for hp in range(o_ref.shape[1] // V7X_LANES):
        ls = slice(hp * V7X_LANES, (hp + 1) * V7X_LANES)
        rows = slice(2 * hp * n_q, 2 * (hp + 1) * n_q)
        pv = _dot(p_in[rows, :], jnp.concatenate([vcat[k0:k0 + n_k, ls], ones], axis=1))
        o2 = pv[:, :V7X_LANES] * (1.0 / pv[:, V7X_LANES:])
        o_ref[r0:r0 + n_q, ls] = jnp.where(low, o2[:n_q], o2[n_q:]).astype(o_ref.dtype)


def _build_rel_bias(table_ref, bias_ref, n_q, n_k, offset):
    n_heads = table_ref.shape[0]
    clip = lambda d: min(max(d, -(CHUNK - 1)), REL_MAX) + (CHUNK - 1)
    heads_per_pass = 4
    for j0 in range(0, n_k, V7X_LANES):
        jw = min(V7X_LANES, n_k - j0)
        r_lo, r_hi = clip(offset - (j0 + jw - 1)), clip(offset + n_q - 1 - j0)
        d = (offset - j0 + lax.broadcasted_iota(jnp.int32, (n_q, jw), 0)
             - lax.broadcasted_iota(jnp.int32, (n_q, jw), 1))
        idx = jnp.clip(d, -(CHUNK - 1), REL_MAX) + (CHUNK - 1)
        for h0 in range(0, n_heads, heads_per_pass):
            heads = range(h0, min(h0 + heads_per_pass, n_heads))
            if r_lo == r_hi:
                vals = [jnp.full((n_q, jw), table_ref[h, r_lo], F32) for h in heads]
            else:
                def pick(r, acc, heads=heads, idx=idx):
                    hit = idx == r
                    return tuple(jnp.where(hit, table_ref[h, r], a) for h, a in zip(heads, acc))
                vals = lax.fori_loop(r_lo, r_hi + 1, pick, tuple(jnp.zeros((n_q, jw), F32) for _ in heads))
            for h, v in zip(heads, vals):
                bias_ref[h * n_q:(h + 1) * n_q, j0:j0 + jw] = v * LOG2E


def _attn_prompt_kernel(qb_rows, table_ref, q_ref, kp_ref, kc_ref, vp_ref, vc_ref, o_ref,
                        kcat, vcat, bias_ref, s_buf, p_buf):
    i = pl.program_id(1)
    band = (A_BACK + 1) * CHUNK
    past = A_BACK * CHUNK
    n_chunks = qb_rows // CHUNK

    @pl.when((pl.program_id(0) == 0) & (i == 0))
    def _():
        _build_rel_bias(table_ref, bias_ref, CHUNK, band, past)

    def cast_rows(j, carry):
        r = pl.multiple_of(j * CHUNK, CHUNK)
        kcat[pl.ds(r, CHUNK), :] = kp_ref[pl.ds(r, CHUNK), :].astype(BF16)
        kcat[pl.ds(qb_rows + r, CHUNK), :] = kc_ref[pl.ds(r, CHUNK), :].astype(BF16)
        vcat[pl.ds(r, CHUNK), :] = vp_ref[pl.ds(r, CHUNK), :].astype(BF16)
        vcat[pl.ds(qb_rows + r, CHUNK), :] = vc_ref[pl.ds(r, CHUNK), :].astype(BF16)
        return carry

    lax.fori_loop(0, n_chunks, cast_rows, 0)
    key_idx = lax.broadcasted_iota(jnp.int32, (1, band), 1)

    _qk_scores(q_ref, kcat, s_buf.at[0], 0, CHUNK, 0, band)
    for c in range(n_chunks + 1):
        if c + 1 < n_chunks:
            _qk_scores(q_ref, kcat, s_buf.at[(c + 1) % 2], (c + 1) * CHUNK, CHUNK, (c + 1) * CHUNK, band)
        if c < n_chunks:
            key_valid = key_idx >= past - c * CHUNK - i * qb_rows
            _softmax_weights(s_buf.at[c % 2], bias_ref, key_valid, p_buf.at[c % 2])
        if c >= 1:
            _weighted_values(p_buf.at[(c - 1) % 2], vcat, o_ref,
                             (c - 1) * CHUNK, CHUNK, (c - 1) * CHUNK, band)


def _attn_prompt_call(z, table, batch, seq, a_width):
    qb_rows = A_BACK * CHUNK
    assert seq % qb_rows == 0 and a_width % V7X_LANES == 0
    nq = seq // qb_rows
    band = (A_BACK + 1) * CHUNK
    rows = A_HEADS * CHUNK
    cur = lambda col: (lambda b, i: (b * nq + i, col))
    prev = lambda col: (lambda b, i: (b * nq + jnp.maximum(i - 1, 0), col))
    blk = (qb_rows, a_width)
    return pl.pallas_call(
        functools.partial(_attn_prompt_kernel, qb_rows),
        grid=(batch, nq),
        in_specs=[pl.BlockSpec(memory_space=pltpu.SMEM),
                  pl.BlockSpec(blk, cur(0)), pl.BlockSpec(blk, prev(1)), pl.BlockSpec(blk, cur(1)),
                  pl.BlockSpec(blk, prev(2)), pl.BlockSpec(blk, cur(2))],
        out_specs=pl.BlockSpec(blk, lambda b, i: (b * nq + i, 0)),
        out_shape=jax.ShapeDtypeStruct((batch * seq, a_width), BF16),
        scratch_shapes=[pltpu.VMEM((2 * qb_rows, a_width), BF16), pltpu.VMEM((2 * qb_rows, a_width), BF16),
                        pltpu.VMEM((rows, band), F32), pltpu.VMEM((2, rows, band), F32),
                        pltpu.VMEM((2, rows, band), BF16)],
        compiler_params=pltpu.CompilerParams(dimension_semantics=("arbitrary", "arbitrary"),
                                             vmem_limit_bytes=V7X_VMEM_LIMIT),
        name="band_attention_prompt",
    )(table, z, z, z, z, z)


def _attn_sample_kernel(table_ref, q_ref, k_ref, v_ref, ck_ref, cv_ref, o_ref, kcat, vcat, bias_ref,
                        s_buf, p_buf):
    lc = ck_ref.shape[0]
    t = q_ref.shape[0]

    @pl.when(pl.program_id(0) == 0)
    def _():
        _build_rel_bias(table_ref, bias_ref, t, lc + t, lc)

    kcat[0:lc, :] = ck_ref[...].astype(BF16)
    kcat[lc:, :] = k_ref[...].astype(BF16)
    vcat[0:lc, :] = cv_ref[...].astype(BF16)
    vcat[lc:, :] = v_ref[...].astype(BF16)
    _qk_scores(q_ref, kcat, s_buf, 0, t, 0, lc + t)
    _softmax_weights(s_buf, bias_ref, None, p_buf)
    _weighted_values(p_buf, vcat, o_ref, 0, t, 0, lc + t)


def _attn_sample_call(z, cache_k, cache_v, table, layer, batch, t, a_width):
    lc = cache_k.shape[2]
    rows = A_HEADS * t
    blk = (t, a_width)
    cblk = (None, None, lc, a_width)
    cidx = lambda b: (layer, b, 0, 0)
    return pl.pallas_call(
        _attn_sample_kernel,
        grid=(batch,),
        in_specs=[pl.BlockSpec(memory_space=pltpu.SMEM),
                  pl.BlockSpec(blk, lambda b: (b, 0)), pl.BlockSpec(blk, lambda b: (b, 1)),
                  pl.BlockSpec(blk, lambda b: (b, 2)),
                  pl.BlockSpec(cblk, cidx), pl.BlockSpec(cblk, cidx)],
        out_specs=pl.BlockSpec(blk, lambda b: (b, 0)),
        out_shape=jax.ShapeDtypeStruct((batch * t, a_width), BF16),
        scratch_shapes=[pltpu.VMEM((lc + t, a_width), BF16), pltpu.VMEM((lc + t, a_width), BF16),
                        pltpu.VMEM((rows, lc + t), F32), pltpu.VMEM((rows, lc + t), F32),
                        pltpu.VMEM((rows, lc + t), BF16)],
        compiler_params=pltpu.CompilerParams(dimension_semantics=("arbitrary",),
                                             vmem_limit_bytes=V7X_VMEM_LIMIT),
        name="band_attention_sample",
    )(table, z, z, z, cache_k, cache_v)


def _split2(x):
    a = x.astype(BF16)
    return a, (x - a.astype(F32)).astype(BF16)


def _recur_kernel(layer, blk, pos0, n_heads, u_ref, q_ref, f_ref, i_ref, g_ref, hist0_ref, st0_ref, lbraw_ref,
                  normw_ref, poolw_ref, pscale_ref, o_ref, st_ref, state, hist, dmask, inter, rows_b, rows_k):
    t_idx = pl.program_id(1)
    tm, width = u_ref.shape
    hd = width // n_heads
    sb = min(SUB_BLOCK, blk)
    assert blk // sb in (1, 2) and tm % blk == 0

    @pl.when(t_idx == 0)
    def _():
        state[...] = st0_ref[...]
        hist[...] = hist0_ref[...]
        ri = lax.broadcasted_iota(jnp.int32, (tm, tm), 0)
        ci = lax.broadcasted_iota(jnp.int32, (tm, tm), 1)
        same_sb = _div(ri, sb) == _div(ci, sb)
        dmask[0] = jnp.where(same_sb & (ci <= ri), 1.0, 0.0).astype(BF16)
        dmask[1] = jnp.where(same_sb, 1.0, 0.0).astype(BF16)

    ext = jnp.concatenate([hist[...], u_ref[...]], axis=0)
    hist[...] = ext[tm:, :]
    s2 = ext + pltpu.roll(ext, 1, 0)
    s4 = s2 + pltpu.roll(s2, 2, 0)
    s8 = s4 + pltpu.roll(s4, 4, 0)
    s16 = s8 + pltpu.roll(s8, 8, 0)
    lane = lax.broadcasted_iota(jnp.int32, (1, width), 1)
    grp = _div(lane, width // len(POOL_WINDOWS))
    wsum = jnp.where(grp == 0, s2, jnp.where(grp == 1, s4, jnp.where(grp == 2, s8, s16)))
    wlen = jnp.where(grp == 0, 2.0, jnp.where(grp == 1, 4.0, jnp.where(grp == 2, 8.0, 16.0)))
    row = lax.broadcasted_iota(jnp.int32, (POOL_PAD + tm, 1), 0)
    pos = (row + (pos0 - POOL_PAD) + t_idx * tm).astype(F32)
    cnt = jnp.maximum(jnp.minimum(pos + 1.0, wlen), 1.0)
    dev = (wsum / cnt - ext)[POOL_PAD:, :]
    o_pool = _dot(dev.astype(BF16), poolw_ref[...]) * pscale_ref[...]
    o_ref[:, 0:width] = o_pool.astype(o_ref.dtype)

    raw = lbraw_ref[...]
    sm = jnp.exp(raw - jnp.max(raw, axis=0, keepdims=True))
    sm = sm / jnp.sum(sm, axis=0, keepdims=True)
    lb = jnp.zeros((1, width), F32)
    for j in range(1, layer + 1):
        lb = lb + sm[j:j + 1, :]
    forget = lb + (1.0 - lb) * _sigmoid(f_ref[...])
    log_f = jnp.log(forget)
    k_in = 1.0 - forget
    qx = q_ref[...]
    qf = qx * _sigmoid(qx)
    vb = i_ref[...].astype(BF16)

    lf2 = jnp.concatenate(_split2(log_f), axis=1)

    def decay_sum(mask01):
        r = _dot(mask01, lf2)
        return r[:, 0:width] + r[:, width:]

    b_rel = decay_sum(dmask[0])
    sb_tot = decay_sum(dmask[1])
    if blk > sb:
        second = (lax.broadcasted_iota(jnp.int32, (tm, 1), 0) & (blk - 1)) >= sb
        prev_tot = pltpu.roll(sb_tot, sb, 0)
        next_tot = pltpu.roll(sb_tot, tm - sb, 0)
        b = b_rel + jnp.where(second, prev_tot, 0.0)
        blk_tot = sb_tot + jnp.where(second, prev_tot, next_tot)
    else:
        b, blk_tot = b_rel, sb_tot

    q_rel = qf * jnp.exp(b_rel)
    k_rel = (k_in * jnp.exp(-b_rel)).astype(BF16)
    k_end = (k_in * jnp.exp(sb_tot - b_rel)).astype(BF16)
    q_abs = (qf * jnp.exp(b)).astype(BF16)
    k_tail = (k_in * jnp.exp(blk_tot - b)).astype(BF16)

    hb = n_heads * blk
    n_chunks = tm // blk
    stack_head = _div(lax.broadcasted_iota(jnp.int32, (hb, 1), 0), blk) == _div(lane, hd)
    tq = lax.broadcasted_iota(jnp.int32, (hb, blk), 0) & (blk - 1)
    ts = lax.broadcasted_iota(jnp.int32, (hb, blk), 1)
    m_intra = (_div(tq, sb) == _div(ts, sb)) & (ts <= tq)
    m_cross = _div(ts, sb) < _div(tq, sb)
    bd = (_div(lax.broadcasted_iota(jnp.int32, (width, width), 0), hd)
          == _div(lax.broadcasted_iota(jnp.int32, (width, width), 1), hd))
    ones_bd = jnp.where(bd, 1.0, 0.0).astype(BF16)
    gx = g_ref[...]
    out_gate = normw_ref[...] * (gx * _sigmoid(gx))

    def write_output(o):
        sq_hi, sq_lo = _split2(o * o)
        ms = (_dot(sq_hi, ones_bd) + _dot(sq_lo, ones_bd)) * (1.0 / hd)
        o_ref[:, width:] = (o * lax.rsqrt(ms + RMS_EPS) * out_gate).astype(o_ref.dtype)

    outs = []
    for c in range(n_chunks):
        rs = slice(c * blk, (c + 1) * blk)
        lhs = jnp.where(stack_head, jnp.concatenate([q_rel[rs]] * n_heads, axis=0), 0.0).astype(BF16)
        a = jnp.where(m_intra, _dot_nt(lhs, k_rel[rs]), 0.0)
        if blk > sb:
            a = a + jnp.where(m_cross, _dot_nt(lhs, k_end[rs]), 0.0)
        stacked = jnp.where(stack_head, _dot(a.astype(BF16), vb[rs]), 0.0)
        o_c = stacked[0:blk]
        for h in range(1, n_heads):
            o_c = o_c + stacked[h * blk:(h + 1) * blk]
        st = state[...]
        from_state = _dot_nt(q_abs[rs], st.astype(BF16))
        inter[rs, :] = from_state
        outs.append(o_c + from_state)
        decay_end = jnp.exp(blk_tot[c * blk:c * blk + 1, :])
        state[...] = st * decay_end + jnp.where(bd, _dot_tn(vb[rs], k_tail[rs]), 0.0)
    write_output(outs[0] if len(outs) == 1 else jnp.concatenate(outs, axis=0))

    @pl.when(jnp.min(sb_tot) <= DECAY_SAFE_LOG)
    def _():
        rows_b[...] = b
        rows_k[...] = k_in
        t_row = lax.broadcasted_iota(jnp.int32, (blk, 1), 0)

        def add_key_row(s, acc):
            parts = []
            for c in range(n_chunks):
                rs = slice(c * blk, (c + 1) * blk)
                r = c * blk + s
                rel = jnp.minimum(b[rs] - rows_b[pl.ds(r, 1), :], 0.0)
                term = jnp.where(t_row >= s, qf[rs] * jnp.exp(rel) * rows_k[pl.ds(r, 1), :], 0.0)
                parts.append(_dot(term.astype(BF16), ones_bd) * i_ref[pl.ds(r, 1), :])
            return acc + (parts[0] if n_chunks == 1 else jnp.concatenate(parts, axis=0))

        write_output(lax.fori_loop(0, blk, add_key_row, inter[...]))

    @pl.when(t_idx == pl.num_programs(1) - 1)
    def _():
        st_ref[...] = state[...]


def _recur_call(z, hist0, st0, lbraw, normw, poolw, pscale, layer, state_layer, batch, seq, blk, pos0, col0):
    width = st0.shape[-1]
    tm = min(256, seq)
    assert seq % tm == 0 and tm % blk == 0
    nt = seq // tm
    col = lambda c: (lambda b, t: (b * nt + t, c))
    per_batch = lambda b, t: (state_layer, b, 0, 0)
    consts = (normw, poolw, pscale)
    return pl.pallas_call(
        functools.partial(_recur_kernel, layer, blk, pos0, C_HEADS),
        grid=(batch, nt),
        in_specs=[pl.BlockSpec((tm, width), col(col0 + j)) for j in range(5)]
        + [pl.BlockSpec((None, None, POOL_PAD, width), per_batch),
           pl.BlockSpec((None, None, width, width), per_batch),
           pl.BlockSpec(lbraw.shape, lambda b, t: (0, 0), pipeline_mode=pl.Buffered(1))]
        + [_layer_spec(c, layer) for c in consts],
        out_specs=[pl.BlockSpec((tm, 2 * width), lambda b, t: (b * nt + t, 0)),
                   pl.BlockSpec((None, width, width), lambda b, t: (b, 0, 0))],
        out_shape=[jax.ShapeDtypeStruct((batch * seq, 2 * width), BF16),
                   jax.ShapeDtypeStruct((batch, width, width), F32)],
        scratch_shapes=[pltpu.VMEM((width, width), F32), pltpu.VMEM((POOL_PAD, width), F32),
                        pltpu.VMEM((2, tm, tm), BF16), pltpu.VMEM((tm, width), F32),
                        pltpu.VMEM((tm, width), F32), pltpu.VMEM((tm, width), F32)],
        compiler_params=pltpu.CompilerParams(dimension_semantics=("parallel", "arbitrary"),
                                             vmem_limit_bytes=V7X_VMEM_LIMIT),
        name="pool_hgrn_mixer",
    )(z, z, z, z, z, hist0, st0, lbraw, *consts)


def _block_diag(blocks):
    g = blocks.shape[-3]
    zero = jnp.zeros_like(blocks[..., 0, :, :])
    rows = [jnp.concatenate([blocks[..., h, :, :] if j == h else zero for j in range(g)], axis=-1)
            for h in range(g)]
    return jnp.concatenate(rows, axis=-2)


def _state_from_kernel(st, n_heads):
    b, w, _ = st.shape
    hd = w // n_heads
    blocks = jnp.stack([st[:, h * hd:(h + 1) * hd, h * hd:(h + 1) * hd] for h in range(n_heads)], axis=1)
    return jnp.swapaxes(blocks, -1, -2)


def kernel(x_prompt, x_sample, p_prompt, p_sample, cache_attn_k, cache_attn_v, state_pool, state_hgrn, ffn1_w_gu, ffn1_w_down, w_in, attn_rel_bias, pool_w, pool_scale, hgrn_lower_bounds, hgrn_norm_w, w_out, ffn2_w_gu, ffn2_w_down, ple_w_gate, ple_w_proj, ln_g, ln_b):
    depth = w_in.shape[0]
    alpha = float((2 * depth) ** 0.25)
    nb, seq, d = x_prompt.shape
    db, dt, _ = x_sample.shape
    a_width = A_HEADS * HEAD_DIM
    c_width = C_HEADS * HEAD_DIM
    in_width = w_in.shape[-1]
    col0 = 3 * a_width // c_width
    past_rows = min(A_BACK * CHUNK, seq)
    assert dt >= POOL_HIST and seq >= POOL_HIST and PAST_LEN >= POOL_HIST

    bf = lambda w: w.astype(BF16)
    w1gu, w1dn, win, wout = bf(ffn1_w_gu), bf(ffn1_w_down), bf(w_in), bf(w_out)
    w2gu, w2dn, wg, wp = bf(ffn2_w_gu), bf(ffn2_w_down), bf(ple_w_gate), bf(ple_w_proj)
    poolw = bf(_block_diag(pool_w))
    pscale = pool_scale.astype(F32)[:, None, :]
    normw = jnp.tile(hgrn_norm_w.astype(F32), (1, C_HEADS))[:, None, :]
    lbraw = hgrn_lower_bounds.astype(F32)
    pp = p_prompt.reshape(depth, nb * seq, -1)
    ps = p_sample.reshape(depth, db * dt, -1)

    zero_hist = jnp.zeros((1, nb, POOL_PAD, c_width), F32)
    zero_state = jnp.zeros((1, nb, c_width, c_width), F32)
    hist_s = jnp.pad(state_pool.astype(F32), ((0, 0), (0, 0), (POOL_PAD - POOL_HIST, 0), (0, 0)))
    state_s = _block_diag(jnp.swapaxes(state_hgrn.astype(F32), -1, -2))

    lc = cache_attn_k.shape[2]
    cache_k = bf(cache_attn_k).reshape(depth, db, lc, a_width)
    cache_v = bf(cache_attn_v).reshape(depth, db, lc, a_width)

    xp = x_prompt.reshape(nb * seq, d)
    xs = x_sample.reshape(db * dt, d)
    outs = [[] for _ in range(8)]

    for i in range(depth):
        rel_table = attn_rel_bias[i].astype(F32)

        xp, zp = _pre_call(xp, w1gu, w1dn, ln_g, ln_b, win, i, alpha)
        oa = _attn_prompt_call(zp, rel_table, nb, seq, a_width)
        obc, st = _recur_call(zp, zero_hist, zero_state, lbraw, normw, poolw, pscale, i, 0, nb, seq, CHUNK, 0, col0)
        xp = _post_call(xp, oa, obc, pp, wout, w2gu, w2dn, wg, wp, ln_g, ln_b, i, alpha)
        zp3 = zp.reshape(nb, seq, in_width)
        outs[0].append(zp3[:, seq - past_rows:, a_width:2 * a_width].reshape(nb, past_rows, A_HEADS, HEAD_DIM))
        outs[1].append(zp3[:, seq - past_rows:, 2 * a_width:3 * a_width].reshape(nb, past_rows, A_HEADS, HEAD_DIM))
        outs[2].append(zp3[:, seq - POOL_HIST:, 3 * a_width:3 * a_width + c_width])
        outs[3].append(_state_from_kernel(st, C_HEADS))

        xs, zs = _pre_call(xs, w1gu, w1dn, ln_g, ln_b, win, i, alpha)
        oa = _attn_sample_call(zs, cache_k, cache_v, rel_table, i, db, dt, a_width)
        obc, st = _recur_call(zs, hist_s, state_s, lbraw, normw, poolw, pscale, i, i, db, dt, dt, PAST_LEN, col0)
        xs = _post_call(xs, oa, obc, ps, wout, w2gu, w2dn, wg, wp, ln_g, ln_b, i, alpha)
        zs3 = zs.reshape(db, dt, in_width)
        outs[4].append(zs3[:, :, a_width:2 * a_width].reshape(db, dt, A_HEADS, HEAD_DIM))
        outs[5].append(zs3[:, :, 2 * a_width:3 * a_width].reshape(db, dt, A_HEADS, HEAD_DIM))
        outs[6].append(zs3[:, dt - POOL_HIST:, 3 * a_width:3 * a_width + c_width])
        outs[7].append(_state_from_kernel(st, C_HEADS))

    stacked = [jnp.stack(o, axis=0) for o in outs]
    return (xp.reshape(nb, seq, d), xs.reshape(db, dt, d), *stacked)
```

```python
import functools

import jax
import jax.numpy as jnp
from jax import lax
from jax.experimental import pallas as pl
from jax.experimental.pallas import tpu as pltpu

F32 = jnp.float32
BF16 = jnp.bfloat16

PAST_LEN = 2048
CHUNK = 64
A_BACK = 8
HEAD_DIM = 64
A_HEADS = 8
C_HEADS = 4
REL_MAX = 128
POOL_WINDOWS = (2, 4, 8, 16)
POOL_HIST = max(POOL_WINDOWS) - 1
POOL_PAD = POOL_HIST + 1
SUB_BLOCK = 32
DECAY_SAFE_LOG = -60.0
TOKEN_SUB_ROWS = 256
LN_EPS = 1e-5
RMS_EPS = 1e-6
NEG_INF = -1e30
LOG2E = 1.4426950408889634

V7X_LANES = 128
V7X_MXU_WIDTH = 256
V7X_VMEM_LIMIT = 56 * 1024 * 1024


def _dot(a, b):
    return jnp.dot(a, b, preferred_element_type=F32)


def _dot_nt(a, b):
    return lax.dot_general(a, b, (((1,), (1,)), ((), ())), preferred_element_type=F32)


def _dot_tn(a, b):
    return lax.dot_general(a, b, (((0,), (0,)), ((), ())), preferred_element_type=F32)


def _layer_norm(y, g, b):
    mu = jnp.mean(y, axis=-1, keepdims=True)
    d = y - mu
    var = jnp.mean(d * d, axis=-1, keepdims=True)
    return d * lax.rsqrt(var + LN_EPS) * g + b


def _sigmoid(x):
    return 1.0 / (1.0 + jnp.exp(-x))


def _log2(n):
    assert n > 0 and n & (n - 1) == 0, n
    return n.bit_length() - 1


def _div(x, n):
    return x >> _log2(n)


def _swiglu_stages(xb, wgu_ref, wdn_ref, n_split):
    hidden = wdn_ref.shape[0]
    groups = _hidden_groups(hidden, n_split)
    gate_up = []
    for lo, hi in groups:
        gate_up.append((_dot(xb, wgu_ref[:, lo:hi]), _dot(xb, wgu_ref[:, hidden + lo:hidden + hi])))
        yield
    acc = None
    for (lo, hi), (gate, up) in zip(groups, gate_up):
        act = (gate * _sigmoid(gate) * up).astype(BF16)
        part = _dot(act, wdn_ref[lo:hi, :])
        acc = part if acc is None else acc + part
        yield
    return acc


def _hidden_groups(hidden, n_split):
    if hidden % V7X_MXU_WIDTH:
        return [(0, hidden)]
    tiles = hidden // V7X_MXU_WIDTH
    bounds = [V7X_MXU_WIDTH * ((tiles * g + n_split - 1) // n_split) for g in range(n_split + 1)]
    return [(lo, hi) for lo, hi in zip(bounds[:-1], bounds[1:]) if hi > lo]


def _run_interleaved(chains):
    live = list(chains)
    while live:
        for ch in list(live):
            try:
                next(ch)
            except StopIteration:
                live.remove(ch)


def _sub_tiles(n_rows):
    sub = min(TOKEN_SUB_ROWS, n_rows)
    return [slice(r, r + sub) for r in range(0, n_rows, sub)]


def _pre_kernel(alpha, n_split, x_ref, wgu_ref, wdn_ref, lng_ref, lnb_ref, win_ref, xo_ref, z_ref):
    def chain(rows):
        x = x_ref[rows, :]
        ffn = yield from _swiglu_stages(x.astype(BF16), wgu_ref, wdn_ref, n_split)
        xn = _layer_norm(alpha * x + 0.5 * ffn, lng_ref[0:1, :], lnb_ref[0:1, :])
        xo_ref[rows, :] = xn
        z_ref[rows, :] = _dot(xn.astype(BF16), win_ref[...])
        yield

    _run_interleaved([chain(rows) for rows in _sub_tiles(x_ref.shape[0])])


def _post_kernel(alpha, n_split, x_ref, oa_ref, obc_ref, p_ref, wout_ref, wgu_ref, wdn_ref, wg_ref, wp_ref,
                 lng_ref, lnb_ref, xo_ref):
    wa = oa_ref.shape[1]

    def chain(rows):
        mixed = _dot(oa_ref[rows, :], wout_ref[0:wa, :]) + _dot(obc_ref[rows, :], wout_ref[wa:, :])
        proj = _dot(p_ref[rows, :].astype(BF16), wp_ref[...])
        yield
        x = _layer_norm(alpha * x_ref[rows, :] + mixed, lng_ref[1:2, :], lnb_ref[1:2, :])
        ffn = yield from _swiglu_stages(x.astype(BF16), wgu_ref, wdn_ref, n_split)
        x = _layer_norm(alpha * x + 0.5 * ffn, lng_ref[2:3, :], lnb_ref[2:3, :])
        emb = _sigmoid(_dot(x.astype(BF16), wg_ref[...])) * proj
        yield
        xo_ref[rows, :] = _layer_norm(alpha * x + emb, lng_ref[3:4, :], lnb_ref[3:4, :])

    _run_interleaved([chain(rows) for rows in _sub_tiles(x_ref.shape[0])])


def _layer_spec(stacked, layer):
    rest = stacked.shape[1:]
    return pl.BlockSpec((None,) + rest, lambda *_: (layer,) + (0,) * len(rest), pipeline_mode=pl.Buffered(1))


def _token_tile(n):
    for tm in (2 * TOKEN_SUB_ROWS, TOKEN_SUB_ROWS, 128, 64, 32, 16, 8):
        if n % tm == 0:
            return tm
    raise ValueError(f"unsupported token count {n}")


def _pre_call(x, wgu, wdn, lng, lnb, win, layer, alpha):
    n, d = x.shape
    zw = win.shape[-1]
    tm = _token_tile(n)
    row = lambda i: (i, 0)
    consts = (wgu, wdn, lng, lnb, win)
    return pl.pallas_call(
        functools.partial(_pre_kernel, alpha, 2),
        grid=(n // tm,),
        in_specs=[pl.BlockSpec((tm, d), row)] + [_layer_spec(c, layer) for c in consts],
        out_specs=[pl.BlockSpec((tm, d), row), pl.BlockSpec((tm, zw), row)],
        out_shape=[jax.ShapeDtypeStruct((n, d), F32), jax.ShapeDtypeStruct((n, zw), F32)],
        compiler_params=pltpu.CompilerParams(dimension_semantics=("parallel",),
                                             vmem_limit_bytes=V7X_VMEM_LIMIT),
        name="pre_mixer_tokens",
    )(x, *consts)


def _post_call(x, oa, obc, p, wout, wgu, wdn, wg, wp, lng, lnb, layer, alpha):
    n, d = x.shape
    tm = _token_tile(n)
    row = lambda i: (i, 0)
    consts = (wout, wgu, wdn, wg, wp, lng, lnb)
    return pl.pallas_call(
        functools.partial(_post_kernel, alpha, 2),
        grid=(n // tm,),
        in_specs=[pl.BlockSpec((tm, d), row), pl.BlockSpec((tm, oa.shape[1]), row),
                  pl.BlockSpec((tm, obc.shape[1]), row),
                  pl.BlockSpec((None, tm, p.shape[-1]), lambda i: (layer, i, 0))]
        + [_layer_spec(c, layer) for c in consts],
        out_specs=pl.BlockSpec((tm, d), row),
        out_shape=jax.ShapeDtypeStruct((n, d), F32),
        compiler_params=pltpu.CompilerParams(dimension_semantics=("parallel",),
                                             vmem_limit_bytes=V7X_VMEM_LIMIT),
        name="post_mixer_tokens",
    )(x, oa, obc, p, *consts)


def _qk_scores(q_ref, kcat, s_out, r0, n_q, k0, n_k):
    scale = HEAD_DIM ** -0.5 * LOG2E
    low = lax.broadcasted_iota(jnp.int32, (1, V7X_LANES), 1) < HEAD_DIM
    for hp in range(q_ref.shape[1] // V7X_LANES):
        ls = slice(hp * V7X_LANES, (hp + 1) * V7X_LANES)
        q = q_ref[r0:r0 + n_q, ls] * scale
        lhs = jnp.concatenate([jnp.where(low, q, 0.0), jnp.where(low, 0.0, q)], axis=0).astype(BF16)
        s_out[2 * hp * n_q:2 * (hp + 1) * n_q, :] = _dot_nt(lhs, kcat[k0:k0 + n_k, ls])


def _softmax_weights(s_in, bias_ref, key_valid, p_out):
    s = s_in[...] + bias_ref[...]
    if key_valid is not None:
        s = jnp.where(key_valid, s, NEG_INF)
    p_out[...] = jnp.exp2(s - jnp.max(s, axis=-1, keepdims=True)).astype(BF16)


def _weighted_values(p_in, vcat, o_ref, r0, n_q, k0, n_k):
    low = lax.broadcasted_iota(jnp.int32, (1, V7X_LANES), 1) < HEAD_DIM
    ones = jnp.ones((n_k, V7X_LANES), BF16)
    for hp in range(o_ref.shape[1] // V7X_LANES):
        ls = slice(hp * V7X_LANES, (hp + 1) * V7X_LANES)
        rows = slice(2 * hp * n_q, 2 * (hp + 1) * n_q)
        pv = _dot(p_in[rows, :], jnp.concatenate([vcat[k0:k0 + n_k, ls], ones], axis=1))
        o2 = pv[:, :V7X_LANES] * (1.0 / pv[:, V7X_LANES:])
        o_ref[r0:r0 + n_q, ls] = jnp.where(low, o2[:n_q], o2[n_q:]).astype(o_ref.dtype)


def _build_rel_bias(table_ref, bias_ref, n_q, n_k, offset):
    n_heads = table_ref.shape[0]
    clip = lambda d: min(max(d, -(CHUNK - 1)), REL_MAX) + (CHUNK - 1)
    heads_per_pass = 4
    for j0 in range(0, n_k, V7X_LANES):
        jw = min(V7X_LANES, n_k - j0)
        r_lo, r_hi = clip(offset - (j0 + jw - 1)), clip(offset + n_q - 1 - j0)
        d = (offset - j0 + lax.broadcasted_iota(jnp.int32, (n_q, jw), 0)
             - lax.broadcasted_iota(jnp.int32, (n_q, jw), 1))
        idx = jnp.clip(d, -(CHUNK - 1), REL_MAX) + (CHUNK - 1)
        for h0 in range(0, n_heads, heads_per_pass):
            heads = range(h0, min(h0 + heads_per_pass, n_heads))
            if r_lo == r_hi:
                vals = [jnp.full((n_q, jw), table_ref[h, r_lo], F32) for h in heads]
            else:
                def pick(r, acc, heads=heads, idx=idx):
                    hit = idx == r
                    return tuple(jnp.where(hit, table_ref[h, r], a) for h, a in zip(heads, acc))
                vals = lax.fori_loop(r_lo, r_hi + 1, pick, tuple(jnp.zeros((n_q, jw), F32) for _ in heads))
            for h, v in zip(heads, vals):
                bias_ref[h * n_q:(h + 1) * n_q, j0:j0 + jw] = v * LOG2E


def _attn_prompt_kernel(qb_rows, table_ref, q_ref, kp_ref, kc_ref, vp_ref, vc_ref, o_ref,
                        kcat, vcat, bias_ref, s_buf, p_buf):
    i = pl.program_id(1)
    band = (A_BACK + 1) * CHUNK
    past = A_BACK * CHUNK
    n_chunks = qb_rows // CHUNK

    @pl.when((pl.program_id(0) == 0) & (i == 0))
    def _():
        _build_rel_bias(table_ref, bias_ref, CHUNK, band, past)

    def cast_rows(j, carry):
        r = pl.multiple_of(j * CHUNK, CHUNK)
        kcat[pl.ds(r, CHUNK), :] = kp_ref[pl.ds(r, CHUNK), :].astype(BF16)
        kcat[pl.ds(qb_rows + r, CHUNK), :] = kc_ref[pl.ds(r, CHUNK), :].astype(BF16)
        vcat[pl.ds(r, CHUNK), :] = vp_ref[pl.ds(r, CHUNK), :].astype(BF16)
        vcat[pl.ds(qb_rows + r, CHUNK), :] = vc_ref[pl.ds(r, CHUNK), :].astype(BF16)
        return carry

    lax.fori_loop(0, n_chunks, cast_rows, 0)
    key_idx = lax.broadcasted_iota(jnp.int32, (1, band), 1)

    _qk_scores(q_ref, kcat, s_buf.at[0], 0, CHUNK, 0, band)
    for c in range(n_chunks + 1):
        if c + 1 < n_chunks:
            _qk_scores(q_ref, kcat, s_buf.at[(c + 1) % 2], (c + 1) * CHUNK, CHUNK, (c + 1) * CHUNK, band)
        if c < n_chunks:
            key_valid = key_idx >= past - c * CHUNK - i * qb_rows
            _softmax_weights(s_buf.at[c % 2], bias_ref, key_valid, p_buf.at[c % 2])
        if c >= 1:
            _weighted_values(p_buf.at[(c - 1) % 2], vcat, o_ref,
                             (c - 1) * CHUNK, CHUNK, (c - 1) * CHUNK, band)


def _attn_prompt_call(z, table, batch, seq, a_width):
    qb_rows = A_BACK * CHUNK
    assert seq % qb_rows == 0 and a_width % V7X_LANES == 0
    nq = seq // qb_rows
    band = (A_BACK + 1) * CHUNK
    rows = A_HEADS * CHUNK
    cur = lambda col: (lambda b, i: (b * nq + i, col))
    prev = lambda col: (lambda b, i: (b * nq + jnp.maximum(i - 1, 0), col))
    blk = (qb_rows, a_width)
    return pl.pallas_call(
        functools.partial(_attn_prompt_kernel, qb_rows),
        grid=(batch, nq),
        in_specs=[pl.BlockSpec(memory_space=pltpu.SMEM),
                  pl.BlockSpec(blk, cur(0)), pl.BlockSpec(blk, prev(1)), pl.BlockSpec(blk, cur(1)),
                  pl.BlockSpec(blk, prev(2)), pl.BlockSpec(blk, cur(2))],
        out_specs=pl.BlockSpec(blk, lambda b, i: (b * nq + i, 0)),
        out_shape=jax.ShapeDtypeStruct((batch * seq, a_width), BF16),
        scratch_shapes=[pltpu.VMEM((2 * qb_rows, a_width), BF16), pltpu.VMEM((2 * qb_rows, a_width), BF16),
                        pltpu.VMEM((rows, band), F32), pltpu.VMEM((2, rows, band), F32),
                        pltpu.VMEM((2, rows, band), BF16)],
        compiler_params=pltpu.CompilerParams(dimension_semantics=("arbitrary", "arbitrary"),
                                             vmem_limit_bytes=V7X_VMEM_LIMIT),
        name="band_attention_prompt",
    )(table, z, z, z, z, z)


def _attn_sample_kernel(table_ref, q_ref, k_ref, v_ref, ck_ref, cv_ref, o_ref, kcat, vcat, bias_ref,
                        s_buf, p_buf):
    lc = ck_ref.shape[0]
    t = q_ref.shape[0]

    @pl.when(pl.program_id(0) == 0)
    def _():
        _build_rel_bias(table_ref, bias_ref, t, lc + t, lc)

    kcat[0:lc, :] = ck_ref[...].astype(BF16)
    kcat[lc:, :] = k_ref[...].astype(BF16)
    vcat[0:lc, :] = cv_ref[...].astype(BF16)
    vcat[lc:, :] = v_ref[...].astype(BF16)
    _qk_scores(q_ref, kcat, s_buf, 0, t, 0, lc + t)
    _softmax_weights(s_buf, bias_ref, None, p_buf)
    _weighted_values(p_buf, vcat, o_ref, 0, t, 0, lc + t)


def _attn_sample_call(z, cache_k, cache_v, table, layer, batch, t, a_width):
    lc = cache_k.shape[2]
    rows = A_HEADS * t
    blk = (t, a_width)
    cblk = (None, None, lc, a_width)
    cidx = lambda b: (layer, b, 0, 0)
    return pl.pallas_call(
        _attn_sample_kernel,
        grid=(batch,),
        in_specs=[pl.BlockSpec(memory_space=pltpu.SMEM),
                  pl.BlockSpec(blk, lambda b: (b, 0)), pl.BlockSpec(blk, lambda b: (b, 1)),
                  pl.BlockSpec(blk, lambda b: (b, 2)),
                  pl.BlockSpec(cblk, cidx), pl.BlockSpec(cblk, cidx)],
        out_specs=pl.BlockSpec(blk, lambda b: (b, 0)),
        out_shape=jax.ShapeDtypeStruct((batch * t, a_width), BF16),
        scratch_shapes=[pltpu.VMEM((lc + t, a_width), BF16), pltpu.VMEM((lc + t, a_width), BF16),
                        pltpu.VMEM((rows, lc + t), F32), pltpu.VMEM((rows, lc + t), F32),
                        pltpu.VMEM((rows, lc + t), BF16)],
        compiler_params=pltpu.CompilerParams(dimension_semantics=("arbitrary",),
                                             vmem_limit_bytes=V7X_VMEM_LIMIT),
        name="band_attention_sample",
    )(table, z, z, z, cache_k, cache_v)


def _split2(x):
    a = x.astype(BF16)
    return a, (x - a.astype(F32)).astype(BF16)


def _recur_kernel(layer, blk, pos0, n_heads, u_ref, q_ref, f_ref, i_ref, g_ref, hist0_ref, st0_ref, lbraw_ref,
                  normw_ref, poolw_ref, pscale_ref, o_ref, st_ref, state, hist, dmask, inter, rows_b, rows_k):
    t_idx = pl.program_id(1)
    tm, width = u_ref.shape
    hd = width // n_heads
    sb = min(SUB_BLOCK, blk)
    assert blk // sb in (1, 2) and tm % blk == 0

    @pl.when(t_idx == 0)
    def _():
        state[...] = st0_ref[...]
        hist[...] = hist0_ref[...]
        ri = lax.broadcasted_iota(jnp.int32, (tm, tm), 0)
        ci = lax.broadcasted_iota(jnp.int32, (tm, tm), 1)
        same_sb = _div(ri, sb) == _div(ci, sb)
        dmask[0] = jnp.where(same_sb & (ci <= ri), 1.0, 0.0).astype(BF16)
        dmask[1] = jnp.where(same_sb, 1.0, 0.0).astype(BF16)

    raw = lbraw_ref[...]
    sm = jnp.exp(raw - jnp.max(raw, axis=0, keepdims=True))
    sm = sm / jnp.sum(sm, axis=0, keepdims=True)
    lb = jnp.zeros((1, width), F32)
    for j in range(1, layer + 1):
        lb = lb + sm[j:j + 1, :]
    forget = lb + (1.0 - lb) * _sigmoid(f_ref[...])
    log_f = jnp.log(forget)
    k_in = 1.0 - forget
    qx = q_ref[...]
    qf = qx * _sigmoid(qx)
    vb = i_ref[...].astype(BF16)

    lf2 = jnp.concatenate(_split2(log_f), axis=1)

    def decay_sum(mask01):
        r = _dot(mask01, lf2)
        return r[:, 0:width] + r[:, width:]

    b_rel = decay_sum(dmask[0])
    sb_tot = decay_sum(dmask[1])
    if blk > sb:
        second = (lax.broadcasted_iota(jnp.int32, (tm, 1), 0) & (blk - 1)) >= sb
        prev_tot = pltpu.roll(sb_tot, sb, 0)
        next_tot = pltpu.roll(sb_tot, tm - sb, 0)
        b = b_rel + jnp.where(second, prev_tot, 0.0)
        blk_tot = sb_tot + jnp.where(second, prev_tot, next_tot)
    else:
        b, blk_tot = b_rel, sb_tot

    ext = jnp.concatenate([hist[...], u_ref[...]], axis=0)
    hist[...] = ext[tm:, :]
    s2 = ext + pltpu.roll(ext, 1, 0)
    s4 = s2 + pltpu.roll(s2, 2, 0)
    s8 = s4 + pltpu.roll(s4, 4, 0)
    s16 = s8 + pltpu.roll(s8, 8, 0)
    lane = lax.broadcasted_iota(jnp.int32, (1, width), 1)
    grp = _div(lane, width // len(POOL_WINDOWS))
    wsum = jnp.where(grp == 0, s2, jnp.where(grp == 1, s4, jnp.where(grp == 2, s8, s16)))
    wlen = jnp.where(grp == 0, 2.0, jnp.where(grp == 1, 4.0, jnp.where(grp == 2, 8.0, 16.0)))
    row = lax.broadcasted_iota(jnp.int32, (POOL_PAD + tm, 1), 0)
    pos = (row + (pos0 - POOL_PAD) + t_idx * tm).astype(F32)
    cnt = jnp.maximum(jnp.minimum(pos + 1.0, wlen), 1.0)
    dev = (wsum / cnt - ext)[POOL_PAD:, :]
    o_pool = _dot(dev.astype(BF16), poolw_ref[...]) * pscale_ref[...]
    o_ref[:, 0:width] = o_pool.astype(o_ref.dtype)

    q_rel = qf * jnp.exp(b_rel)
    k_rel = (k_in * jnp.exp(-b_rel)).astype(BF16)
    k_end = (k_in * jnp.exp(sb_tot - b_rel)).astype(BF16)
    q_abs = (qf * jnp.exp(b)).astype(BF16)
    k_tail = (k_in * jnp.exp(blk_tot - b)).astype(BF16)

    hb = n_heads * blk
    n_chunks = tm // blk
    stack_head = _div(lax.broadcasted_iota(jnp.int32, (hb, 1), 0), blk) == _div(lane, hd)
    tq = lax.broadcasted_iota(jnp.int32, (hb, blk), 0) & (blk - 1)
    ts = lax.broadcasted_iota(jnp.int32, (hb, blk), 1)
    m_intra = (_div(tq, sb) == _div(ts, sb)) & (ts <= tq)
    m_cross = _div(ts, sb) < _div(tq, sb)
    bd = (_div(lax.broadcasted_iota(jnp.int32, (width, width), 0), hd)
          == _div(lax.broadcasted_iota(jnp.int32, (width, width), 1), hd))
    ones_bd = jnp.where(bd, 1.0, 0.0).astype(BF16)
    gx = g_ref[...]
    out_gate = normw_ref[...] * (gx * _sigmoid(gx))

    def write_output(o):
        sq_hi, sq_lo = _split2(o * o)
        ms = (_dot(sq_hi, ones_bd) + _dot(sq_lo, ones_bd)) * (1.0 / hd)
        o_ref[:, width:] = (o * lax.rsqrt(ms + RMS_EPS) * out_gate).astype(o_ref.dtype)

    chunks = [slice(c * blk, (c + 1) * blk) for c in range(n_chunks)]
    state_in = [jnp.where(bd, _dot_tn(vb[rs], k_tail[rs]), 0.0) for rs in chunks]
    scores = []
    for rs in chunks:
        lhs = jnp.where(stack_head, jnp.concatenate([q_rel[rs]] * n_heads, axis=0), 0.0).astype(BF16)
        a = jnp.where(m_intra, _dot_nt(lhs, k_rel[rs]), 0.0)
        if blk > sb:
            a = a + jnp.where(m_cross, _dot_nt(lhs, k_end[rs]), 0.0)
        scores.append(a.astype(BF16))
    outs = []
    st = state[...]
    for c, rs in enumerate(chunks):
        from_state = _dot_nt(q_abs[rs], st.astype(BF16))
        stacked = jnp.where(stack_head, _dot(scores[c], vb[rs]), 0.0)
        o_c = stacked[0:blk]
        for h in range(1, n_heads):
            o_c = o_c + stacked[h * blk:(h + 1) * blk]
        inter[rs, :] = from_state
        outs.append(o_c + from_state)
        decay_end = jnp.exp(blk_tot[c * blk:c * blk + 1, :])
        st = st * decay_end + state_in[c]
    state[...] = st
    write_output(outs[0] if len(outs) == 1 else jnp.concatenate(outs, axis=0))

    @pl.when(jnp.min(sb_tot) <= DECAY_SAFE_LOG)
    def _():
        rows_b[...] = b
        rows_k[...] = k_in
        t_row = lax.broadcasted_iota(jnp.int32, (blk, 1), 0)

        def add_key_row(s, acc):
            parts = []
            for c in range(n_chunks):
                rs = slice(c * blk, (c + 1) * blk)
                r = c * blk + s
                rel = jnp.minimum(b[rs] - rows_b[pl.ds(r, 1), :], 0.0)
                term = jnp.where(t_row >= s, qf[rs] * jnp.exp(rel) * rows_k[pl.ds(r, 1), :], 0.0)
                parts.append(_dot(term.astype(BF16), ones_bd) * i_ref[pl.ds(r, 1), :])
            return acc + (parts[0] if n_chunks == 1 else jnp.concatenate(parts, axis=0))

        write_output(lax.fori_loop(0, blk, add_key_row, inter[...]))

    @pl.when(t_idx == pl.num_programs(1) - 1)
    def _():
        st_ref[...] = state[...]


def _recur_call(z, hist0, st0, lbraw, normw, poolw, pscale, layer, state_layer, batch, seq, blk, pos0, col0):
    width = st0.shape[-1]
    tm = min(256, seq)
    assert seq % tm == 0 and tm % blk == 0
    nt = seq // tm
    col = lambda c: (lambda b, t: (b * nt + t, c))
    per_batch = lambda b, t: (state_layer, b, 0, 0)
    consts = (normw, poolw, pscale)
    return pl.pallas_call(
        functools.partial(_recur_kernel, layer, blk, pos0, C_HEADS),
        grid=(batch, nt),
        in_specs=[pl.BlockSpec((tm, width), col(col0 + j)) for j in range(5)]
        + [pl.BlockSpec((None, None, POOL_PAD, width), per_batch),
           pl.BlockSpec((None, None, width, width), per_batch),
           pl.BlockSpec(lbraw.shape, lambda b, t: (0, 0), pipeline_mode=pl.Buffered(1))]
        + [_layer_spec(c, layer) for c in consts],
        out_specs=[pl.BlockSpec((tm, 2 * width), lambda b, t: (b * nt + t, 0)),
                   pl.BlockSpec((None, width, width), lambda b, t: (b, 0, 0))],
        out_shape=[jax.ShapeDtypeStruct((batch * seq, 2 * width), BF16),
                   jax.ShapeDtypeStruct((batch, width, width), F32)],
        scratch_shapes=[pltpu.VMEM((width, width), F32), pltpu.VMEM((POOL_PAD, width), F32),
                        pltpu.VMEM((2, tm, tm), BF16), pltpu.VMEM((tm, width), F32),
                        pltpu.VMEM((tm, width), F32), pltpu.VMEM((tm, width), F32)],
        compiler_params=pltpu.CompilerParams(dimension_semantics=("parallel", "arbitrary"),
                                             vmem_limit_bytes=V7X_VMEM_LIMIT),
        name="pool_hgrn_mixer",
    )(z, z, z, z, z, hist0, st0, lbraw, *consts)


def _block_diag(blocks):
    g = blocks.shape[-3]
    zero = jnp.zeros_like(blocks[..., 0, :, :])
    rows = [jnp.concatenate([blocks[..., h, :, :] if j == h else zero for j in range(g)], axis=-1)
            for h in range(g)]
    return jnp.concatenate(rows, axis=-2)


def _state_from_kernel(st, n_heads):
    b, w, _ = st.shape
    hd = w // n_heads
    blocks = jnp.stack([st[:, h * hd:(h + 1) * hd, h * hd:(h + 1) * hd] for h in range(n_heads)], axis=1)
    return jnp.swapaxes(blocks, -1, -2)


def kernel(x_prompt, x_sample, p_prompt, p_sample, cache_attn_k, cache_attn_v, state_pool, state_hgrn, ffn1_w_gu, ffn1_w_down, w_in, attn_rel_bias, pool_w, pool_scale, hgrn_lower_bounds, hgrn_norm_w, w_out, ffn2_w_gu, ffn2_w_down, ple_w_gate, ple_w_proj, ln_g, ln_b):
    depth = w_in.shape[0]
    alpha = float((2 * depth) ** 0.25)
    nb, seq, d = x_prompt.shape
    db, dt, _ = x_sample.shape
    a_width = A_HEADS * HEAD_DIM
    c_width = C_HEADS * HEAD_DIM
    in_width = w_in.shape[-1]
    col0 = 3 * a_width // c_width
    past_rows = min(A_BACK * CHUNK, seq)
    assert dt >= POOL_HIST and seq >= POOL_HIST and PAST_LEN >= POOL_HIST

    bf = lambda w: w.astype(BF16)
    w1gu, w1dn, win, wout = bf(ffn1_w_gu), bf(ffn1_w_down), bf(w_in), bf(w_out)
    w2gu, w2dn, wg, wp = bf(ffn2_w_gu), bf(ffn2_w_down), bf(ple_w_gate), bf(ple_w_proj)
    poolw = bf(_block_diag(pool_w))
    pscale = pool_scale.astype(F32)[:, None, :]
    normw = jnp.tile(hgrn_norm_w.astype(F32), (1, C_HEADS))[:, None, :]
    lbraw = hgrn_lower_bounds.astype(F32)
    pp = p_prompt.reshape(depth, nb * seq, -1)
    ps = p_sample.reshape(depth, db * dt, -1)

    zero_hist = jnp.zeros((1, nb, POOL_PAD, c_width), F32)
    zero_state = jnp.zeros((1, nb, c_width, c_width), F32)
    hist_s = jnp.pad(state_pool.astype(F32), ((0, 0), (0, 0), (POOL_PAD - POOL_HIST, 0), (0, 0)))
    state_s = _block_diag(jnp.swapaxes(state_hgrn.astype(F32), -1, -2))

    lc = cache_attn_k.shape[2]
    cache_k = cache_attn_k.reshape(depth, db, lc, a_width)
    cache_v = cache_attn_v.reshape(depth, db, lc, a_width)

    xp = x_prompt.reshape(nb * seq, d)
    xs = x_sample.reshape(db * dt, d)
    outs = [[] for _ in range(8)]

    for i in range(depth):
        rel_table = attn_rel_bias[i].astype(F32)

        xp, zp = _pre_call(xp, w1gu, w1dn, ln_g, ln_b, win, i, alpha)
        oa = _attn_prompt_call(zp, rel_table, nb, seq, a_width)
        obc, st = _recur_call(zp, zero_hist, zero_state, lbraw, normw, poolw, pscale, i, 0, nb, seq, CHUNK, 0, col0)
        xp = _post_call(xp, oa, obc, pp, wout, w2gu, w2dn, wg, wp, ln_g, ln_b, i, alpha)
        zp3 = zp.reshape(nb, seq, in_width)
        outs[0].append(zp3[:, seq - past_rows:, a_width:2 * a_width].reshape(nb, past_rows, A_HEADS, HEAD_DIM))
        outs[1].append(zp3[:, seq - past_rows:, 2 * a_width:3 * a_width].reshape(nb, past_rows, A_HEADS, HEAD_DIM))
        outs[2].append(zp3[:, seq - POOL_HIST:, 3 * a_width:3 * a_width + c_width])
        outs[3].append(_state_from_kernel(st, C_HEADS))

        xs, zs = _pre_call(xs, w1gu, w1dn, ln_g, ln_b, win, i, alpha)
        oa = _attn_sample_call(zs, cache_k, cache_v, rel_table, i, db, dt, a_width)
        obc, st = _recur_call(zs, hist_s, state_s, lbraw, normw, poolw, pscale, i, i, db, dt, dt, PAST_LEN, col0)
        xs = _post_call(xs, oa, obc, ps, wout, w2gu, w2dn, wg, wp, ln_g, ln_b, i, alpha)
        zs3 = zs.reshape(db, dt, in_width)
        outs[4].append(zs3[:, :, a_width:2 * a_width].reshape(db, dt, A_HEADS, HEAD_DIM))
        outs[5].append(zs3[:, :, 2 * a_width:3 * a_width].reshape(db, dt, A_HEADS, HEAD_DIM))
        outs[6].append(zs3[:, dt - POOL_HIST:, 3 * a_width:3 * a_width + c_width])
        outs[7].append(_state_from_kernel(st, C_HEADS))

    stacked = [jnp.stack(o, axis=0) for o in outs]
    return (xp.reshape(nb, seq, d), xs.reshape(db, dt, d), *stacked)
```

```python
import collections
import functools

import jax
import jax.numpy as jnp
from jax import lax
from jax.experimental import pallas as pl
from jax.experimental.pallas import tpu as pltpu

F32 = jnp.float32
BF16 = jnp.bfloat16

PAST_LEN = 2048
CHUNK = 64
A_BACK = 8
HEAD_DIM = 64
A_HEADS = 8
C_HEADS = 4
REL_MAX = 128
POOL_WINDOWS = (2, 4, 8, 16)
POOL_HIST = max(POOL_WINDOWS) - 1
POOL_PAD = POOL_HIST + 1
SUB_BLOCK = 32
DECAY_SAFE_LOG = -60.0
TOKEN_SUB_ROWS = 256
LN_EPS = 1e-5
RMS_EPS = 1e-6
NEG_INF = -1e30
LOG2E = 1.4426950408889634

V7X_LANES = 128
V7X_MXU_WIDTH = 256
V7X_VMEM_LIMIT = 56 * 1024 * 1024


def _dot(a, b):
    return jnp.dot(a, b, preferred_element_type=F32)


def _dot_nt(a, b):
    return lax.dot_general(a, b, (((1,), (1,)), ((), ())), preferred_element_type=F32)


def _dot_tn(a, b):
    return lax.dot_general(a, b, (((0,), (0,)), ((), ())), preferred_element_type=F32)


def _layer_norm(y, g, b):
    mu = jnp.mean(y, axis=-1, keepdims=True)
    d = y - mu
    var = jnp.mean(d * d, axis=-1, keepdims=True)
    return d * lax.rsqrt(var + LN_EPS) * g + b


def _sigmoid(x):
    return 1.0 / (1.0 + jnp.exp(-x))


def _log2(n):
    assert n > 0 and n & (n - 1) == 0, n
    return n.bit_length() - 1


def _div(x, n):
    return x >> _log2(n)


def _swiglu_stages(xb, wgu_ref, wdn_ref, n_split):
    hidden = wdn_ref.shape[0]
    groups = _hidden_groups(hidden, n_split)
    gate_up = []
    for lo, hi in groups:
        gate_up.append((_dot(xb, wgu_ref[:, lo:hi]), _dot(xb, wgu_ref[:, hidden + lo:hidden + hi])))
        yield
    acc = None
    for (lo, hi), (gate, up) in zip(groups, gate_up):
        act = (gate * _sigmoid(gate) * up).astype(BF16)
        part = _dot(act, wdn_ref[lo:hi, :])
        acc = part if acc is None else acc + part
        yield
    return acc


def _hidden_groups(hidden, n_split):
    if hidden % V7X_MXU_WIDTH:
        return [(0, hidden)]
    tiles = hidden // V7X_MXU_WIDTH
    bounds = [V7X_MXU_WIDTH * ((tiles * g + n_split - 1) // n_split) for g in range(n_split + 1)]
    return [(lo, hi) for lo, hi in zip(bounds[:-1], bounds[1:]) if hi > lo]


def _run_interleaved(chains):
    live = list(chains)
    while live:
        for ch in list(live):
            if ch not in live:
                continue
            try:
                next(ch)
            except StopIteration:
                live = [c for c in live if c is not ch]


def _sub_tiles(n_rows):
    sub = min(TOKEN_SUB_ROWS, n_rows)
    return [slice(r, r + sub) for r in range(0, n_rows, sub)]


def _pre_kernel(alpha, n_split, x_ref, wgu_ref, wdn_ref, lng_ref, lnb_ref, win_ref, xo_ref, z_ref):
    def chain(rows):
        x = x_ref[rows, :]
        ffn = yield from _swiglu_stages(x.astype(BF16), wgu_ref, wdn_ref, n_split)
        xn = _layer_norm(alpha * x + 0.5 * ffn, lng_ref[0:1, :], lnb_ref[0:1, :])
        xo_ref[rows, :] = xn
        z_ref[rows, :] = _dot(xn.astype(BF16), win_ref[...])
        yield

    _run_interleaved([chain(rows) for rows in _sub_tiles(x_ref.shape[0])])


def _post_chain(alpha, n_split, rows, x_ref, oa_ref, obc_ref, p_ref, wout_ref, wgu_ref, wdn_ref, wg_ref, wp_ref,
                lng_ref, lnb_ref, xo_ref):
    wa = oa_ref.shape[1]
    mixed = _dot(oa_ref[rows, :], wout_ref[0:wa, :]) + _dot(obc_ref[rows, :], wout_ref[wa:, :])
    proj = _dot(p_ref[rows, :].astype(BF16), wp_ref[...])
    yield
    x = _layer_norm(alpha * x_ref[rows, :] + mixed, lng_ref[1:2, :], lnb_ref[1:2, :])
    ffn = yield from _swiglu_stages(x.astype(BF16), wgu_ref, wdn_ref, n_split)
    x = _layer_norm(alpha * x + 0.5 * ffn, lng_ref[2:3, :], lnb_ref[2:3, :])
    emb = _sigmoid(_dot(x.astype(BF16), wg_ref[...])) * proj
    yield
    xo_ref[rows, :] = _layer_norm(alpha * x + emb, lng_ref[3:4, :], lnb_ref[3:4, :])


def _post_kernel(alpha, n_split, x_ref, oa_ref, obc_ref, p_ref, wout_ref, wgu_ref, wdn_ref, wg_ref, wp_ref,
                 lng_ref, lnb_ref, xo_ref):
    _run_interleaved([_post_chain(alpha, n_split, rows, x_ref, oa_ref, obc_ref, p_ref, wout_ref, wgu_ref, wdn_ref,
                                  wg_ref, wp_ref, lng_ref, lnb_ref, xo_ref) for rows in _sub_tiles(x_ref.shape[0])])


def _layer_spec(stacked, layer):
    rest = stacked.shape[1:]
    return pl.BlockSpec((None,) + rest, lambda *_: (layer,) + (0,) * len(rest), pipeline_mode=pl.Buffered(1))


def _token_tile(n):
    for tm in (2 * TOKEN_SUB_ROWS, TOKEN_SUB_ROWS, 128, 64, 32, 16, 8):
        if n % tm == 0:
            return tm
    raise ValueError(f"unsupported token count {n}")


def _pre_call(x, wgu, wdn, lng, lnb, win, layer, alpha):
    n, d = x.shape
    zw = win.shape[-1]
    tm = _token_tile(n)
    row = lambda i: (i, 0)
    consts = (wgu, wdn, lng, lnb, win)
    return pl.pallas_call(
        functools.partial(_pre_kernel, alpha, 2),
        grid=(n // tm,),
        in_specs=[pl.BlockSpec((tm, d), row)] + [_layer_spec(c, layer) for c in consts],
        out_specs=[pl.BlockSpec((tm, d), row), pl.BlockSpec((tm, zw), row)],
        out_shape=[jax.ShapeDtypeStruct((n, d), F32), jax.ShapeDtypeStruct((n, zw), F32)],
        compiler_params=pltpu.CompilerParams(dimension_semantics=("parallel",),
                                             vmem_limit_bytes=V7X_VMEM_LIMIT),
        name="pre_mixer_tokens",
    )(x, *consts)


def _post_call(x, oa, obc, p, wout, wgu, wdn, wg, wp, lng, lnb, layer, alpha):
    n, d = x.shape
    tm = _token_tile(n)
    row = lambda i: (i, 0)
    consts = (wout, wgu, wdn, wg, wp, lng, lnb)
    return pl.pallas_call(
        functools.partial(_post_kernel, alpha, 2),
        grid=(n // tm,),
        in_specs=[pl.BlockSpec((tm, d), row), pl.BlockSpec((tm, oa.shape[1]), row),
                  pl.BlockSpec((tm, obc.shape[1]), row),
                  pl.BlockSpec((None, tm, p.shape[-1]), lambda i: (layer, i, 0))]
        + [_layer_spec(c, layer) for c in consts],
        out_specs=pl.BlockSpec((tm, d), row),
        out_shape=jax.ShapeDtypeStruct((n, d), F32),
        compiler_params=pltpu.CompilerParams(dimension_semantics=("parallel",),
                                             vmem_limit_bytes=V7X_VMEM_LIMIT),
        name="post_mixer_tokens",
    )(x, oa, obc, p, *consts)


def _qk_scores(q_ref, kcat, s_out, r0, n_q, k0, n_k):
    scale = HEAD_DIM ** -0.5 * LOG2E
    low = lax.broadcasted_iota(jnp.int32, (1, V7X_LANES), 1) < HEAD_DIM
    for hp in range(q_ref.shape[1] // V7X_LANES):
        ls = slice(hp * V7X_LANES, (hp + 1) * V7X_LANES)
        q = q_ref[r0:r0 + n_q, ls] * scale
        lhs = jnp.concatenate([jnp.where(low, q, 0.0), jnp.where(low, 0.0, q)], axis=0).astype(BF16)
        s_out[2 * hp * n_q:2 * (hp + 1) * n_q, :] = _dot_nt(lhs, kcat[k0:k0 + n_k, ls])


def _softmax_weights(s_in, bias_ref, key_valid, p_out):
    s = s_in[...] + bias_ref[...]
    if key_valid is not None:
        s = jnp.where(key_valid, s, NEG_INF)
    p_out[...] = jnp.exp2(s - jnp.max(s, axis=-1, keepdims=True)).astype(BF16)


def _weighted_values(p_in, vcat, o_ref, r0, n_q, k0, n_k):
    low = lax.broadcasted_iota(jnp.int32, (1, V7X_LANES), 1) < HEAD_DIM
    ones = jnp.ones((n_k, V7X_LANES), BF16)
    for hp in range(o_ref.shape[1] // V7X_LANES):
        ls = slice(hp * V7X_LANES, (hp + 1) * V7X_LANES)
        rows = slice(2 * hp * n_q, 2 * (hp + 1) * n_q)
        pv = _dot(p_in[rows, :], jnp.concatenate([vcat[k0:k0 + n_k, ls], ones], axis=1))
        o2 = pv[:, :V7X_LANES] * (1.0 / pv[:, V7X_LANES:])
        o_ref[r0:r0 + n_q, ls] = jnp.where(low, o2[:n_q], o2[n_q:]).astype(o_ref.dtype)


def _build_rel_bias(table_ref, bias_ref, n_q, n_k, offset):
    n_heads = table_ref.shape[0]
    clip = lambda d: min(max(d, -(CHUNK - 1)), REL_MAX) + (CHUNK - 1)
    heads_per_pass = 4
    for j0 in range(0, n_k, V7X_LANES):
        jw = min(V7X_LANES, n_k - j0)
        r_lo, r_hi = clip(offset - (j0 + jw - 1)), clip(offset + n_q - 1 - j0)
        d = (offset - j0 + lax.broadcasted_iota(jnp.int32, (n_q, jw), 0)
             - lax.broadcasted_iota(jnp.int32, (n_q, jw), 1))
        idx = jnp.clip(d, -(CHUNK - 1), REL_MAX) + (CHUNK - 1)
        for h0 in range(0, n_heads, heads_per_pass):
            heads = range(h0, min(h0 + heads_per_pass, n_heads))
            if r_lo == r_hi:
                vals = [jnp.full((n_q, jw), table_ref[h, r_lo], F32) for h in heads]
            else:
                def pick(r, acc, heads=heads, idx=idx):
                    hit = idx == r
                    return tuple(jnp.where(hit, table_ref[h, r], a) for h, a in zip(heads, acc))
                vals = lax.fori_loop(r_lo, r_hi + 1, pick, tuple(jnp.zeros((n_q, jw), F32) for _ in heads))
            for h, v in zip(heads, vals):
                bias_ref[h * n_q:(h + 1) * n_q, j0:j0 + jw] = v * LOG2E


def _attn_prompt_kernel(qb_rows, table_ref, q_ref, kp_ref, kc_ref, vp_ref, vc_ref, o_ref,
                        kcat, vcat, bias_ref, s_buf, p_buf):
    i = pl.program_id(1)
    band = (A_BACK + 1) * CHUNK
    past = A_BACK * CHUNK
    n_chunks = qb_rows // CHUNK

    @pl.when((pl.program_id(0) == 0) & (i == 0))
    def _():
        _build_rel_bias(table_ref, bias_ref, CHUNK, band, past)

    def cast_rows(j, carry):
        r = pl.multiple_of(j * CHUNK, CHUNK)
        kcat[pl.ds(r, CHUNK), :] = kp_ref[pl.ds(r, CHUNK), :].astype(BF16)
        kcat[pl.ds(qb_rows + r, CHUNK), :] = kc_ref[pl.ds(r, CHUNK), :].astype(BF16)
        vcat[pl.ds(r, CHUNK), :] = vp_ref[pl.ds(r, CHUNK), :].astype(BF16)
        vcat[pl.ds(qb_rows + r, CHUNK), :] = vc_ref[pl.ds(r, CHUNK), :].astype(BF16)
        return carry

    lax.fori_loop(0, n_chunks, cast_rows, 0)
    key_idx = lax.broadcasted_iota(jnp.int32, (1, band), 1)

    _qk_scores(q_ref, kcat, s_buf.at[0], 0, CHUNK, 0, band)
    for c in range(n_chunks + 1):
        if c + 1 < n_chunks:
            _qk_scores(q_ref, kcat, s_buf.at[(c + 1) % 2], (c + 1) * CHUNK, CHUNK, (c + 1) * CHUNK, band)
        if c < n_chunks:
            key_valid = key_idx >= past - c * CHUNK - i * qb_rows
            _softmax_weights(s_buf.at[c % 2], bias_ref, key_valid, p_buf.at[c % 2])
        if c >= 1:
            _weighted_values(p_buf.at[(c - 1) % 2], vcat, o_ref,
                             (c - 1) * CHUNK, CHUNK, (c - 1) * CHUNK, band)


def _attn_prompt_call(z, table, batch, seq, a_width):
    qb_rows = A_BACK * CHUNK
    assert seq % qb_rows == 0 and a_width % V7X_LANES == 0
    nq = seq // qb_rows
    band = (A_BACK + 1) * CHUNK
    rows = A_HEADS * CHUNK
    cur = lambda col: (lambda b, i: (b * nq + i, col))
    prev = lambda col: (lambda b, i: (b * nq + jnp.maximum(i - 1, 0), col))
    blk = (qb_rows, a_width)
    return pl.pallas_call(
        functools.partial(_attn_prompt_kernel, qb_rows),
        grid=(batch, nq),
        in_specs=[pl.BlockSpec(memory_space=pltpu.SMEM),
                  pl.BlockSpec(blk, cur(0)), pl.BlockSpec(blk, prev(1)), pl.BlockSpec(blk, cur(1)),
                  pl.BlockSpec(blk, prev(2)), pl.BlockSpec(blk, cur(2))],
        out_specs=pl.BlockSpec(blk, lambda b, i: (b * nq + i, 0)),
        out_shape=jax.ShapeDtypeStruct((batch * seq, a_width), BF16),
        scratch_shapes=[pltpu.VMEM((2 * qb_rows, a_width), BF16), pltpu.VMEM((2 * qb_rows, a_width), BF16),
                        pltpu.VMEM((rows, band), F32), pltpu.VMEM((2, rows, band), F32),
                        pltpu.VMEM((2, rows, band), BF16)],
        compiler_params=pltpu.CompilerParams(dimension_semantics=("arbitrary", "arbitrary"),
                                             vmem_limit_bytes=V7X_VMEM_LIMIT),
        name="band_attention_prompt",
    )(table, z, z, z, z, z)


def _attn_sample_kernel(table_ref, q_ref, k_ref, v_ref, ck_ref, cv_ref, o_ref, kcat, vcat, bias_ref,
                        s_buf, p_buf):
    lc = ck_ref.shape[0]
    t = q_ref.shape[0]

    @pl.when(pl.program_id(0) == 0)
    def _():
        _build_rel_bias(table_ref, bias_ref, t, lc + t, lc)

    kcat[0:lc, :] = ck_ref[...].astype(BF16)
    kcat[lc:, :] = k_ref[...].astype(BF16)
    vcat[0:lc, :] = cv_ref[...].astype(BF16)
    vcat[lc:, :] = v_ref[...].astype(BF16)
    _qk_scores(q_ref, kcat, s_buf, 0, t, 0, lc + t)
    _softmax_weights(s_buf, bias_ref, None, p_buf)
    _weighted_values(p_buf, vcat, o_ref, 0, t, 0, lc + t)


def _attn_sample_call(z, cache_k, cache_v, table, layer, batch, t, a_width):
    lc = cache_k.shape[2]
    rows = A_HEADS * t
    blk = (t, a_width)
    cblk = (None, None, lc, a_width)
    cidx = lambda b: (layer, b, 0, 0)
    return pl.pallas_call(
        _attn_sample_kernel,
        grid=(batch,),
        in_specs=[pl.BlockSpec(memory_space=pltpu.SMEM),
                  pl.BlockSpec(blk, lambda b: (b, 0)), pl.BlockSpec(blk, lambda b: (b, 1)),
                  pl.BlockSpec(blk, lambda b: (b, 2)),
                  pl.BlockSpec(cblk, cidx), pl.BlockSpec(cblk, cidx)],
        out_specs=pl.BlockSpec(blk, lambda b: (b, 0)),
        out_shape=jax.ShapeDtypeStruct((batch * t, a_width), BF16),
        scratch_shapes=[pltpu.VMEM((lc + t, a_width), BF16), pltpu.VMEM((lc + t, a_width), BF16),
                        pltpu.VMEM((rows, lc + t), F32), pltpu.VMEM((rows, lc + t), F32),
                        pltpu.VMEM((rows, lc + t), BF16)],
        compiler_params=pltpu.CompilerParams(dimension_semantics=("arbitrary",),
                                             vmem_limit_bytes=V7X_VMEM_LIMIT),
        name="band_attention_sample",
    )(table, z, z, z, cache_k, cache_v)


def _split2(x):
    a = x.astype(BF16)
    return a, (x - a.astype(F32)).astype(BF16)


_RecurRefs = collections.namedtuple(
    "_RecurRefs", "u q f i g lbraw normw poolw pscale out state hist dmask inter rows_b rows_k")


def _recur_reset(R, hist0_ref, st0_ref, blk, tm):
    sb = min(SUB_BLOCK, blk)
    R.state[...] = st0_ref[...]
    R.hist[...] = hist0_ref[...]
    ri = lax.broadcasted_iota(jnp.int32, (tm, tm), 0)
    ci = lax.broadcasted_iota(jnp.int32, (tm, tm), 1)
    same_sb = _div(ri, sb) == _div(ci, sb)
    R.dmask[0] = jnp.where(same_sb & (ci <= ri), 1.0, 0.0).astype(BF16)
    R.dmask[1] = jnp.where(same_sb, 1.0, 0.0).astype(BF16)


def _recur_tile(layer, blk, pos0, n_heads, R, rows, t_tile, fallbacks):
    tm = rows.stop - rows.start
    width = R.u.shape[1]
    hd = width // n_heads
    sb = min(SUB_BLOCK, blk)
    assert blk // sb in (1, 2) and tm % blk == 0

    raw = R.lbraw[...]
    sm = jnp.exp(raw - jnp.max(raw, axis=0, keepdims=True))
    sm = sm / jnp.sum(sm, axis=0, keepdims=True)
    lb = jnp.zeros((1, width), F32)
    for j in range(1, layer + 1):
        lb = lb + sm[j:j + 1, :]
    forget = lb + (1.0 - lb) * _sigmoid(R.f[rows, :])
    log_f = jnp.log(forget)
    k_in = 1.0 - forget
    qx = R.q[rows, :]
    qf = qx * _sigmoid(qx)
    vb = R.i[rows, :].astype(BF16)

    lf2 = jnp.concatenate(_split2(log_f), axis=1)

    def decay_sum(mask01):
        r = _dot(mask01, lf2)
        return r[:, 0:width] + r[:, width:]

    b_rel = decay_sum(R.dmask[0])
    sb_tot = decay_sum(R.dmask[1])
    yield
    if blk > sb:
        second = (lax.broadcasted_iota(jnp.int32, (tm, 1), 0) & (blk - 1)) >= sb
        prev_tot = pltpu.roll(sb_tot, sb, 0)
        next_tot = pltpu.roll(sb_tot, tm - sb, 0)
        b = b_rel + jnp.where(second, prev_tot, 0.0)
        blk_tot = sb_tot + jnp.where(second, prev_tot, next_tot)
    else:
        b, blk_tot = b_rel, sb_tot

    ext = jnp.concatenate([R.hist[...], R.u[rows, :]], axis=0)
    R.hist[...] = ext[tm:, :]
    s2 = ext + pltpu.roll(ext, 1, 0)
    s4 = s2 + pltpu.roll(s2, 2, 0)
    s8 = s4 + pltpu.roll(s4, 4, 0)
    s16 = s8 + pltpu.roll(s8, 8, 0)
    lane = lax.broadcasted_iota(jnp.int32, (1, width), 1)
    grp = _div(lane, width // len(POOL_WINDOWS))
    wsum = jnp.where(grp == 0, s2, jnp.where(grp == 1, s4, jnp.where(grp == 2, s8, s16)))
    wlen = jnp.where(grp == 0, 2.0, jnp.where(grp == 1, 4.0, jnp.where(grp == 2, 8.0, 16.0)))
    row = lax.broadcasted_iota(jnp.int32, (POOL_PAD + tm, 1), 0)
    pos = (row + (pos0 - POOL_PAD) + t_tile * tm).astype(F32)
    cnt = jnp.maximum(jnp.minimum(pos + 1.0, wlen), 1.0)
    dev = (wsum / cnt - ext)[POOL_PAD:, :]
    o_pool = _dot(dev.astype(BF16), R.poolw[...]) * R.pscale[...]
    R.out[rows, 0:width] = o_pool.astype(R.out.dtype)

    q_rel = qf * jnp.exp(b_rel)
    k_rel = (k_in * jnp.exp(-b_rel)).astype(BF16)
    k_end = (k_in * jnp.exp(sb_tot - b_rel)).astype(BF16)
    q_abs = (qf * jnp.exp(b)).astype(BF16)
    k_tail = (k_in * jnp.exp(blk_tot - b)).astype(BF16)

    hb = n_heads * blk
    n_chunks = tm // blk
    stack_head = _div(lax.broadcasted_iota(jnp.int32, (hb, 1), 0), blk) == _div(lane, hd)
    tq = lax.broadcasted_iota(jnp.int32, (hb, blk), 0) & (blk - 1)
    ts = lax.broadcasted_iota(jnp.int32, (hb, blk), 1)
    m_intra = (_div(tq, sb) == _div(ts, sb)) & (ts <= tq)
    m_cross = _div(ts, sb) < _div(tq, sb)
    bd = (_div(lax.broadcasted_iota(jnp.int32, (width, width), 0), hd)
          == _div(lax.broadcasted_iota(jnp.int32, (width, width), 1), hd))
    ones_bd = jnp.where(bd, 1.0, 0.0).astype(BF16)
    gx = R.g[rows, :]
    out_gate = R.normw[...] * (gx * _sigmoid(gx))

    def write_output(o):
        sq_hi, sq_lo = _split2(o * o)
        ms = (_dot(sq_hi, ones_bd) + _dot(sq_lo, ones_bd)) * (1.0 / hd)
        R.out[rows, width:] = (o * lax.rsqrt(ms + RMS_EPS) * out_gate).astype(R.out.dtype)

    chunks = [slice(c * blk, (c + 1) * blk) for c in range(n_chunks)]
    state_in = [jnp.where(bd, _dot_tn(vb[rs], k_tail[rs]), 0.0) for rs in chunks]
    yield
    scores = []
    for rs in chunks:
        lhs = jnp.where(stack_head, jnp.concatenate([q_rel[rs]] * n_heads, axis=0), 0.0).astype(BF16)
        a = jnp.where(m_intra, _dot_nt(lhs, k_rel[rs]), 0.0)
        if blk > sb:
            a = a + jnp.where(m_cross, _dot_nt(lhs, k_end[rs]), 0.0)
        scores.append(a.astype(BF16))
    yield
    outs = []
    st = R.state[...]
    for c, rs in enumerate(chunks):
        from_state = _dot_nt(q_abs[rs], st.astype(BF16))
        stacked = jnp.where(stack_head, _dot(scores[c], vb[rs]), 0.0)
        o_c = stacked[0:blk]
        for h in range(1, n_heads):
            o_c = o_c + stacked[h * blk:(h + 1) * blk]
        R.inter[pl.ds(rows.start + c * blk, blk), :] = from_state
        outs.append(o_c + from_state)
        decay_end = jnp.exp(blk_tot[c * blk:c * blk + 1, :])
        st = st * decay_end + state_in[c]
        yield
    R.state[...] = st
    write_output(outs[0] if len(outs) == 1 else jnp.concatenate(outs, axis=0))

    def fallback():
        @pl.when(jnp.min(sb_tot) <= DECAY_SAFE_LOG)
        def _():
            R.rows_b[...] = b
            R.rows_k[...] = k_in
            t_row = lax.broadcasted_iota(jnp.int32, (blk, 1), 0)

            def add_key_row(s, acc):
                parts = []
                for c, rs in enumerate(chunks):
                    r = c * blk + s
                    rel = jnp.minimum(b[rs] - R.rows_b[pl.ds(r, 1), :], 0.0)
                    term = jnp.where(t_row >= s, qf[rs] * jnp.exp(rel) * R.rows_k[pl.ds(r, 1), :], 0.0)
                    parts.append(_dot(term.astype(BF16), ones_bd) * R.i[pl.ds(rows.start + r, 1), :])
                return acc + (parts[0] if n_chunks == 1 else jnp.concatenate(parts, axis=0))

            write_output(lax.fori_loop(0, blk, add_key_row, R.inter[rows, :]))

    fallbacks.append(fallback)
    yield


def _recur_kernel(layer, blk, pos0, n_heads, u_ref, q_ref, f_ref, i_ref, g_ref, hist0_ref, st0_ref, lbraw_ref,
                  normw_ref, poolw_ref, pscale_ref, o_ref, st_ref, state, hist, dmask, inter, rows_b, rows_k):
    t_idx = pl.program_id(1)
    tm = u_ref.shape[0]
    R = _RecurRefs(u_ref, q_ref, f_ref, i_ref, g_ref, lbraw_ref, normw_ref, poolw_ref, pscale_ref, o_ref,
                   state, hist, dmask, inter, rows_b, rows_k)

    @pl.when(t_idx == 0)
    def _():
        _recur_reset(R, hist0_ref, st0_ref, blk, tm)

    fallbacks = []
    _run_interleaved([_recur_tile(layer, blk, pos0, n_heads, R, slice(0, tm), t_idx, fallbacks)])
    for fb in fallbacks:
        fb()

    @pl.when(t_idx == pl.num_programs(1) - 1)
    def _():
        st_ref[...] = state[...]


def _post_recur_kernel(alpha, n_split, layer, blk, n_heads, rtm, tiles_per_seq, n_tiles,
                       x_ref, oa_ref, p_ref, u_ref, q_ref, f_ref, i_ref, g_ref, hist0_ref, st0_ref, lbraw_ref,
                       normw_ref, poolw_ref, pscale_ref, wout_ref, wgu_ref, wdn_ref, wg_ref, wp_ref, lng_ref, lnb_ref,
                       xo_ref, st_ref, obc_buf, state, hist, dmask, inter, rows_b, rows_k):
    j = pl.program_id(0)
    tile = jnp.minimum(j, n_tiles - 1)
    t_seq = tile % tiles_per_seq
    tm = x_ref.shape[0]
    slot = j % 2
    R = _RecurRefs(u_ref, q_ref, f_ref, i_ref, g_ref, lbraw_ref, normw_ref, poolw_ref, pscale_ref,
                   obc_buf.at[slot], state, hist, dmask, inter, rows_b, rows_k)

    @pl.when(j == 0)
    def _():
        obc_buf[...] = jnp.zeros(obc_buf.shape, obc_buf.dtype)

    @pl.when(t_seq == 0)
    def _():
        _recur_reset(R, hist0_ref, st0_ref, blk, rtm)

    fallbacks = []

    def recur_chain():
        for k in range(tm // rtm):
            yield from _recur_tile(layer, blk, 0, n_heads, R, slice(k * rtm, (k + 1) * rtm),
                                   t_seq * (tm // rtm) + k, fallbacks)

    mixer = recur_chain()
    prev = obc_buf.at[1 - slot]
    tokens = [_post_chain(alpha, n_split, rows, x_ref, oa_ref, prev, p_ref, wout_ref, wgu_ref, wdn_ref, wg_ref, wp_ref,
                          lng_ref, lnb_ref, xo_ref) for rows in _sub_tiles(tm)]
    order = []
    for ch in tokens:
        order += [mixer, ch]
    _run_interleaved(order)
    for fb in fallbacks:
        fb()

    @pl.when((t_seq == tiles_per_seq - 1) & (j < n_tiles))
    def _():
        st_ref[...] = state[...]


RECUR_TILE = 256


def _recur_scratch(rows, tm, width):
    return [pltpu.VMEM((width, width), F32), pltpu.VMEM((POOL_PAD, width), F32), pltpu.VMEM((2, tm, tm), BF16),
            pltpu.VMEM((rows, width), F32), pltpu.VMEM((tm, width), F32), pltpu.VMEM((tm, width), F32)]


def _recur_call(z, hist0, st0, lbraw, normw, poolw, pscale, layer, state_layer, batch, seq, blk, pos0, col0):
    width = st0.shape[-1]
    tm = min(RECUR_TILE, seq)
    assert seq % tm == 0 and tm % blk == 0
    nt = seq // tm
    col = lambda c: (lambda b, t: (b * nt + t, c))
    per_batch = lambda b, t: (state_layer, b, 0, 0)
    consts = (normw, poolw, pscale)
    return pl.pallas_call(
        functools.partial(_recur_kernel, layer, blk, pos0, C_HEADS),
        grid=(batch, nt),
        in_specs=[pl.BlockSpec((tm, width), col(col0 + j)) for j in range(5)]
        + [pl.BlockSpec((None, None, POOL_PAD, width), per_batch),
           pl.BlockSpec((None, None, width, width), per_batch),
           pl.BlockSpec(lbraw.shape, lambda b, t: (0, 0), pipeline_mode=pl.Buffered(1))]
        + [_layer_spec(c, layer) for c in consts],
        out_specs=[pl.BlockSpec((tm, 2 * width), lambda b, t: (b * nt + t, 0)),
                   pl.BlockSpec((None, width, width), lambda b, t: (b, 0, 0))],
        out_shape=[jax.ShapeDtypeStruct((batch * seq, 2 * width), BF16),
                   jax.ShapeDtypeStruct((batch, width, width), F32)],
        scratch_shapes=_recur_scratch(tm, tm, width),
        compiler_params=pltpu.CompilerParams(dimension_semantics=("parallel", "arbitrary"),
                                             vmem_limit_bytes=V7X_VMEM_LIMIT),
        name="pool_hgrn_mixer",
    )(z, z, z, z, z, hist0, st0, lbraw, *consts)


def _post_recur_call(x, oa, p, z, hist0, st0, lbraw, normw, poolw, pscale, wout, wgu, wdn, wg, wp, lng, lnb,
                     layer, batch, seq, blk, col0, alpha):
    n, d = x.shape
    width = st0.shape[-1]
    tm = _token_tile(seq)
    rtm = min(RECUR_TILE, tm)
    assert n == batch * seq and tm % rtm == 0 and rtm % blk == 0
    n_tiles = n // tm
    tiles_per_seq = seq // tm
    behind = lambda j: jnp.maximum(j - 1, 0)
    ahead = lambda j: jnp.minimum(j, n_tiles - 1)
    seq_of = lambda j: ahead(j) // tiles_per_seq
    mixer_consts = (normw, poolw, pscale)
    token_consts = (wout, wgu, wdn, wg, wp, lng, lnb)
    return pl.pallas_call(
        functools.partial(_post_recur_kernel, alpha, 2, layer, blk, C_HEADS, rtm, tiles_per_seq, n_tiles),
        grid=(n_tiles + 1,),
        in_specs=[pl.BlockSpec((tm, d), lambda j: (behind(j), 0)),
                  pl.BlockSpec((tm, oa.shape[1]), lambda j: (behind(j), 0)),
                  pl.BlockSpec((None, tm, p.shape[-1]), lambda j: (layer, behind(j), 0))]
        + [pl.BlockSpec((tm, width), (lambda c: (lambda j: (ahead(j), c)))(col0 + c)) for c in range(5)]
        + [pl.BlockSpec((None, None, POOL_PAD, width), lambda j: (0, seq_of(j), 0, 0)),
           pl.BlockSpec((None, None, width, width), lambda j: (0, seq_of(j), 0, 0)),
           pl.BlockSpec(lbraw.shape, lambda j: (0, 0), pipeline_mode=pl.Buffered(1))]
        + [_layer_spec(c, layer) for c in mixer_consts + token_consts],
        out_specs=[pl.BlockSpec((tm, d), lambda j: (behind(j), 0)),
                   pl.BlockSpec((None, width, width), lambda j: (seq_of(j), 0, 0))],
        out_shape=[jax.ShapeDtypeStruct((n, d), F32), jax.ShapeDtypeStruct((batch, width, width), F32)],
        scratch_shapes=[pltpu.VMEM((2, tm, 2 * width), BF16)] + _recur_scratch(tm, rtm, width),
        compiler_params=pltpu.CompilerParams(dimension_semantics=("arbitrary",),
                                             vmem_limit_bytes=V7X_VMEM_LIMIT),
        name="mixer_and_post_tokens",
    )(x, oa, p, z, z, z, z, z, hist0, st0, lbraw, *mixer_consts, *token_consts)


def _block_diag(blocks):
    g = blocks.shape[-3]
    zero = jnp.zeros_like(blocks[..., 0, :, :])
    rows = [jnp.concatenate([blocks[..., h, :, :] if j == h else zero for j in range(g)], axis=-1)
            for h in range(g)]
    return jnp.concatenate(rows, axis=-2)


def _state_from_kernel(st, n_heads):
    b, w, _ = st.shape
    hd = w // n_heads
    blocks = jnp.stack([st[:, h * hd:(h + 1) * hd, h * hd:(h + 1) * hd] for h in range(n_heads)], axis=1)
    return jnp.swapaxes(blocks, -1, -2)


def kernel(x_prompt, x_sample, p_prompt, p_sample, cache_attn_k, cache_attn_v, state_pool, state_hgrn, ffn1_w_gu, ffn1_w_down, w_in, attn_rel_bias, pool_w, pool_scale, hgrn_lower_bounds, hgrn_norm_w, w_out, ffn2_w_gu, ffn2_w_down, ple_w_gate, ple_w_proj, ln_g, ln_b):
    depth = w_in.shape[0]
    alpha = float((2 * depth) ** 0.25)
    nb, seq, d = x_prompt.shape
    db, dt, _ = x_sample.shape
    a_width = A_HEADS * HEAD_DIM
    c_width = C_HEADS * HEAD_DIM
    in_width = w_in.shape[-1]
    col0 = 3 * a_width // c_width
    past_rows = min(A_BACK * CHUNK, seq)
    assert dt >= POOL_HIST and seq >= POOL_HIST and PAST_LEN >= POOL_HIST

    bf = lambda w: w.astype(BF16)
    w1gu, w1dn, win, wout = bf(ffn1_w_gu), bf(ffn1_w_down), bf(w_in), bf(w_out)
    w2gu, w2dn, wg, wp = bf(ffn2_w_gu), bf(ffn2_w_down), bf(ple_w_gate), bf(ple_w_proj)
    poolw = bf(_block_diag(pool_w))
    pscale = pool_scale.astype(F32)[:, None, :]
    normw = jnp.tile(hgrn_norm_w.astype(F32), (1, C_HEADS))[:, None, :]
    lbraw = hgrn_lower_bounds.astype(F32)
    pp = p_prompt.reshape(depth, nb * seq, -1)
    ps = p_sample.reshape(depth, db * dt, -1)

    zero_hist = jnp.zeros((1, nb, POOL_PAD, c_width), F32)
    zero_state = jnp.zeros((1, nb, c_width, c_width), F32)
    hist_s = jnp.pad(state_pool.astype(F32), ((0, 0), (0, 0), (POOL_PAD - POOL_HIST, 0), (0, 0)))
    state_s = _block_diag(jnp.swapaxes(state_hgrn.astype(F32), -1, -2))

    lc = cache_attn_k.shape[2]
    cache_k = cache_attn_k.reshape(depth, db, lc, a_width)
    cache_v = cache_attn_v.reshape(depth, db, lc, a_width)

    xp = x_prompt.reshape(nb * seq, d)
    xs = x_sample.reshape(db * dt, d)
    outs = [[] for _ in range(8)]

    for i in range(depth):
        rel_table = attn_rel_bias[i].astype(F32)

        xp, zp = _pre_call(xp, w1gu, w1dn, ln_g, ln_b, win, i, alpha)
        oa = _attn_prompt_call(zp, rel_table, nb, seq, a_width)
        xp, st = _post_recur_call(xp, oa, pp, zp, zero_hist, zero_state, lbraw, normw, poolw, pscale,
                                  wout, w2gu, w2dn, wg, wp, ln_g, ln_b, i, nb, seq, CHUNK, col0, alpha)
        zp3 = zp.reshape(nb, seq, in_width)
        outs[0].append(zp3[:, seq - past_rows:, a_width:2 * a_width].reshape(nb, past_rows, A_HEADS, HEAD_DIM))
        outs[1].append(zp3[:, seq - past_rows:, 2 * a_width:3 * a_width].reshape(nb, past_rows, A_HEADS, HEAD_DIM))
        outs[2].append(zp3[:, seq - POOL_HIST:, 3 * a_width:3 * a_width + c_width])
        outs[3].append(_state_from_kernel(st, C_HEADS))

        xs, zs = _pre_call(xs, w1gu, w1dn, ln_g, ln_b, win, i, alpha)
        oa = _attn_sample_call(zs, cache_k, cache_v, rel_table, i, db, dt, a_width)
        obc, st = _recur_call(zs, hist_s, state_s, lbraw, normw, poolw, pscale, i, i, db, dt, dt, PAST_LEN, col0)
        xs = _post_call(xs, oa, obc, ps, wout, w2gu, w2dn, wg, wp, ln_g, ln_b, i, alpha)
        zs3 = zs.reshape(db, dt, in_width)
        outs[4].append(zs3[:, :, a_width:2 * a_width].reshape(db, dt, A_HEADS, HEAD_DIM))
        outs[5].append(zs3[:, :, 2 * a_width:3 * a_width].reshape(db, dt, A_HEADS, HEAD_DIM))
        outs[6].append(zs3[:, dt - POOL_HIST:, 3 * a_width:3 * a_width + c_width])
        outs[7].append(_state_from_kernel(st, C_HEADS))

    stacked = [jnp.stack(o, axis=0) for o in outs]
    return (xp.reshape(nb, seq, d), xs.reshape(db, dt, d), *stacked)
```

```python
import collections
import functools

import jax
import jax.numpy as jnp
from jax import lax
from jax.experimental import pallas as pl
from jax.experimental.pallas import tpu as pltpu

F32 = jnp.float32
BF16 = jnp.bfloat16

PAST_LEN = 2048
CHUNK = 64
A_BACK = 8
HEAD_DIM = 64
A_HEADS = 8
C_HEADS = 4
REL_MAX = 128
POOL_WINDOWS = (2, 4, 8, 16)
POOL_HIST = max(POOL_WINDOWS) - 1
POOL_PAD = POOL_HIST + 1
SUB_BLOCK = 32
DECAY_SAFE_LOG = -60.0
TOKEN_SUB_ROWS = 256
LN_EPS = 1e-5
RMS_EPS = 1e-6
NEG_INF = -1e30
LOG2E = 1.4426950408889634

V7X_LANES = 128
V7X_MXU_WIDTH = 256
V7X_VMEM_LIMIT = 56 * 1024 * 1024


def _dot(a, b):
    return jnp.dot(a, b, preferred_element_type=F32)


def _dot_nt(a, b):
    return lax.dot_general(a, b, (((1,), (1,)), ((), ())), preferred_element_type=F32)


def _dot_tn(a, b):
    return lax.dot_general(a, b, (((0,), (0,)), ((), ())), preferred_element_type=F32)


def _layer_norm(y, g, b):
    mu = jnp.mean(y, axis=-1, keepdims=True)
    d = y - mu
    var = jnp.mean(d * d, axis=-1, keepdims=True)
    return d * lax.rsqrt(var + LN_EPS) * g + b


def _sigmoid(x):
    return 1.0 / (1.0 + jnp.exp(-x))


def _log2(n):
    assert n > 0 and n & (n - 1) == 0, n
    return n.bit_length() - 1


def _div(x, n):
    return x >> _log2(n)


def _swiglu_stages(xb, wgu_ref, wdn_ref, n_split):
    hidden = wdn_ref.shape[0]
    groups = _hidden_groups(hidden, n_split)
    gate_up = []
    for lo, hi in groups:
        gate_up.append((_dot(xb, wgu_ref[:, lo:hi]), _dot(xb, wgu_ref[:, hidden + lo:hidden + hi])))
        yield
    acc = None
    for (lo, hi), (gate, up) in zip(groups, gate_up):
        act = (gate * _sigmoid(gate) * up).astype(BF16)
        part = _dot(act, wdn_ref[lo:hi, :])
        acc = part if acc is None else acc + part
        yield
    return acc


def _hidden_groups(hidden, n_split):
    if hidden % V7X_MXU_WIDTH:
        return [(0, hidden)]
    tiles = hidden // V7X_MXU_WIDTH
    bounds = [V7X_MXU_WIDTH * ((tiles * g + n_split - 1) // n_split) for g in range(n_split + 1)]
    return [(lo, hi) for lo, hi in zip(bounds[:-1], bounds[1:]) if hi > lo]


def _run_interleaved(chains):
    live = list(chains)
    while live:
        for ch in list(live):
            if ch not in live:
                continue
            try:
                next(ch)
            except StopIteration:
                live = [c for c in live if c is not ch]


def _sub_tiles(n_rows):
    sub = min(TOKEN_SUB_ROWS, n_rows)
    return [slice(r, r + sub) for r in range(0, n_rows, sub)]


def _pre_kernel(alpha, n_split, x_ref, wgu_ref, wdn_ref, lng_ref, lnb_ref, win_ref, xo_ref, z_ref):
    def chain(rows):
        x = x_ref[rows, :]
        ffn = yield from _swiglu_stages(x.astype(BF16), wgu_ref, wdn_ref, n_split)
        xn = _layer_norm(alpha * x + 0.5 * ffn, lng_ref[0:1, :], lnb_ref[0:1, :])
        xo_ref[rows, :] = xn
        z_ref[rows, :] = _dot(xn.astype(BF16), win_ref[...])
        yield

    _run_interleaved([chain(rows) for rows in _sub_tiles(x_ref.shape[0])])


def _post_chain(alpha, n_split, rows, x_ref, oa_ref, obc_ref, p_ref, wout_ref, wgu_ref, wdn_ref, wg_ref, wp_ref,
                lng_ref, lnb_ref, xo_ref):
    wa = oa_ref.shape[1]
    mixed = _dot(oa_ref[rows, :], wout_ref[0:wa, :]) + _dot(obc_ref[rows, :], wout_ref[wa:, :])
    proj = _dot(p_ref[rows, :].astype(BF16), wp_ref[...])
    yield
    x = _layer_norm(alpha * x_ref[rows, :] + mixed, lng_ref[1:2, :], lnb_ref[1:2, :])
    ffn = yield from _swiglu_stages(x.astype(BF16), wgu_ref, wdn_ref, n_split)
    x = _layer_norm(alpha * x + 0.5 * ffn, lng_ref[2:3, :], lnb_ref[2:3, :])
    emb = _sigmoid(_dot(x.astype(BF16), wg_ref[...])) * proj
    yield
    xo_ref[rows, :] = _layer_norm(alpha * x + emb, lng_ref[3:4, :], lnb_ref[3:4, :])


def _post_kernel(alpha, n_split, x_ref, oa_ref, obc_ref, p_ref, wout_ref, wgu_ref, wdn_ref, wg_ref, wp_ref,
                 lng_ref, lnb_ref, xo_ref):
    _run_interleaved([_post_chain(alpha, n_split, rows, x_ref, oa_ref, obc_ref, p_ref, wout_ref, wgu_ref, wdn_ref,
                                  wg_ref, wp_ref, lng_ref, lnb_ref, xo_ref) for rows in _sub_tiles(x_ref.shape[0])])


def _layer_spec(stacked, layer):
    rest = stacked.shape[1:]
    return pl.BlockSpec((None,) + rest, lambda *_: (layer,) + (0,) * len(rest), pipeline_mode=pl.Buffered(1))


def _token_tile(n):
    for tm in (2 * TOKEN_SUB_ROWS, TOKEN_SUB_ROWS, 128, 64, 32, 16, 8):
        if n % tm == 0:
            return tm
    raise ValueError(f"unsupported token count {n}")


def _pre_call(x, wgu, wdn, lng, lnb, win, layer, alpha):
    n, d = x.shape
    zw = win.shape[-1]
    tm = _token_tile(n)
    row = lambda i: (i, 0)
    consts = (wgu, wdn, lng, lnb, win)
    return pl.pallas_call(
        functools.partial(_pre_kernel, alpha, 2),
        grid=(n // tm,),
        in_specs=[pl.BlockSpec((tm, d), row)] + [_layer_spec(c, layer) for c in consts],
        out_specs=[pl.BlockSpec((tm, d), row), pl.BlockSpec((tm, zw), row)],
        out_shape=[jax.ShapeDtypeStruct((n, d), F32), jax.ShapeDtypeStruct((n, zw), F32)],
        compiler_params=pltpu.CompilerParams(dimension_semantics=("parallel",),
                                             vmem_limit_bytes=V7X_VMEM_LIMIT),
        name="pre_mixer_tokens",
    )(x, *consts)


def _post_call(x, oa, obc, p, wout, wgu, wdn, wg, wp, lng, lnb, layer, alpha):
    n, d = x.shape
    tm = _token_tile(n)
    row = lambda i: (i, 0)
    consts = (wout, wgu, wdn, wg, wp, lng, lnb)
    return pl.pallas_call(
        functools.partial(_post_kernel, alpha, 2),
        grid=(n // tm,),
        in_specs=[pl.BlockSpec((tm, d), row), pl.BlockSpec((tm, oa.shape[1]), row),
                  pl.BlockSpec((tm, obc.shape[1]), row),
                  pl.BlockSpec((None, tm, p.shape[-1]), lambda i: (layer, i, 0))]
        + [_layer_spec(c, layer) for c in consts],
        out_specs=pl.BlockSpec((tm, d), row),
        out_shape=jax.ShapeDtypeStruct((n, d), F32),
        compiler_params=pltpu.CompilerParams(dimension_semantics=("parallel",),
                                             vmem_limit_bytes=V7X_VMEM_LIMIT),
        name="post_mixer_tokens",
    )(x, oa, obc, p, *consts)


def _qk_scores(q_ref, kcat, s_out, r0, n_q, k0, n_k):
    scale = HEAD_DIM ** -0.5 * LOG2E
    low = lax.broadcasted_iota(jnp.int32, (1, V7X_LANES), 1) < HEAD_DIM
    for hp in range(q_ref.shape[1] // V7X_LANES):
        ls = slice(hp * V7X_LANES, (hp + 1) * V7X_LANES)
        q = q_ref[r0:r0 + n_q, ls] * scale
        lhs = jnp.concatenate([jnp.where(low, q, 0.0), jnp.where(low, 0.0, q)], axis=0).astype(BF16)
        s_out[2 * hp * n_q:2 * (hp + 1) * n_q, :] = _dot_nt(lhs, kcat[k0:k0 + n_k, ls])


def _softmax_weights(s_in, bias_ref, key_valid, p_out):
    s = s_in[...] + bias_ref[...]
    if key_valid is not None:
        s = jnp.where(key_valid, s, NEG_INF)
    p_out[...] = jnp.exp2(s - jnp.max(s, axis=-1, keepdims=True)).astype(BF16)


def _weighted_values(p_in, vcat, o_ref, r0, n_q, k0, n_k):
    low = lax.broadcasted_iota(jnp.int32, (1, V7X_LANES), 1) < HEAD_DIM
    ones = jnp.ones((n_k, V7X_LANES), BF16)
    for hp in range(o_ref.shape[1] // V7X_LANES):
        ls = slice(hp * V7X_LANES, (hp + 1) * V7X_LANES)
        rows = slice(2 * hp * n_q, 2 * (hp + 1) * n_q)
        pv = _dot(p_in[rows, :], jnp.concatenate([vcat[k0:k0 + n_k, ls], ones], axis=1))
        o2 = pv[:, :V7X_LANES] * (1.0 / pv[:, V7X_LANES:])
        o_ref[r0:r0 + n_q, ls] = jnp.where(low, o2[:n_q], o2[n_q:]).astype(o_ref.dtype)


def _build_rel_bias(table_ref, bias_ref, n_q, n_k, offset):
    n_heads = table_ref.shape[0]
    clip = lambda d: min(max(d, -(CHUNK - 1)), REL_MAX) + (CHUNK - 1)
    heads_per_pass = 4
    for j0 in range(0, n_k, V7X_LANES):
        jw = min(V7X_LANES, n_k - j0)
        r_lo, r_hi = clip(offset - (j0 + jw - 1)), clip(offset + n_q - 1 - j0)
        d = (offset - j0 + lax.broadcasted_iota(jnp.int32, (n_q, jw), 0)
             - lax.broadcasted_iota(jnp.int32, (n_q, jw), 1))
        idx = jnp.clip(d, -(CHUNK - 1), REL_MAX) + (CHUNK - 1)
        for h0 in range(0, n_heads, heads_per_pass):
            heads = range(h0, min(h0 + heads_per_pass, n_heads))
            if r_lo == r_hi:
                vals = [jnp.full((n_q, jw), table_ref[h, r_lo], F32) for h in heads]
            else:
                def pick(r, acc, heads=heads, idx=idx):
                    hit = idx == r
                    return tuple(jnp.where(hit, table_ref[h, r], a) for h, a in zip(heads, acc))
                vals = lax.fori_loop(r_lo, r_hi + 1, pick, tuple(jnp.zeros((n_q, jw), F32) for _ in heads))
            for h, v in zip(heads, vals):
                bias_ref[h * n_q:(h + 1) * n_q, j0:j0 + jw] = v * LOG2E


def _attn_prompt_kernel(qb_rows, table_ref, q_ref, kp_ref, kc_ref, vp_ref, vc_ref, o_ref,
                        kcat, vcat, bias_ref, s_buf, p_buf):
    i = pl.program_id(1)
    band = (A_BACK + 1) * CHUNK
    past = A_BACK * CHUNK
    n_chunks = qb_rows // CHUNK

    @pl.when((pl.program_id(0) == 0) & (i == 0))
    def _():
        _build_rel_bias(table_ref, bias_ref, CHUNK, band, past)

    def cast_rows(j, carry):
        r = pl.multiple_of(j * CHUNK, CHUNK)
        kcat[pl.ds(r, CHUNK), :] = kp_ref[pl.ds(r, CHUNK), :].astype(BF16)
        kcat[pl.ds(qb_rows + r, CHUNK), :] = kc_ref[pl.ds(r, CHUNK), :].astype(BF16)
        vcat[pl.ds(r, CHUNK), :] = vp_ref[pl.ds(r, CHUNK), :].astype(BF16)
        vcat[pl.ds(qb_rows + r, CHUNK), :] = vc_ref[pl.ds(r, CHUNK), :].astype(BF16)
        return carry

    lax.fori_loop(0, n_chunks, cast_rows, 0)
    key_idx = lax.broadcasted_iota(jnp.int32, (1, band), 1)

    _qk_scores(q_ref, kcat, s_buf.at[0], 0, CHUNK, 0, band)
    for c in range(n_chunks + 1):
        if c + 1 < n_chunks:
            _qk_scores(q_ref, kcat, s_buf.at[(c + 1) % 2], (c + 1) * CHUNK, CHUNK, (c + 1) * CHUNK, band)
        if c < n_chunks:
            key_valid = key_idx >= past - c * CHUNK - i * qb_rows
            _softmax_weights(s_buf.at[c % 2], bias_ref, key_valid, p_buf.at[c % 2])
        if c >= 1:
            _weighted_values(p_buf.at[(c - 1) % 2], vcat, o_ref,
                             (c - 1) * CHUNK, CHUNK, (c - 1) * CHUNK, band)


def _attn_prompt_call(z, table, batch, seq, a_width):
    qb_rows = A_BACK * CHUNK
    assert seq % qb_rows == 0 and a_width % V7X_LANES == 0
    nq = seq // qb_rows
    band = (A_BACK + 1) * CHUNK
    rows = A_HEADS * CHUNK
    cur = lambda col: (lambda b, i: (b * nq + i, col))
    prev = lambda col: (lambda b, i: (b * nq + jnp.maximum(i - 1, 0), col))
    blk = (qb_rows, a_width)
    return pl.pallas_call(
        functools.partial(_attn_prompt_kernel, qb_rows),
        grid=(batch, nq),
        in_specs=[pl.BlockSpec(memory_space=pltpu.SMEM),
                  pl.BlockSpec(blk, cur(0)), pl.BlockSpec(blk, prev(1)), pl.BlockSpec(blk, cur(1)),
                  pl.BlockSpec(blk, prev(2)), pl.BlockSpec(blk, cur(2))],
        out_specs=pl.BlockSpec(blk, lambda b, i: (b * nq + i, 0)),
        out_shape=jax.ShapeDtypeStruct((batch * seq, a_width), BF16),
        scratch_shapes=[pltpu.VMEM((2 * qb_rows, a_width), BF16), pltpu.VMEM((2 * qb_rows, a_width), BF16),
                        pltpu.VMEM((rows, band), F32), pltpu.VMEM((2, rows, band), F32),
                        pltpu.VMEM((2, rows, band), BF16)],
        compiler_params=pltpu.CompilerParams(dimension_semantics=("arbitrary", "arbitrary"),
                                             vmem_limit_bytes=V7X_VMEM_LIMIT),
        name="band_attention_prompt",
    )(table, z, z, z, z, z)


def _attn_sample_kernel(table_ref, q_ref, k_ref, v_ref, ck_ref, cv_ref, o_ref, kcat, vcat, bias_ref,
                        s_buf, p_buf):
    lc = ck_ref.shape[0]
    t = q_ref.shape[0]

    @pl.when(pl.program_id(0) == 0)
    def _():
        _build_rel_bias(table_ref, bias_ref, t, lc + t, lc)

    kcat[0:lc, :] = ck_ref[...].astype(BF16)
    kcat[lc:, :] = k_ref[...].astype(BF16)
    vcat[0:lc, :] = cv_ref[...].astype(BF16)
    vcat[lc:, :] = v_ref[...].astype(BF16)
    _qk_scores(q_ref, kcat, s_buf, 0, t, 0, lc + t)
    _softmax_weights(s_buf, bias_ref, None, p_buf)
    _weighted_values(p_buf, vcat, o_ref, 0, t, 0, lc + t)


def _attn_sample_call(z, cache_k, cache_v, table, layer, batch, t, a_width):
    lc = cache_k.shape[2]
    rows = A_HEADS * t
    blk = (t, a_width)
    cblk = (None, None, lc, a_width)
    cidx = lambda b: (layer, b, 0, 0)
    return pl.pallas_call(
        _attn_sample_kernel,
        grid=(batch,),
        in_specs=[pl.BlockSpec(memory_space=pltpu.SMEM),
                  pl.BlockSpec(blk, lambda b: (b, 0)), pl.BlockSpec(blk, lambda b: (b, 1)),
                  pl.BlockSpec(blk, lambda b: (b, 2)),
                  pl.BlockSpec(cblk, cidx), pl.BlockSpec(cblk, cidx)],
        out_specs=pl.BlockSpec(blk, lambda b: (b, 0)),
        out_shape=jax.ShapeDtypeStruct((batch * t, a_width), BF16),
        scratch_shapes=[pltpu.VMEM((lc + t, a_width), BF16), pltpu.VMEM((lc + t, a_width), BF16),
                        pltpu.VMEM((rows, lc + t), F32), pltpu.VMEM((rows, lc + t), F32),
                        pltpu.VMEM((rows, lc + t), BF16)],
        compiler_params=pltpu.CompilerParams(dimension_semantics=("arbitrary",),
                                             vmem_limit_bytes=V7X_VMEM_LIMIT),
        name="band_attention_sample",
    )(table, z, z, z, cache_k, cache_v)


def _split2(x):
    a = x.astype(BF16)
    return a, (x - a.astype(F32)).astype(BF16)


_RecurRefs = collections.namedtuple(
    "_RecurRefs", "u q f i g lbraw normw poolw pscale out state hist dmask inter rows_b rows_k")


def _recur_reset(R, hist0_ref, st0_ref, blk, tm):
    sb = min(SUB_BLOCK, blk)
    R.state[...] = st0_ref[...]
    R.hist[...] = hist0_ref[...]
    ri = lax.broadcasted_iota(jnp.int32, (tm, tm), 0)
    ci = lax.broadcasted_iota(jnp.int32, (tm, tm), 1)
    same_sb = _div(ri, sb) == _div(ci, sb)
    R.dmask[0] = jnp.where(same_sb & (ci <= ri), 1.0, 0.0).astype(BF16)
    R.dmask[1] = jnp.where(same_sb, 1.0, 0.0).astype(BF16)


def _recur_tile(layer, blk, pos0, n_heads, R, rows, t_tile, fallbacks):
    tm = rows.stop - rows.start
    width = R.u.shape[1]
    hd = width // n_heads
    sb = min(SUB_BLOCK, blk)
    assert blk // sb in (1, 2) and tm % blk == 0

    raw = R.lbraw[...]
    sm = jnp.exp(raw - jnp.max(raw, axis=0, keepdims=True))
    sm = sm / jnp.sum(sm, axis=0, keepdims=True)
    lb = jnp.zeros((1, width), F32)
    for j in range(1, layer + 1):
        lb = lb + sm[j:j + 1, :]
    forget = lb + (1.0 - lb) * _sigmoid(R.f[rows, :])
    log_f = jnp.log(forget)
    k_in = 1.0 - forget
    qx = R.q[rows, :]
    qf = qx * _sigmoid(qx)
    vb = R.i[rows, :].astype(BF16)

    lf2 = jnp.concatenate(_split2(log_f), axis=1)

    def decay_sum(mask01):
        r = _dot(mask01, lf2)
        return r[:, 0:width] + r[:, width:]

    b_rel = decay_sum(R.dmask[0])
    sb_tot = decay_sum(R.dmask[1])
    yield
    if blk > sb:
        second = (lax.broadcasted_iota(jnp.int32, (tm, 1), 0) & (blk - 1)) >= sb
        prev_tot = pltpu.roll(sb_tot, sb, 0)
        next_tot = pltpu.roll(sb_tot, tm - sb, 0)
        b = b_rel + jnp.where(second, prev_tot, 0.0)
        blk_tot = sb_tot + jnp.where(second, prev_tot, next_tot)
    else:
        b, blk_tot = b_rel, sb_tot

    ext = jnp.concatenate([R.hist[...], R.u[rows, :]], axis=0)
    R.hist[...] = ext[tm:, :]
    s2 = ext + pltpu.roll(ext, 1, 0)
    s4 = s2 + pltpu.roll(s2, 2, 0)
    s8 = s4 + pltpu.roll(s4, 4, 0)
    s16 = s8 + pltpu.roll(s8, 8, 0)
    lane = lax.broadcasted_iota(jnp.int32, (1, width), 1)
    grp = _div(lane, width // len(POOL_WINDOWS))
    wsum = jnp.where(grp == 0, s2, jnp.where(grp == 1, s4, jnp.where(grp == 2, s8, s16)))
    wlen = jnp.where(grp == 0, 2.0, jnp.where(grp == 1, 4.0, jnp.where(grp == 2, 8.0, 16.0)))
    row = lax.broadcasted_iota(jnp.int32, (POOL_PAD + tm, 1), 0)
    pos = (row + (pos0 - POOL_PAD) + t_tile * tm).astype(F32)
    cnt = jnp.maximum(jnp.minimum(pos + 1.0, wlen), 1.0)
    dev = (wsum / cnt - ext)[POOL_PAD:, :]
    o_pool = _dot(dev.astype(BF16), R.poolw[...]) * R.pscale[...]
    R.out[rows, 0:width] = o_pool.astype(R.out.dtype)

    q_rel = qf * jnp.exp(b_rel)
    k_rel = (k_in * jnp.exp(-b_rel)).astype(BF16)
    k_end = (k_in * jnp.exp(sb_tot - b_rel)).astype(BF16)
    q_abs = (qf * jnp.exp(b)).astype(BF16)
    k_tail = (k_in * jnp.exp(blk_tot - b)).astype(BF16)

    hb = n_heads * blk
    n_chunks = tm // blk
    stack_head = _div(lax.broadcasted_iota(jnp.int32, (hb, 1), 0), blk) == _div(lane, hd)
    tq = lax.broadcasted_iota(jnp.int32, (hb, blk), 0) & (blk - 1)
    ts = lax.broadcasted_iota(jnp.int32, (hb, blk), 1)
    m_intra = (_div(tq, sb) == _div(ts, sb)) & (ts <= tq)
    m_cross = _div(ts, sb) < _div(tq, sb)
    bd = (_div(lax.broadcasted_iota(jnp.int32, (width, width), 0), hd)
          == _div(lax.broadcasted_iota(jnp.int32, (width, width), 1), hd))
    ones_bd = jnp.where(bd, 1.0, 0.0).astype(BF16)
    gx = R.g[rows, :]
    out_gate = R.normw[...] * (gx * _sigmoid(gx))

    def write_output(o):
        sq_hi, sq_lo = _split2(o * o)
        ms = (_dot(sq_hi, ones_bd) + _dot(sq_lo, ones_bd)) * (1.0 / hd)
        R.out[rows, width:] = (o * lax.rsqrt(ms + RMS_EPS) * out_gate).astype(R.out.dtype)

    chunks = [slice(c * blk, (c + 1) * blk) for c in range(n_chunks)]
    state_in = [jnp.where(bd, _dot_tn(vb[rs], k_tail[rs]), 0.0) for rs in chunks]
    yield
    two_sets = blk > sb
    if two_sets:
        tq2 = lax.broadcasted_iota(jnp.int32, (hb, 2 * blk), 0) & (blk - 1)
        col = lax.broadcasted_iota(jnp.int32, (hb, 2 * blk), 1)
        ts2 = col & (blk - 1)
        keep = (((col < blk) & (_div(tq2, sb) == _div(ts2, sb)) & (ts2 <= tq2))
                | ((col >= blk) & (_div(ts2, sb) < _div(tq2, sb))))
    else:
        keep = m_intra
    scores = []
    for rs in chunks:
        lhs = jnp.where(stack_head, jnp.concatenate([q_rel[rs]] * n_heads, axis=0), 0.0).astype(BF16)
        keys = jnp.concatenate([k_rel[rs], k_end[rs]], axis=0) if two_sets else k_rel[rs]
        scores.append(jnp.where(keep, _dot_nt(lhs, keys), 0.0).astype(BF16))
    yield
    outs = []
    st = R.state[...]
    for c, rs in enumerate(chunks):
        from_state = _dot_nt(q_abs[rs], st.astype(BF16))
        values = jnp.concatenate([vb[rs], vb[rs]], axis=0) if two_sets else vb[rs]
        stacked = jnp.where(stack_head, _dot(scores[c], values), 0.0)
        o_c = stacked[0:blk]
        for h in range(1, n_heads):
            o_c = o_c + stacked[h * blk:(h + 1) * blk]
        R.inter[pl.ds(rows.start + c * blk, blk), :] = from_state
        outs.append(o_c + from_state)
        decay_end = jnp.exp(blk_tot[c * blk:c * blk + 1, :])
        st = st * decay_end + state_in[c]
        yield
    R.state[...] = st
    write_output(outs[0] if len(outs) == 1 else jnp.concatenate(outs, axis=0))

    def fallback():
        @pl.when(jnp.min(sb_tot) <= DECAY_SAFE_LOG)
        def _():
            R.rows_b[...] = b
            R.rows_k[...] = k_in
            t_row = lax.broadcasted_iota(jnp.int32, (blk, 1), 0)

            def add_key_row(s, acc):
                parts = []
                for c, rs in enumerate(chunks):
                    r = c * blk + s
                    rel = jnp.minimum(b[rs] - R.rows_b[pl.ds(r, 1), :], 0.0)
                    term = jnp.where(t_row >= s, qf[rs] * jnp.exp(rel) * R.rows_k[pl.ds(r, 1), :], 0.0)
                    parts.append(_dot(term.astype(BF16), ones_bd) * R.i[pl.ds(rows.start + r, 1), :])
                return acc + (parts[0] if n_chunks == 1 else jnp.concatenate(parts, axis=0))

            write_output(lax.fori_loop(0, blk, add_key_row, R.inter[rows, :]))

    fallbacks.append(fallback)
    yield


def _recur_kernel(layer, blk, pos0, n_heads, u_ref, q_ref, f_ref, i_ref, g_ref, hist0_ref, st0_ref, lbraw_ref,
                  normw_ref, poolw_ref, pscale_ref, o_ref, st_ref, state, hist, dmask, inter, rows_b, rows_k):
    t_idx = pl.program_id(1)
    tm = u_ref.shape[0]
    R = _RecurRefs(u_ref, q_ref, f_ref, i_ref, g_ref, lbraw_ref, normw_ref, poolw_ref, pscale_ref, o_ref,
                   state, hist, dmask, inter, rows_b, rows_k)

    @pl.when(t_idx == 0)
    def _():
        _recur_reset(R, hist0_ref, st0_ref, blk, tm)

    fallbacks = []
    _run_interleaved([_recur_tile(layer, blk, pos0, n_heads, R, slice(0, tm), t_idx, fallbacks)])
    for fb in fallbacks:
        fb()

    @pl.when(t_idx == pl.num_programs(1) - 1)
    def _():
        st_ref[...] = state[...]


RECUR_TILE = 256


def _recur_scratch(rows, tm, width):
    return [pltpu.VMEM((width, width), F32), pltpu.VMEM((POOL_PAD, width), F32), pltpu.VMEM((2, tm, tm), BF16),
            pltpu.VMEM((rows, width), F32), pltpu.VMEM((tm, width), F32), pltpu.VMEM((tm, width), F32)]


def _recur_call(z, hist0, st0, lbraw, normw, poolw, pscale, layer, state_layer, batch, seq, blk, pos0, col0):
    width = st0.shape[-1]
    tm = min(RECUR_TILE, seq)
    assert seq % tm == 0 and tm % blk == 0
    nt = seq // tm
    col = lambda c: (lambda b, t: (b * nt + t, c))
    per_batch = lambda b, t: (state_layer, b, 0, 0)
    consts = (normw, poolw, pscale)
    return pl.pallas_call(
        functools.partial(_recur_kernel, layer, blk, pos0, C_HEADS),
        grid=(batch, nt),
        in_specs=[pl.BlockSpec((tm, width), col(col0 + j)) for j in range(5)]
        + [pl.BlockSpec((None, None, POOL_PAD, width), per_batch),
           pl.BlockSpec((None, None, width, width), per_batch),
           pl.BlockSpec(lbraw.shape, lambda b, t: (0, 0), pipeline_mode=pl.Buffered(1))]
        + [_layer_spec(c, layer) for c in consts],
        out_specs=[pl.BlockSpec((tm, 2 * width), lambda b, t: (b * nt + t, 0)),
                   pl.BlockSpec((None, width, width), lambda b, t: (b, 0, 0))],
        out_shape=[jax.ShapeDtypeStruct((batch * seq, 2 * width), BF16),
                   jax.ShapeDtypeStruct((batch, width, width), F32)],
        scratch_shapes=_recur_scratch(tm, tm, width),
        compiler_params=pltpu.CompilerParams(dimension_semantics=("parallel", "arbitrary"),
                                             vmem_limit_bytes=V7X_VMEM_LIMIT),
        name="pool_hgrn_mixer",
    )(z, z, z, z, z, hist0, st0, lbraw, *consts)


def _block_diag(blocks):
    g = blocks.shape[-3]
    zero = jnp.zeros_like(blocks[..., 0, :, :])
    rows = [jnp.concatenate([blocks[..., h, :, :] if j == h else zero for j in range(g)], axis=-1)
            for h in range(g)]
    return jnp.concatenate(rows, axis=-2)


def _state_from_kernel(st, n_heads):
    b, w, _ = st.shape
    hd = w // n_heads
    blocks = jnp.stack([st[:, h * hd:(h + 1) * hd, h * hd:(h + 1) * hd] for h in range(n_heads)], axis=1)
    return jnp.swapaxes(blocks, -1, -2)


def kernel(x_prompt, x_sample, p_prompt, p_sample, cache_attn_k, cache_attn_v, state_pool, state_hgrn, ffn1_w_gu, ffn1_w_down, w_in, attn_rel_bias, pool_w, pool_scale, hgrn_lower_bounds, hgrn_norm_w, w_out, ffn2_w_gu, ffn2_w_down, ple_w_gate, ple_w_proj, ln_g, ln_b):
    depth = w_in.shape[0]
    alpha = float((2 * depth) ** 0.25)
    nb, seq, d = x_prompt.shape
    db, dt, _ = x_sample.shape
    a_width = A_HEADS * HEAD_DIM
    c_width = C_HEADS * HEAD_DIM
    in_width = w_in.shape[-1]
    col0 = 3 * a_width // c_width
    past_rows = min(A_BACK * CHUNK, seq)
    assert dt >= POOL_HIST and seq >= POOL_HIST and PAST_LEN >= POOL_HIST

    bf = lambda w: w.astype(BF16)
    w1gu, w1dn, win, wout = bf(ffn1_w_gu), bf(ffn1_w_down), bf(w_in), bf(w_out)
    w2gu, w2dn, wg, wp = bf(ffn2_w_gu), bf(ffn2_w_down), bf(ple_w_gate), bf(ple_w_proj)
    poolw = bf(_block_diag(pool_w))
    pscale = pool_scale.astype(F32)[:, None, :]
    normw = jnp.tile(hgrn_norm_w.astype(F32), (1, C_HEADS))[:, None, :]
    lbraw = hgrn_lower_bounds.astype(F32)
    pp = p_prompt.reshape(depth, nb * seq, -1)
    ps = p_sample.reshape(depth, db * dt, -1)

    zero_hist = jnp.zeros((1, nb, POOL_PAD, c_width), F32)
    zero_state = jnp.zeros((1, nb, c_width, c_width), F32)
    hist_s = jnp.pad(state_pool.astype(F32), ((0, 0), (0, 0), (POOL_PAD - POOL_HIST, 0), (0, 0)))
    state_s = _block_diag(jnp.swapaxes(state_hgrn.astype(F32), -1, -2))

    lc = cache_attn_k.shape[2]
    cache_k = cache_attn_k.reshape(depth, db, lc, a_width)
    cache_v = cache_attn_v.reshape(depth, db, lc, a_width)

    xp = x_prompt.reshape(nb * seq, d)
    xs = x_sample.reshape(db * dt, d)
    outs = [[] for _ in range(8)]

    for i in range(depth):
        rel_table = attn_rel_bias[i].astype(F32)

        xp, zp = _pre_call(xp, w1gu, w1dn, ln_g, ln_b, win, i, alpha)
        oa = _attn_prompt_call(zp, rel_table, nb, seq, a_width)
        obc, st = _recur_call(zp, zero_hist, zero_state, lbraw, normw, poolw, pscale, i, 0, nb, seq, CHUNK, 0, col0)
        xp = _post_call(xp, oa, obc, pp, wout, w2gu, w2dn, wg, wp, ln_g, ln_b, i, alpha)
        zp3 = zp.reshape(nb, seq, in_width)
        outs[0].append(zp3[:, seq - past_rows:, a_width:2 * a_width].reshape(nb, past_rows, A_HEADS, HEAD_DIM))
        outs[1].append(zp3[:, seq - past_rows:, 2 * a_width:3 * a_width].reshape(nb, past_rows, A_HEADS, HEAD_DIM))
        outs[2].append(zp3[:, seq - POOL_HIST:, 3 * a_width:3 * a_width + c_width])
        outs[3].append(_state_from_kernel(st, C_HEADS))

        xs, zs = _pre_call(xs, w1gu, w1dn, ln_g, ln_b, win, i, alpha)
        oa = _attn_sample_call(zs, cache_k, cache_v, rel_table, i, db, dt, a_width)
        obc, st = _recur_call(zs, hist_s, state_s, lbraw, normw, poolw, pscale, i, i, db, dt, dt, PAST_LEN, col0)
        xs = _post_call(xs, oa, obc, ps, wout, w2gu, w2dn, wg, wp, ln_g, ln_b, i, alpha)
        zs3 = zs.reshape(db, dt, in_width)
        outs[4].append(zs3[:, :, a_width:2 * a_width].reshape(db, dt, A_HEADS, HEAD_DIM))
        outs[5].append(zs3[:, :, 2 * a_width:3 * a_width].reshape(db, dt, A_HEADS, HEAD_DIM))
        outs[6].append(zs3[:, dt - POOL_HIST:, 3 * a_width:3 * a_width + c_width])
        outs[7].append(_state_from_kernel(st, C_HEADS))

    stacked = [jnp.stack(o, axis=0) for o in outs]
    return (xp.reshape(nb, seq, d), xs.reshape(db, dt, d), *stacked)
```

```python
import collections
import functools

import jax
import jax.numpy as jnp
from jax import lax
from jax.experimental import pallas as pl
from jax.experimental.pallas import tpu as pltpu

F32 = jnp.float32
BF16 = jnp.bfloat16

PAST_LEN = 2048
CHUNK = 64
A_BACK = 8
HEAD_DIM = 64
A_HEADS = 8
C_HEADS = 4
REL_MAX = 128
POOL_WINDOWS = (2, 4, 8, 16)
POOL_HIST = max(POOL_WINDOWS) - 1
POOL_PAD = POOL_HIST + 1
SUB_BLOCK = 32
DECAY_SAFE_LOG = -60.0
TOKEN_SUB_ROWS = 256
LN_EPS = 1e-5
RMS_EPS = 1e-6
NEG_INF = -1e30
LOG2E = 1.4426950408889634

V7X_LANES = 128
V7X_MXU_WIDTH = 256
V7X_VMEM_LIMIT = 56 * 1024 * 1024


def _dot(a, b):
    return jnp.dot(a, b, preferred_element_type=F32)


def _dot_nt(a, b):
    return lax.dot_general(a, b, (((1,), (1,)), ((), ())), preferred_element_type=F32)


def _dot_tn(a, b):
    return lax.dot_general(a, b, (((0,), (0,)), ((), ())), preferred_element_type=F32)


def _layer_norm(y, g, b):
    mu = jnp.mean(y, axis=-1, keepdims=True)
    d = y - mu
    var = jnp.mean(d * d, axis=-1, keepdims=True)
    return d * lax.rsqrt(var + LN_EPS) * g + b


def _sigmoid(x):
    return 1.0 / (1.0 + jnp.exp(-x))


def _log2(n):
    assert n > 0 and n & (n - 1) == 0, n
    return n.bit_length() - 1


def _div(x, n):
    return x >> _log2(n)


def _swiglu_stages(xb, wgu_ref, wdn_ref, n_split):
    hidden = wdn_ref.shape[0]
    groups = _hidden_groups(hidden, n_split)
    gate_up = []
    for lo, hi in groups:
        gate_up.append((_dot(xb, wgu_ref[:, lo:hi]), _dot(xb, wgu_ref[:, hidden + lo:hidden + hi])))
        yield
    acc = None
    for (lo, hi), (gate, up) in zip(groups, gate_up):
        act = (gate * _sigmoid(gate) * up).astype(BF16)
        part = _dot(act, wdn_ref[lo:hi, :])
        acc = part if acc is None else acc + part
        yield
    return acc


def _hidden_groups(hidden, n_split):
    if hidden % V7X_MXU_WIDTH:
        return [(0, hidden)]
    tiles = hidden // V7X_MXU_WIDTH
    bounds = [V7X_MXU_WIDTH * ((tiles * g + n_split - 1) // n_split) for g in range(n_split + 1)]
    return [(lo, hi) for lo, hi in zip(bounds[:-1], bounds[1:]) if hi > lo]


def _run_interleaved(chains):
    live = list(chains)
    while live:
        for ch in list(live):
            if ch not in live:
                continue
            try:
                next(ch)
            except StopIteration:
                live = [c for c in live if c is not ch]


def _sub_tiles(n_rows):
    sub = min(TOKEN_SUB_ROWS, n_rows)
    return [slice(r, r + sub) for r in range(0, n_rows, sub)]


def _pre_kernel(alpha, n_split, x_ref, wgu_ref, wdn_ref, lng_ref, lnb_ref, win_ref, xo_ref, z_ref):
    def chain(rows):
        x = x_ref[rows, :]
        ffn = yield from _swiglu_stages(x.astype(BF16), wgu_ref, wdn_ref, n_split)
        xn = _layer_norm(alpha * x + 0.5 * ffn, lng_ref[0:1, :], lnb_ref[0:1, :])
        xo_ref[rows, :] = xn
        z_ref[rows, :] = _dot(xn.astype(BF16), win_ref[...])
        yield

    _run_interleaved([chain(rows) for rows in _sub_tiles(x_ref.shape[0])])


def _post_chain(alpha, n_split, rows, x_ref, oa_ref, obc_ref, p_ref, wout_ref, wgu_ref, wdn_ref, wg_ref, wp_ref,
                lng_ref, lnb_ref, xo_ref):
    wa = oa_ref.shape[1]
    mixed = _dot(oa_ref[rows, :], wout_ref[0:wa, :]) + _dot(obc_ref[rows, :], wout_ref[wa:, :])
    proj = _dot(p_ref[rows, :].astype(BF16), wp_ref[...])
    yield
    x = _layer_norm(alpha * x_ref[rows, :] + mixed, lng_ref[1:2, :], lnb_ref[1:2, :])
    ffn = yield from _swiglu_stages(x.astype(BF16), wgu_ref, wdn_ref, n_split)
    x = _layer_norm(alpha * x + 0.5 * ffn, lng_ref[2:3, :], lnb_ref[2:3, :])
    emb = _sigmoid(_dot(x.astype(BF16), wg_ref[...])) * proj
    yield
    xo_ref[rows, :] = _layer_norm(alpha * x + emb, lng_ref[3:4, :], lnb_ref[3:4, :])


def _post_kernel(alpha, n_split, x_ref, oa_ref, obc_ref, p_ref, wout_ref, wgu_ref, wdn_ref, wg_ref, wp_ref,
                 lng_ref, lnb_ref, xo_ref):
    _run_interleaved([_post_chain(alpha, n_split, rows, x_ref, oa_ref, obc_ref, p_ref, wout_ref, wgu_ref, wdn_ref,
                                  wg_ref, wp_ref, lng_ref, lnb_ref, xo_ref) for rows in _sub_tiles(x_ref.shape[0])])


def _layer_spec(stacked, layer):
    rest = stacked.shape[1:]
    return pl.BlockSpec((None,) + rest, lambda *_: (layer,) + (0,) * len(rest), pipeline_mode=pl.Buffered(1))


def _token_tile(n):
    for tm in (2 * TOKEN_SUB_ROWS, TOKEN_SUB_ROWS, 128, 64, 32, 16, 8):
        if n % tm == 0:
            return tm
    raise ValueError(f"unsupported token count {n}")


def _pre_call(x, wgu, wdn, lng, lnb, win, layer, alpha):
    n, d = x.shape
    zw = win.shape[-1]
    tm = _token_tile(n)
    row = lambda i: (i, 0)
    consts = (wgu, wdn, lng, lnb, win)
    return pl.pallas_call(
        functools.partial(_pre_kernel, alpha, 2),
        grid=(n // tm,),
        in_specs=[pl.BlockSpec((tm, d), row)] + [_layer_spec(c, layer) for c in consts],
        out_specs=[pl.BlockSpec((tm, d), row), pl.BlockSpec((tm, zw), row)],
        out_shape=[jax.ShapeDtypeStruct((n, d), F32), jax.ShapeDtypeStruct((n, zw), F32)],
        compiler_params=pltpu.CompilerParams(dimension_semantics=("parallel",),
                                             vmem_limit_bytes=V7X_VMEM_LIMIT),
        name="pre_mixer_tokens",
    )(x, *consts)


def _post_call(x, oa, obc, p, wout, wgu, wdn, wg, wp, lng, lnb, layer, alpha):
    n, d = x.shape
    tm = _token_tile(n)
    row = lambda i: (i, 0)
    consts = (wout, wgu, wdn, wg, wp, lng, lnb)
    return pl.pallas_call(
        functools.partial(_post_kernel, alpha, 2),
        grid=(n // tm,),
        in_specs=[pl.BlockSpec((tm, d), row), pl.BlockSpec((tm, oa.shape[1]), row),
                  pl.BlockSpec((tm, obc.shape[1]), row),
                  pl.BlockSpec((None, tm, p.shape[-1]), lambda i: (layer, i, 0))]
        + [_layer_spec(c, layer) for c in consts],
        out_specs=pl.BlockSpec((tm, d), row),
        out_shape=jax.ShapeDtypeStruct((n, d), F32),
        compiler_params=pltpu.CompilerParams(dimension_semantics=("parallel",),
                                             vmem_limit_bytes=V7X_VMEM_LIMIT),
        name="post_mixer_tokens",
    )(x, oa, obc, p, *consts)


def _qk_scores(q_ref, kcat, s_out, r0, n_q, k0, n_k):
    scale = HEAD_DIM ** -0.5 * LOG2E
    low = lax.broadcasted_iota(jnp.int32, (1, V7X_LANES), 1) < HEAD_DIM
    for hp in range(q_ref.shape[1] // V7X_LANES):
        ls = slice(hp * V7X_LANES, (hp + 1) * V7X_LANES)
        q = q_ref[r0:r0 + n_q, ls] * scale
        lhs = jnp.concatenate([jnp.where(low, q, 0.0), jnp.where(low, 0.0, q)], axis=0).astype(BF16)
        s_out[2 * hp * n_q:2 * (hp + 1) * n_q, :] = _dot_nt(lhs, kcat[k0:k0 + n_k, ls])


def _softmax_weights(s_in, bias_ref, key_valid, p_out):
    s = s_in[...] + bias_ref[...]
    if key_valid is not None:
        s = jnp.where(key_valid, s, NEG_INF)
    p_out[...] = jnp.exp2(s - jnp.max(s, axis=-1, keepdims=True)).astype(BF16)


def _weighted_values(p_in, vcat, o_ref, r0, n_q, k0, n_k):
    low = lax.broadcasted_iota(jnp.int32, (1, V7X_LANES), 1) < HEAD_DIM
    ones = jnp.ones((n_k, V7X_LANES), BF16)
    for hp in range(o_ref.shape[1] // V7X_LANES):
        ls = slice(hp * V7X_LANES, (hp + 1) * V7X_LANES)
        rows = slice(2 * hp * n_q, 2 * (hp + 1) * n_q)
        pv = _dot(p_in[rows, :], jnp.concatenate([vcat[k0:k0 + n_k, ls], ones], axis=1))
        o2 = pv[:, :V7X_LANES] * (1.0 / pv[:, V7X_LANES:])
        o_ref[r0:r0 + n_q, ls] = jnp.where(low, o2[:n_q], o2[n_q:]).astype(o_ref.dtype)


def _build_rel_bias(table_ref, bias_ref, n_q, n_k, offset):
    n_heads = table_ref.shape[0]
    clip = lambda d: min(max(d, -(CHUNK - 1)), REL_MAX) + (CHUNK - 1)
    heads_per_pass = 4
    for j0 in range(0, n_k, V7X_LANES):
        jw = min(V7X_LANES, n_k - j0)
        r_lo, r_hi = clip(offset - (j0 + jw - 1)), clip(offset + n_q - 1 - j0)
        d = (offset - j0 + lax.broadcasted_iota(jnp.int32, (n_q, jw), 0)
             - lax.broadcasted_iota(jnp.int32, (n_q, jw), 1))
        idx = jnp.clip(d, -(CHUNK - 1), REL_MAX) + (CHUNK - 1)
        for h0 in range(0, n_heads, heads_per_pass):
            heads = range(h0, min(h0 + heads_per_pass, n_heads))
            if r_lo == r_hi:
                vals = [jnp.full((n_q, jw), table_ref[h, r_lo], F32) for h in heads]
            else:
                def pick(r, acc, heads=heads, idx=idx):
                    hit = idx == r
                    return tuple(jnp.where(hit, table_ref[h, r], a) for h, a in zip(heads, acc))
                vals = lax.fori_loop(r_lo, r_hi + 1, pick, tuple(jnp.zeros((n_q, jw), F32) for _ in heads))
            for h, v in zip(heads, vals):
                bias_ref[h * n_q:(h + 1) * n_q, j0:j0 + jw] = v * LOG2E


def _attn_prompt_kernel(qb_rows, table_ref, q_ref, kp_ref, kc_ref, vp_ref, vc_ref, o_ref,
                        kcat, vcat, bias_ref, s_buf, p_buf):
    i = pl.program_id(1)
    band = (A_BACK + 1) * CHUNK
    past = A_BACK * CHUNK
    n_chunks = qb_rows // CHUNK

    @pl.when((pl.program_id(0) == 0) & (i == 0))
    def _():
        _build_rel_bias(table_ref, bias_ref, CHUNK, band, past)

    def cast_rows(j, carry):
        r = pl.multiple_of(j * CHUNK, CHUNK)
        kcat[pl.ds(r, CHUNK), :] = kp_ref[pl.ds(r, CHUNK), :].astype(BF16)
        kcat[pl.ds(qb_rows + r, CHUNK), :] = kc_ref[pl.ds(r, CHUNK), :].astype(BF16)
        vcat[pl.ds(r, CHUNK), :] = vp_ref[pl.ds(r, CHUNK), :].astype(BF16)
        vcat[pl.ds(qb_rows + r, CHUNK), :] = vc_ref[pl.ds(r, CHUNK), :].astype(BF16)
        return carry

    lax.fori_loop(0, n_chunks, cast_rows, 0)
    key_idx = lax.broadcasted_iota(jnp.int32, (1, band), 1)

    _qk_scores(q_ref, kcat, s_buf.at[0], 0, CHUNK, 0, band)
    for c in range(n_chunks + 1):
        if c + 1 < n_chunks:
            _qk_scores(q_ref, kcat, s_buf.at[(c + 1) % 2], (c + 1) * CHUNK, CHUNK, (c + 1) * CHUNK, band)
        if c < n_chunks:
            key_valid = key_idx >= past - c * CHUNK - i * qb_rows
            _softmax_weights(s_buf.at[c % 2], bias_ref, key_valid, p_buf.at[c % 2])
        if c >= 1:
            _weighted_values(p_buf.at[(c - 1) % 2], vcat, o_ref,
                             (c - 1) * CHUNK, CHUNK, (c - 1) * CHUNK, band)


def _attn_prompt_call(z, table, batch, seq, a_width):
    qb_rows = A_BACK * CHUNK
    assert seq % qb_rows == 0 and a_width % V7X_LANES == 0
    nq = seq // qb_rows
    band = (A_BACK + 1) * CHUNK
    rows = A_HEADS * CHUNK
    cur = lambda col: (lambda b, i: (b * nq + i, col))
    prev = lambda col: (lambda b, i: (b * nq + jnp.maximum(i - 1, 0), col))
    blk = (qb_rows, a_width)
    return pl.pallas_call(
        functools.partial(_attn_prompt_kernel, qb_rows),
        grid=(batch, nq),
        in_specs=[pl.BlockSpec(memory_space=pltpu.SMEM),
                  pl.BlockSpec(blk, cur(0)), pl.BlockSpec(blk, prev(1)), pl.BlockSpec(blk, cur(1)),
                  pl.BlockSpec(blk, prev(2)), pl.BlockSpec(blk, cur(2))],
        out_specs=pl.BlockSpec(blk, lambda b, i: (b * nq + i, 0)),
        out_shape=jax.ShapeDtypeStruct((batch * seq, a_width), BF16),
        scratch_shapes=[pltpu.VMEM((2 * qb_rows, a_width), BF16), pltpu.VMEM((2 * qb_rows, a_width), BF16),
                        pltpu.VMEM((rows, band), F32), pltpu.VMEM((2, rows, band), F32),
                        pltpu.VMEM((2, rows, band), BF16)],
        compiler_params=pltpu.CompilerParams(dimension_semantics=("arbitrary", "arbitrary"),
                                             vmem_limit_bytes=V7X_VMEM_LIMIT),
        name="band_attention_prompt",
    )(table, z, z, z, z, z)


def _attn_sample_kernel(table_ref, q_ref, k_ref, v_ref, ck_ref, cv_ref, o_ref, kcat, vcat, bias_ref,
                        s_buf, p_buf):
    lc = ck_ref.shape[0]
    t = q_ref.shape[0]

    @pl.when(pl.program_id(0) == 0)
    def _():
        _build_rel_bias(table_ref, bias_ref, t, lc + t, lc)

    kcat[0:lc, :] = ck_ref[...].astype(BF16)
    kcat[lc:, :] = k_ref[...].astype(BF16)
    vcat[0:lc, :] = cv_ref[...].astype(BF16)
    vcat[lc:, :] = v_ref[...].astype(BF16)
    _qk_scores(q_ref, kcat, s_buf, 0, t, 0, lc + t)
    _softmax_weights(s_buf, bias_ref, None, p_buf)
    _weighted_values(p_buf, vcat, o_ref, 0, t, 0, lc + t)


def _attn_sample_call(z, cache_k, cache_v, table, layer, batch, t, a_width):
    lc = cache_k.shape[2]
    rows = A_HEADS * t
    blk = (t, a_width)
    cblk = (None, None, lc, a_width)
    cidx = lambda b: (layer, b, 0, 0)
    return pl.pallas_call(
        _attn_sample_kernel,
        grid=(batch,),
        in_specs=[pl.BlockSpec(memory_space=pltpu.SMEM),
                  pl.BlockSpec(blk, lambda b: (b, 0)), pl.BlockSpec(blk, lambda b: (b, 1)),
                  pl.BlockSpec(blk, lambda b: (b, 2)),
                  pl.BlockSpec(cblk, cidx), pl.BlockSpec(cblk, cidx)],
        out_specs=pl.BlockSpec(blk, lambda b: (b, 0)),
        out_shape=jax.ShapeDtypeStruct((batch * t, a_width), BF16),
        scratch_shapes=[pltpu.VMEM((lc + t, a_width), BF16), pltpu.VMEM((lc + t, a_width), BF16),
                        pltpu.VMEM((rows, lc + t), F32), pltpu.VMEM((rows, lc + t), F32),
                        pltpu.VMEM((rows, lc + t), BF16)],
        compiler_params=pltpu.CompilerParams(dimension_semantics=("arbitrary",),
                                             vmem_limit_bytes=V7X_VMEM_LIMIT),
        name="band_attention_sample",
    )(table, z, z, z, cache_k, cache_v)


def _split2(x):
    a = x.astype(BF16)
    return a, (x - a.astype(F32)).astype(BF16)


_RecurRefs = collections.namedtuple(
    "_RecurRefs", "u q f i g lbraw normw poolw pscale out state hist dmask inter rows_b rows_k")


def _recur_reset(R, hist0_ref, st0_ref, blk, tm):
    sb = min(SUB_BLOCK, blk)
    R.state[...] = st0_ref[...]
    R.hist[...] = hist0_ref[...]
    ri = lax.broadcasted_iota(jnp.int32, (tm, tm), 0)
    ci = lax.broadcasted_iota(jnp.int32, (tm, tm), 1)
    same_sb = _div(ri, sb) == _div(ci, sb)
    R.dmask[0] = jnp.where(same_sb & (ci <= ri), 1.0, 0.0).astype(BF16)
    R.dmask[1] = jnp.where(same_sb, 1.0, 0.0).astype(BF16)


def _recur_tile(layer, blk, pos0, n_heads, R, rows, t_tile, fallbacks):
    tm = rows.stop - rows.start
    width = R.u.shape[1]
    hd = width // n_heads
    sb = min(SUB_BLOCK, blk)
    assert blk // sb in (1, 2) and tm % blk == 0

    raw = R.lbraw[...]
    sm = jnp.exp(raw - jnp.max(raw, axis=0, keepdims=True))
    sm = sm / jnp.sum(sm, axis=0, keepdims=True)
    lb = jnp.zeros((1, width), F32)
    for j in range(1, layer + 1):
        lb = lb + sm[j:j + 1, :]
    forget = lb + (1.0 - lb) * _sigmoid(R.f[rows, :])
    log_f = jnp.log(forget)
    k_in = 1.0 - forget
    qx = R.q[rows, :]
    qf = qx * _sigmoid(qx)
    vb = R.i[rows, :].astype(BF16)

    lf2 = jnp.concatenate(_split2(log_f), axis=1)

    def decay_sum(mask01):
        r = _dot(mask01, lf2)
        return r[:, 0:width] + r[:, width:]

    b_rel = decay_sum(R.dmask[0])
    sb_tot = decay_sum(R.dmask[1])
    yield
    if blk > sb:
        second = (lax.broadcasted_iota(jnp.int32, (tm, 1), 0) & (blk - 1)) >= sb
        prev_tot = pltpu.roll(sb_tot, sb, 0)
        next_tot = pltpu.roll(sb_tot, tm - sb, 0)
        b = b_rel + jnp.where(second, prev_tot, 0.0)
        blk_tot = sb_tot + jnp.where(second, prev_tot, next_tot)
    else:
        b, blk_tot = b_rel, sb_tot

    ext = jnp.concatenate([R.hist[...], R.u[rows, :]], axis=0)
    R.hist[...] = ext[tm:, :]
    s2 = ext + pltpu.roll(ext, 1, 0)
    s4 = s2 + pltpu.roll(s2, 2, 0)
    s8 = s4 + pltpu.roll(s4, 4, 0)
    s16 = s8 + pltpu.roll(s8, 8, 0)
    lane = lax.broadcasted_iota(jnp.int32, (1, width), 1)
    grp = _div(lane, width // len(POOL_WINDOWS))
    wsum = jnp.where(grp == 0, s2, jnp.where(grp == 1, s4, jnp.where(grp == 2, s8, s16)))
    wlen = jnp.where(grp == 0, 2.0, jnp.where(grp == 1, 4.0, jnp.where(grp == 2, 8.0, 16.0)))
    row = lax.broadcasted_iota(jnp.int32, (POOL_PAD + tm, 1), 0)
    pos = (row + (pos0 - POOL_PAD) + t_tile * tm).astype(F32)
    cnt = jnp.maximum(jnp.minimum(pos + 1.0, wlen), 1.0)
    dev = (wsum / cnt - ext)[POOL_PAD:, :]
    o_pool = _dot(dev.astype(BF16), R.poolw[...]) * R.pscale[...]
    R.out[rows, 0:width] = o_pool.astype(R.out.dtype)

    q_rel = qf * jnp.exp(b_rel)
    k_rel = (k_in * jnp.exp(-b_rel)).astype(BF16)
    k_end = (k_in * jnp.exp(sb_tot - b_rel)).astype(BF16)
    q_abs = (qf * jnp.exp(b)).astype(BF16)
    k_tail = (k_in * jnp.exp(blk_tot - b)).astype(BF16)

    hb = n_heads * blk
    n_chunks = tm // blk
    stack_head = _div(lax.broadcasted_iota(jnp.int32, (hb, 1), 0), blk) == _div(lane, hd)
    tq = lax.broadcasted_iota(jnp.int32, (hb, blk), 0) & (blk - 1)
    ts = lax.broadcasted_iota(jnp.int32, (hb, blk), 1)
    m_intra = (_div(tq, sb) == _div(ts, sb)) & (ts <= tq)
    m_cross = _div(ts, sb) < _div(tq, sb)
    bd = (_div(lax.broadcasted_iota(jnp.int32, (width, width), 0), hd)
          == _div(lax.broadcasted_iota(jnp.int32, (width, width), 1), hd))
    ones_bd = jnp.where(bd, 1.0, 0.0).astype(BF16)
    gx = R.g[rows, :]
    out_gate = R.normw[...] * (gx * _sigmoid(gx))

    def write_output(o):
        sq_hi, sq_lo = _split2(o * o)
        ms = (_dot(sq_hi, ones_bd) + _dot(sq_lo, ones_bd)) * (1.0 / hd)
        R.out[rows, width:] = (o * lax.rsqrt(ms + RMS_EPS) * out_gate).astype(R.out.dtype)

    chunks = [slice(c * blk, (c + 1) * blk) for c in range(n_chunks)]
    state_in = [jnp.where(bd, _dot_tn(vb[rs], k_tail[rs]), 0.0) for rs in chunks]
    yield
    two_sets = blk > sb
    if two_sets:
        tq2 = lax.broadcasted_iota(jnp.int32, (hb, 2 * blk), 0) & (blk - 1)
        col = lax.broadcasted_iota(jnp.int32, (hb, 2 * blk), 1)
        ts2 = col & (blk - 1)
        keep = (((col < blk) & (_div(tq2, sb) == _div(ts2, sb)) & (ts2 <= tq2))
                | ((col >= blk) & (_div(ts2, sb) < _div(tq2, sb))))
    else:
        keep = m_intra
    scores = []
    for rs in chunks:
        lhs = jnp.where(stack_head, jnp.concatenate([q_rel[rs]] * n_heads, axis=0), 0.0).astype(BF16)
        keys = jnp.concatenate([k_rel[rs], k_end[rs]], axis=0) if two_sets else k_rel[rs]
        scores.append(jnp.where(keep, _dot_nt(lhs, keys), 0.0).astype(BF16))
    yield
    outs = []
    st = R.state[...]
    for c, rs in enumerate(chunks):
        from_state = _dot_nt(q_abs[rs], st.astype(BF16))
        values = jnp.concatenate([vb[rs], vb[rs]], axis=0) if two_sets else vb[rs]
        stacked = jnp.where(stack_head, _dot(scores[c], values), 0.0)
        o_c = stacked[0:blk]
        for h in range(1, n_heads):
            o_c = o_c + stacked[h * blk:(h + 1) * blk]
        R.inter[pl.ds(rows.start + c * blk, blk), :] = from_state
        outs.append(o_c + from_state)
        decay_end = jnp.exp(blk_tot[c * blk:c * blk + 1, :])
        st = st * decay_end + state_in[c]
        yield
    R.state[...] = st
    write_output(outs[0] if len(outs) == 1 else jnp.concatenate(outs, axis=0))

    def fallback():
        @pl.when(jnp.min(sb_tot) <= DECAY_SAFE_LOG)
        def _():
            R.rows_b[...] = b
            R.rows_k[...] = k_in
            t_row = lax.broadcasted_iota(jnp.int32, (blk, 1), 0)

            def add_key_row(s, acc):
                parts = []
                for c, rs in enumerate(chunks):
                    r = c * blk + s
                    rel = jnp.minimum(b[rs] - R.rows_b[pl.ds(r, 1), :], 0.0)
                    term = jnp.where(t_row >= s, qf[rs] * jnp.exp(rel) * R.rows_k[pl.ds(r, 1), :], 0.0)
                    parts.append(_dot(term.astype(BF16), ones_bd) * R.i[pl.ds(rows.start + r, 1), :])
                return acc + (parts[0] if n_chunks == 1 else jnp.concatenate(parts, axis=0))

            write_output(lax.fori_loop(0, blk, add_key_row, R.inter[rows, :]))

    fallbacks.append(fallback)
    yield


def _run_staggered(chains, lag):
    live = list(enumerate(chains))
    rnd = 0
    while live:
        for item in list(live):
            k, ch = item
            if rnd >= k * lag:
                try:
                    next(ch)
                except StopIteration:
                    live.remove(item)
        rnd += 1


def _recur_kernel(layer, blk, pos0, n_heads, rtm, u_ref, q_ref, f_ref, i_ref, g_ref, hist0_ref, st0_ref, lbraw_ref,
                  normw_ref, poolw_ref, pscale_ref, o_ref, st_ref, state, hist, dmask, inter, rows_b, rows_k):
    t_idx = pl.program_id(1)
    tiles = u_ref.shape[0] // rtm
    R = _RecurRefs(u_ref, q_ref, f_ref, i_ref, g_ref, lbraw_ref, normw_ref, poolw_ref, pscale_ref, o_ref,
                   state, hist, dmask, inter, rows_b, rows_k)

    @pl.when(t_idx == 0)
    def _():
        _recur_reset(R, hist0_ref, st0_ref, blk, rtm)

    fallbacks = []
    _run_staggered([_recur_tile(layer, blk, pos0, n_heads, R, slice(k * rtm, (k + 1) * rtm), t_idx * tiles + k,
                                fallbacks) for k in range(tiles)], lag=rtm // blk)
    for fb in fallbacks:
        fb()

    @pl.when(t_idx == pl.num_programs(1) - 1)
    def _():
        st_ref[...] = state[...]


RECUR_TILE = 256
RECUR_STEP_ROWS = 1024


def _recur_scratch(rows, tm, width):
    return [pltpu.VMEM((width, width), F32), pltpu.VMEM((POOL_PAD, width), F32), pltpu.VMEM((2, tm, tm), BF16),
            pltpu.VMEM((rows, width), F32), pltpu.VMEM((tm, width), F32), pltpu.VMEM((tm, width), F32)]


def _recur_call(z, hist0, st0, lbraw, normw, poolw, pscale, layer, state_layer, batch, seq, blk, pos0, col0):
    width = st0.shape[-1]
    rtm = min(RECUR_TILE, seq)
    tm = min(RECUR_STEP_ROWS, seq)
    assert seq % tm == 0 and tm % rtm == 0 and rtm % blk == 0
    nt = seq // tm
    col = lambda c: (lambda b, t: (b * nt + t, c))
    per_batch = lambda b, t: (state_layer, b, 0, 0)
    consts = (normw, poolw, pscale)
    return pl.pallas_call(
        functools.partial(_recur_kernel, layer, blk, pos0, C_HEADS, rtm),
        grid=(batch, nt),
        in_specs=[pl.BlockSpec((tm, width), col(col0 + j)) for j in range(5)]
        + [pl.BlockSpec((None, None, POOL_PAD, width), per_batch),
           pl.BlockSpec((None, None, width, width), per_batch),
           pl.BlockSpec(lbraw.shape, lambda b, t: (0, 0), pipeline_mode=pl.Buffered(1))]
        + [_layer_spec(c, layer) for c in consts],
        out_specs=[pl.BlockSpec((tm, 2 * width), lambda b, t: (b * nt + t, 0)),
                   pl.BlockSpec((None, width, width), lambda b, t: (b, 0, 0))],
        out_shape=[jax.ShapeDtypeStruct((batch * seq, 2 * width), BF16),
                   jax.ShapeDtypeStruct((batch, width, width), F32)],
        scratch_shapes=_recur_scratch(tm, rtm, width),
        compiler_params=pltpu.CompilerParams(dimension_semantics=("parallel", "arbitrary"),
                                             vmem_limit_bytes=V7X_VMEM_LIMIT),
        name="pool_hgrn_mixer",
    )(z, z, z, z, z, hist0, st0, lbraw, *consts)


def _block_diag(blocks):
    g = blocks.shape[-3]
    zero = jnp.zeros_like(blocks[..., 0, :, :])
    rows = [jnp.concatenate([blocks[..., h, :, :] if j == h else zero for j in range(g)], axis=-1)
            for h in range(g)]
    return jnp.concatenate(rows, axis=-2)


def _state_from_kernel(st, n_heads):
    b, w, _ = st.shape
    hd = w // n_heads
    blocks = jnp.stack([st[:, h * hd:(h + 1) * hd, h * hd:(h + 1) * hd] for h in range(n_heads)], axis=1)
    return jnp.swapaxes(blocks, -1, -2)


def kernel(x_prompt, x_sample, p_prompt, p_sample, cache_attn_k, cache_attn_v, state_pool, state_hgrn, ffn1_w_gu, ffn1_w_down, w_in, attn_rel_bias, pool_w, pool_scale, hgrn_lower_bounds, hgrn_norm_w, w_out, ffn2_w_gu, ffn2_w_down, ple_w_gate, ple_w_proj, ln_g, ln_b):
    depth = w_in.shape[0]
    alpha = float((2 * depth) ** 0.25)
    nb, seq, d = x_prompt.shape
    db, dt, _ = x_sample.shape
    a_width = A_HEADS * HEAD_DIM
    c_width = C_HEADS * HEAD_DIM
    in_width = w_in.shape[-1]
    col0 = 3 * a_width // c_width
    past_rows = min(A_BACK * CHUNK, seq)
    assert dt >= POOL_HIST and seq >= POOL_HIST and PAST_LEN >= POOL_HIST

    bf = lambda w: w.astype(BF16)
    w1gu, w1dn, win, wout = bf(ffn1_w_gu), bf(ffn1_w_down), bf(w_in), bf(w_out)
    w2gu, w2dn, wg, wp = bf(ffn2_w_gu), bf(ffn2_w_down), bf(ple_w_gate), bf(ple_w_proj)
    poolw = bf(_block_diag(pool_w))
    pscale = pool_scale.astype(F32)[:, None, :]
    normw = jnp.tile(hgrn_norm_w.astype(F32), (1, C_HEADS))[:, None, :]
    lbraw = hgrn_lower_bounds.astype(F32)
    pp = p_prompt.reshape(depth, nb * seq, -1)
    ps = p_sample.reshape(depth, db * dt, -1)

    zero_hist = jnp.zeros((1, nb, POOL_PAD, c_width), F32)
    zero_state = jnp.zeros((1, nb, c_width, c_width), F32)
    hist_s = jnp.pad(state_pool.astype(F32), ((0, 0), (0, 0), (POOL_PAD - POOL_HIST, 0), (0, 0)))
    state_s = _block_diag(jnp.swapaxes(state_hgrn.astype(F32), -1, -2))

    lc = cache_attn_k.shape[2]
    cache_k = cache_attn_k.reshape(depth, db, lc, a_width)
    cache_v = cache_attn_v.reshape(depth, db, lc, a_width)

    xp = x_prompt.reshape(nb * seq, d)
    xs = x_sample.reshape(db * dt, d)
    outs = [[] for _ in range(8)]

    for i in range(depth):
        rel_table = attn_rel_bias[i].astype(F32)

        xp, zp = _pre_call(xp, w1gu, w1dn, ln_g, ln_b, win, i, alpha)
        oa = _attn_prompt_call(zp, rel_table, nb, seq, a_width)
        obc, st = _recur_call(zp, zero_hist, zero_state, lbraw, normw, poolw, pscale, i, 0, nb, seq, CHUNK, 0, col0)
        xp = _post_call(xp, oa, obc, pp, wout, w2gu, w2dn, wg, wp, ln_g, ln_b, i, alpha)
        zp3 = zp.reshape(nb, seq, in_width)
        outs[0].append(zp3[:, seq - past_rows:, a_width:2 * a_width].reshape(nb, past_rows, A_HEADS, HEAD_DIM))
        outs[1].append(zp3[:, seq - past_rows:, 2 * a_width:3 * a_width].reshape(nb, past_rows, A_HEADS, HEAD_DIM))
        outs[2].append(zp3[:, seq - POOL_HIST:, 3 * a_width:3 * a_width + c_width])
        outs[3].append(_state_from_kernel(st, C_HEADS))

        xs, zs = _pre_call(xs, w1gu, w1dn, ln_g, ln_b, win, i, alpha)
        oa = _attn_sample_call(zs, cache_k, cache_v, rel_table, i, db, dt, a_width)
        obc, st = _recur_call(zs, hist_s, state_s, lbraw, normw, poolw, pscale, i, i, db, dt, dt, PAST_LEN, col0)
        xs = _post_call(xs, oa, obc, ps, wout, w2gu, w2dn, wg, wp, ln_g, ln_b, i, alpha)
        zs3 = zs.reshape(db, dt, in_width)
        outs[4].append(zs3[:, :, a_width:2 * a_width].reshape(db, dt, A_HEADS, HEAD_DIM))
        outs[5].append(zs3[:, :, 2 * a_width:3 * a_width].reshape(db, dt, A_HEADS, HEAD_DIM))
        outs[6].append(zs3[:, dt - POOL_HIST:, 3 * a_width:3 * a_width + c_width])
        outs[7].append(_state_from_kernel(st, C_HEADS))

    stacked = [jnp.stack(o, axis=0) for o in outs]
    return (xp.reshape(nb, seq, d), xs.reshape(db, dt, d), *stacked)
```

```python
import collections
import functools

import jax
import jax.numpy as jnp
from jax import lax
from jax.experimental import pallas as pl
from jax.experimental.pallas import tpu as pltpu

F32 = jnp.float32
BF16 = jnp.bfloat16

PAST_LEN = 2048
CHUNK = 64
A_BACK = 8
HEAD_DIM = 64
A_HEADS = 8
C_HEADS = 4
REL_MAX = 128
POOL_WINDOWS = (2, 4, 8, 16)
POOL_HIST = max(POOL_WINDOWS) - 1
POOL_PAD = POOL_HIST + 1
SUB_BLOCK = 32
DECAY_SAFE_LOG = -60.0
TOKEN_SUB_ROWS = 256
ATTN_STEP_SEQS = 4
LN_EPS = 1e-5
RMS_EPS = 1e-6
NEG_INF = -1e30
LOG2E = 1.4426950408889634

V7X_LANES = 128
V7X_MXU_WIDTH = 256
V7X_VMEM_LIMIT = 56 * 1024 * 1024


def _dot(a, b):
    return jnp.dot(a, b, preferred_element_type=F32)


def _dot_nt(a, b):
    return lax.dot_general(a, b, (((1,), (1,)), ((), ())), preferred_element_type=F32)


def _dot_tn(a, b):
    return lax.dot_general(a, b, (((0,), (0,)), ((), ())), preferred_element_type=F32)


def _layer_norm(y, g, b):
    mu = jnp.mean(y, axis=-1, keepdims=True)
    d = y - mu
    var = jnp.mean(d * d, axis=-1, keepdims=True)
    return d * lax.rsqrt(var + LN_EPS) * g + b


def _sigmoid(x):
    return 1.0 / (1.0 + jnp.exp(-x))


def _log2(n):
    assert n > 0 and n & (n - 1) == 0, n
    return n.bit_length() - 1


def _div(x, n):
    return x >> _log2(n)


def _swiglu_stages(xb, wgu_ref, wdn_ref, n_split):
    hidden = wdn_ref.shape[0]
    groups = _hidden_groups(hidden, n_split)
    gate_up = []
    for lo, hi in groups:
        gate_up.append((_dot(xb, wgu_ref[:, lo:hi]), _dot(xb, wgu_ref[:, hidden + lo:hidden + hi])))
        yield
    acc = None
    for (lo, hi), (gate, up) in zip(groups, gate_up):
        act = (gate * _sigmoid(gate) * up).astype(BF16)
        part = _dot(act, wdn_ref[lo:hi, :])
        acc = part if acc is None else acc + part
        yield
    return acc


def _hidden_groups(hidden, n_split):
    if hidden % V7X_MXU_WIDTH:
        return [(0, hidden)]
    tiles = hidden // V7X_MXU_WIDTH
    bounds = [V7X_MXU_WIDTH * ((tiles * g + n_split - 1) // n_split) for g in range(n_split + 1)]
    return [(lo, hi) for lo, hi in zip(bounds[:-1], bounds[1:]) if hi > lo]


def _run_interleaved(chains):
    live = list(chains)
    while live:
        for ch in list(live):
            if ch not in live:
                continue
            try:
                next(ch)
            except StopIteration:
                live = [c for c in live if c is not ch]


def _sub_tiles(n_rows):
    sub = min(TOKEN_SUB_ROWS, n_rows)
    return [slice(r, r + sub) for r in range(0, n_rows, sub)]


def _pre_kernel(alpha, n_split, x_ref, wgu_ref, wdn_ref, lng_ref, lnb_ref, win_ref, xo_ref, z_ref):
    def chain(rows):
        x = x_ref[rows, :]
        ffn = yield from _swiglu_stages(x.astype(BF16), wgu_ref, wdn_ref, n_split)
        xn = _layer_norm(alpha * x + 0.5 * ffn, lng_ref[0:1, :], lnb_ref[0:1, :])
        xo_ref[rows, :] = xn
        z_ref[rows, :] = _dot(xn.astype(BF16), win_ref[...])
        yield

    _run_interleaved([chain(rows) for rows in _sub_tiles(x_ref.shape[0])])


def _post_chain(alpha, n_split, rows, x_ref, oa_ref, obc_ref, p_ref, wout_ref, wgu_ref, wdn_ref, wg_ref, wp_ref,
                lng_ref, lnb_ref, xo_ref):
    wa = oa_ref.shape[1]
    mixed = _dot(oa_ref[rows, :], wout_ref[0:wa, :]) + _dot(obc_ref[rows, :], wout_ref[wa:, :])
    proj = _dot(p_ref[rows, :].astype(BF16), wp_ref[...])
    yield
    x = _layer_norm(alpha * x_ref[rows, :] + mixed, lng_ref[1:2, :], lnb_ref[1:2, :])
    ffn = yield from _swiglu_stages(x.astype(BF16), wgu_ref, wdn_ref, n_split)
    x = _layer_norm(alpha * x + 0.5 * ffn, lng_ref[2:3, :], lnb_ref[2:3, :])
    emb = _sigmoid(_dot(x.astype(BF16), wg_ref[...])) * proj
    yield
    xo_ref[rows, :] = _layer_norm(alpha * x + emb, lng_ref[3:4, :], lnb_ref[3:4, :])


def _post_kernel(alpha, n_split, x_ref, oa_ref, obc_ref, p_ref, wout_ref, wgu_ref, wdn_ref, wg_ref, wp_ref,
                 lng_ref, lnb_ref, xo_ref):
    _run_interleaved([_post_chain(alpha, n_split, rows, x_ref, oa_ref, obc_ref, p_ref, wout_ref, wgu_ref, wdn_ref,
                                  wg_ref, wp_ref, lng_ref, lnb_ref, xo_ref) for rows in _sub_tiles(x_ref.shape[0])])


def _layer_spec(stacked, layer):
    rest = stacked.shape[1:]
    return pl.BlockSpec((None,) + rest, lambda *_: (layer,) + (0,) * len(rest), pipeline_mode=pl.Buffered(1))


def _token_tile(n):
    for tm in (2 * TOKEN_SUB_ROWS, TOKEN_SUB_ROWS, 128, 64, 32, 16, 8):
        if n % tm == 0:
            return tm
    raise ValueError(f"unsupported token count {n}")


def _pre_call(x, wgu, wdn, lng, lnb, win, layer, alpha):
    n, d = x.shape
    zw = win.shape[-1]
    tm = _token_tile(n)
    row = lambda i: (i, 0)
    consts = (wgu, wdn, lng, lnb, win)
    return pl.pallas_call(
        functools.partial(_pre_kernel, alpha, 2),
        grid=(n // tm,),
        in_specs=[pl.BlockSpec((tm, d), row)] + [_layer_spec(c, layer) for c in consts],
        out_specs=[pl.BlockSpec((tm, d), row), pl.BlockSpec((tm, zw), row)],
        out_shape=[jax.ShapeDtypeStruct((n, d), F32), jax.ShapeDtypeStruct((n, zw), F32)],
        compiler_params=pltpu.CompilerParams(dimension_semantics=("parallel",),
                                             vmem_limit_bytes=V7X_VMEM_LIMIT),
        name="pre_mixer_tokens",
    )(x, *consts)


def _post_call(x, oa, obc, p, wout, wgu, wdn, wg, wp, lng, lnb, layer, alpha):
    n, d = x.shape
    tm = _token_tile(n)
    row = lambda i: (i, 0)
    consts = (wout, wgu, wdn, wg, wp, lng, lnb)
    return pl.pallas_call(
        functools.partial(_post_kernel, alpha, 2),
        grid=(n // tm,),
        in_specs=[pl.BlockSpec((tm, d), row), pl.BlockSpec((tm, oa.shape[1]), row),
                  pl.BlockSpec((tm, obc.shape[1]), row),
                  pl.BlockSpec((None, tm, p.shape[-1]), lambda i: (layer, i, 0))]
        + [_layer_spec(c, layer) for c in consts],
        out_specs=pl.BlockSpec((tm, d), row),
        out_shape=jax.ShapeDtypeStruct((n, d), F32),
        compiler_params=pltpu.CompilerParams(dimension_semantics=("parallel",),
                                             vmem_limit_bytes=V7X_VMEM_LIMIT),
        name="post_mixer_tokens",
    )(x, oa, obc, p, *consts)


def _qk_scores(q_ref, kcat, s_out, r0, n_q, k0, n_k):
    scale = HEAD_DIM ** -0.5 * LOG2E
    low = lax.broadcasted_iota(jnp.int32, (1, V7X_LANES), 1) < HEAD_DIM
    for hp in range(q_ref.shape[1] // V7X_LANES):
        ls = slice(hp * V7X_LANES, (hp + 1) * V7X_LANES)
        q = q_ref[r0:r0 + n_q, ls] * scale
        lhs = jnp.concatenate([jnp.where(low, q, 0.0), jnp.where(low, 0.0, q)], axis=0).astype(BF16)
        s_out[2 * hp * n_q:2 * (hp + 1) * n_q, :] = _dot_nt(lhs, kcat[k0:k0 + n_k, ls])


def _softmax_weights(s_in, bias_ref, key_valid, p_out):
    s = s_in[...] + bias_ref[...]
    if key_valid is not None:
        s = jnp.where(key_valid, s, NEG_INF)
    p_out[...] = jnp.exp2(s - jnp.max(s, axis=-1, keepdims=True)).astype(BF16)


def _weighted_values(p_in, vcat, o_ref, r0, n_q, k0, n_k):
    low = lax.broadcasted_iota(jnp.int32, (1, V7X_LANES), 1) < HEAD_DIM
    ones = jnp.ones((n_k, V7X_LANES), BF16)
    for hp in range(o_ref.shape[1] // V7X_LANES):
        ls = slice(hp * V7X_LANES, (hp + 1) * V7X_LANES)
        rows = slice(2 * hp * n_q, 2 * (hp + 1) * n_q)
        pv = _dot(p_in[rows, :], jnp.concatenate([vcat[k0:k0 + n_k, ls], ones], axis=1))
        o2 = pv[:, :V7X_LANES] * (1.0 / pv[:, V7X_LANES:])
        o_ref[r0:r0 + n_q, ls] = jnp.where(low, o2[:n_q], o2[n_q:]).astype(o_ref.dtype)


def _build_rel_bias(table_ref, bias_ref, n_q, n_k, offset):
    n_heads = table_ref.shape[0]
    clip = lambda d: min(max(d, -(CHUNK - 1)), REL_MAX) + (CHUNK - 1)
    heads_per_pass = 4
    for j0 in range(0, n_k, V7X_LANES):
        jw = min(V7X_LANES, n_k - j0)
        r_lo, r_hi = clip(offset - (j0 + jw - 1)), clip(offset + n_q - 1 - j0)
        d = (offset - j0 + lax.broadcasted_iota(jnp.int32, (n_q, jw), 0)
             - lax.broadcasted_iota(jnp.int32, (n_q, jw), 1))
        idx = jnp.clip(d, -(CHUNK - 1), REL_MAX) + (CHUNK - 1)
        for h0 in range(0, n_heads, heads_per_pass):
            heads = range(h0, min(h0 + heads_per_pass, n_heads))
            if r_lo == r_hi:
                vals = [jnp.full((n_q, jw), table_ref[h, r_lo], F32) for h in heads]
            else:
                def pick(r, acc, heads=heads, idx=idx):
                    hit = idx == r
                    return tuple(jnp.where(hit, table_ref[h, r], a) for h, a in zip(heads, acc))
                vals = lax.fori_loop(r_lo, r_hi + 1, pick, tuple(jnp.zeros((n_q, jw), F32) for _ in heads))
            for h, v in zip(heads, vals):
                bias_ref[h * n_q:(h + 1) * n_q, j0:j0 + jw] = v * LOG2E


def _attn_prompt_kernel(qb_rows, table_ref, q_ref, kp_ref, kc_ref, vp_ref, vc_ref, o_ref,
                        kcat, vcat, bias_ref, s_buf, p_buf):
    i = pl.program_id(1)
    band = (A_BACK + 1) * CHUNK
    past = A_BACK * CHUNK
    n_chunks = qb_rows // CHUNK

    @pl.when((pl.program_id(0) == 0) & (i == 0))
    def _():
        _build_rel_bias(table_ref, bias_ref, CHUNK, band, past)

    def cast_rows(j, carry):
        r = pl.multiple_of(j * CHUNK, CHUNK)
        kcat[pl.ds(r, CHUNK), :] = kp_ref[pl.ds(r, CHUNK), :].astype(BF16)
        kcat[pl.ds(qb_rows + r, CHUNK), :] = kc_ref[pl.ds(r, CHUNK), :].astype(BF16)
        vcat[pl.ds(r, CHUNK), :] = vp_ref[pl.ds(r, CHUNK), :].astype(BF16)
        vcat[pl.ds(qb_rows + r, CHUNK), :] = vc_ref[pl.ds(r, CHUNK), :].astype(BF16)
        return carry

    lax.fori_loop(0, n_chunks, cast_rows, 0)
    key_idx = lax.broadcasted_iota(jnp.int32, (1, band), 1)

    _qk_scores(q_ref, kcat, s_buf.at[0], 0, CHUNK, 0, band)
    for c in range(n_chunks + 1):
        if c + 1 < n_chunks:
            _qk_scores(q_ref, kcat, s_buf.at[(c + 1) % 2], (c + 1) * CHUNK, CHUNK, (c + 1) * CHUNK, band)
        if c < n_chunks:
            key_valid = key_idx >= past - c * CHUNK - i * qb_rows
            _softmax_weights(s_buf.at[c % 2], bias_ref, key_valid, p_buf.at[c % 2])
        if c >= 1:
            _weighted_values(p_buf.at[(c - 1) % 2], vcat, o_ref,
                             (c - 1) * CHUNK, CHUNK, (c - 1) * CHUNK, band)


def _attn_prompt_call(z, table, batch, seq, a_width):
    qb_rows = A_BACK * CHUNK
    assert seq % qb_rows == 0 and a_width % V7X_LANES == 0
    nq = seq // qb_rows
    band = (A_BACK + 1) * CHUNK
    rows = A_HEADS * CHUNK
    cur = lambda col: (lambda b, i: (b * nq + i, col))
    prev = lambda col: (lambda b, i: (b * nq + jnp.maximum(i - 1, 0), col))
    blk = (qb_rows, a_width)
    return pl.pallas_call(
        functools.partial(_attn_prompt_kernel, qb_rows),
        grid=(batch, nq),
        in_specs=[pl.BlockSpec(memory_space=pltpu.SMEM),
                  pl.BlockSpec(blk, cur(0)), pl.BlockSpec(blk, prev(1)), pl.BlockSpec(blk, cur(1)),
                  pl.BlockSpec(blk, prev(2)), pl.BlockSpec(blk, cur(2))],
        out_specs=pl.BlockSpec(blk, lambda b, i: (b * nq + i, 0)),
        out_shape=jax.ShapeDtypeStruct((batch * seq, a_width), BF16),
        scratch_shapes=[pltpu.VMEM((2 * qb_rows, a_width), BF16), pltpu.VMEM((2 * qb_rows, a_width), BF16),
                        pltpu.VMEM((rows, band), F32), pltpu.VMEM((2, rows, band), F32),
                        pltpu.VMEM((2, rows, band), BF16)],
        compiler_params=pltpu.CompilerParams(dimension_semantics=("arbitrary", "arbitrary"),
                                             vmem_limit_bytes=V7X_VMEM_LIMIT),
        name="band_attention_prompt",
    )(table, z, z, z, z, z)


def _attn_sample_kernel(n_seq, table_ref, q_ref, k_ref, v_ref, ck_ref, cv_ref, o_ref, kcat, vcat, bias_ref,
                        s_buf, p_buf):
    lc = ck_ref.shape[1]
    t = q_ref.shape[0] // n_seq
    n_k = lc + t

    @pl.when(pl.program_id(0) == 0)
    def _():
        _build_rel_bias(table_ref, bias_ref, t, n_k, lc)

    for s in range(n_seq):
        new = slice(s * t, (s + 1) * t)
        kcat[s, 0:lc, :] = ck_ref[s].astype(BF16)
        kcat[s, lc:, :] = k_ref[new, :].astype(BF16)
        vcat[s, 0:lc, :] = cv_ref[s].astype(BF16)
        vcat[s, lc:, :] = v_ref[new, :].astype(BF16)

    _qk_scores(q_ref, kcat.at[0], s_buf.at[0], 0, t, 0, n_k)
    for s in range(n_seq + 1):
        if s + 1 < n_seq:
            _qk_scores(q_ref, kcat.at[s + 1], s_buf.at[s + 1], (s + 1) * t, t, 0, n_k)
        if s < n_seq:
            _softmax_weights(s_buf.at[s], bias_ref, None, p_buf.at[s])
        if s >= 1:
            _weighted_values(p_buf.at[s - 1], vcat.at[s - 1], o_ref, (s - 1) * t, t, 0, n_k)


def _attn_sample_call(z, cache_k, cache_v, table, layer, batch, t, a_width):
    lc = cache_k.shape[2]
    n_seq = ATTN_STEP_SEQS if batch % ATTN_STEP_SEQS == 0 else 1
    rows = A_HEADS * t
    blk = (n_seq * t, a_width)
    cblk = (None, n_seq, lc, a_width)
    cidx = lambda b: (layer, b, 0, 0)
    return pl.pallas_call(
        functools.partial(_attn_sample_kernel, n_seq),
        grid=(batch // n_seq,),
        in_specs=[pl.BlockSpec(memory_space=pltpu.SMEM),
                  pl.BlockSpec(blk, lambda b: (b, 0)), pl.BlockSpec(blk, lambda b: (b, 1)),
                  pl.BlockSpec(blk, lambda b: (b, 2)),
                  pl.BlockSpec(cblk, cidx), pl.BlockSpec(cblk, cidx)],
        out_specs=pl.BlockSpec(blk, lambda b: (b, 0)),
        out_shape=jax.ShapeDtypeStruct((batch * t, a_width), BF16),
        scratch_shapes=[pltpu.VMEM((n_seq, lc + t, a_width), BF16), pltpu.VMEM((n_seq, lc + t, a_width), BF16),
                        pltpu.VMEM((rows, lc + t), F32), pltpu.VMEM((n_seq, rows, lc + t), F32),
                        pltpu.VMEM((n_seq, rows, lc + t), BF16)],
        compiler_params=pltpu.CompilerParams(dimension_semantics=("arbitrary",),
                                             vmem_limit_bytes=V7X_VMEM_LIMIT),
        name="band_attention_sample",
    )(table, z, z, z, cache_k, cache_v)


def _split2(x):
    a = x.astype(BF16)
    return a, (x - a.astype(F32)).astype(BF16)


_RecurRefs = collections.namedtuple(
    "_RecurRefs", "u q f i g lbraw normw poolw pscale out state hist dmask inter rows_b rows_k")


def _build_decay_masks(dmask, blk, tm):
    sb = min(SUB_BLOCK, blk)
    ri = lax.broadcasted_iota(jnp.int32, (tm, tm), 0)
    ci = lax.broadcasted_iota(jnp.int32, (tm, tm), 1)
    same_sb = _div(ri, sb) == _div(ci, sb)
    dmask[0] = jnp.where(same_sb & (ci <= ri), 1.0, 0.0).astype(BF16)
    dmask[1] = jnp.where(same_sb, 1.0, 0.0).astype(BF16)


def _recur_tile(layer, blk, pos0, n_heads, R, rows, t_tile, fallbacks):
    tm = rows.stop - rows.start
    width = R.u.shape[1]
    hd = width // n_heads
    sb = min(SUB_BLOCK, blk)
    assert blk // sb in (1, 2) and tm % blk == 0

    raw = R.lbraw[...]
    sm = jnp.exp(raw - jnp.max(raw, axis=0, keepdims=True))
    sm = sm / jnp.sum(sm, axis=0, keepdims=True)
    lb = jnp.zeros((1, width), F32)
    for j in range(1, layer + 1):
        lb = lb + sm[j:j + 1, :]
    forget = lb + (1.0 - lb) * _sigmoid(R.f[rows, :])
    log_f = jnp.log(forget)
    k_in = 1.0 - forget
    qx = R.q[rows, :]
    qf = qx * _sigmoid(qx)
    vb = R.i[rows, :].astype(BF16)

    lf2 = jnp.concatenate(_split2(log_f), axis=1)

    def decay_sum(mask01):
        r = _dot(mask01, lf2)
        return r[:, 0:width] + r[:, width:]

    b_rel = decay_sum(R.dmask[0])
    sb_tot = decay_sum(R.dmask[1])
    yield
    if blk > sb:
        second = (lax.broadcasted_iota(jnp.int32, (tm, 1), 0) & (blk - 1)) >= sb
        prev_tot = pltpu.roll(sb_tot, sb, 0)
        next_tot = pltpu.roll(sb_tot, tm - sb, 0)
        b = b_rel + jnp.where(second, prev_tot, 0.0)
        blk_tot = sb_tot + jnp.where(second, prev_tot, next_tot)
    else:
        b, blk_tot = b_rel, sb_tot

    ext = jnp.concatenate([R.hist[...], R.u[rows, :]], axis=0)
    R.hist[...] = ext[tm:, :]
    s2 = ext + pltpu.roll(ext, 1, 0)
    s4 = s2 + pltpu.roll(s2, 2, 0)
    s8 = s4 + pltpu.roll(s4, 4, 0)
    s16 = s8 + pltpu.roll(s8, 8, 0)
    lane = lax.broadcasted_iota(jnp.int32, (1, width), 1)
    grp = _div(lane, width // len(POOL_WINDOWS))
    wsum = jnp.where(grp == 0, s2, jnp.where(grp == 1, s4, jnp.where(grp == 2, s8, s16)))
    wlen = jnp.where(grp == 0, 2.0, jnp.where(grp == 1, 4.0, jnp.where(grp == 2, 8.0, 16.0)))
    row = lax.broadcasted_iota(jnp.int32, (POOL_PAD + tm, 1), 0)
    pos = (row + (pos0 - POOL_PAD) + t_tile * tm).astype(F32)
    cnt = jnp.maximum(jnp.minimum(pos + 1.0, wlen), 1.0)
    dev = (wsum / cnt - ext)[POOL_PAD:, :]
    o_pool = _dot(dev.astype(BF16), R.poolw[...]) * R.pscale[...]
    R.out[rows, 0:width] = o_pool.astype(R.out.dtype)

    q_rel = qf * jnp.exp(b_rel)
    k_rel = (k_in * jnp.exp(-b_rel)).astype(BF16)
    k_end = (k_in * jnp.exp(sb_tot - b_rel)).astype(BF16)
    q_abs = (qf * jnp.exp(b)).astype(BF16)
    k_tail = (k_in * jnp.exp(blk_tot - b)).astype(BF16)

    hb = n_heads * blk
    n_chunks = tm // blk
    stack_head = _div(lax.broadcasted_iota(jnp.int32, (hb, 1), 0), blk) == _div(lane, hd)
    tq = lax.broadcasted_iota(jnp.int32, (hb, blk), 0) & (blk - 1)
    ts = lax.broadcasted_iota(jnp.int32, (hb, blk), 1)
    m_intra = (_div(tq, sb) == _div(ts, sb)) & (ts <= tq)
    m_cross = _div(ts, sb) < _div(tq, sb)
    bd = (_div(lax.broadcasted_iota(jnp.int32, (width, width), 0), hd)
          == _div(lax.broadcasted_iota(jnp.int32, (width, width), 1), hd))
    ones_bd = jnp.where(bd, 1.0, 0.0).astype(BF16)
    gx = R.g[rows, :]
    out_gate = R.normw[...] * (gx * _sigmoid(gx))

    def write_output(o):
        sq_hi, sq_lo = _split2(o * o)
        ms = (_dot(sq_hi, ones_bd) + _dot(sq_lo, ones_bd)) * (1.0 / hd)
        R.out[rows, width:] = (o * lax.rsqrt(ms + RMS_EPS) * out_gate).astype(R.out.dtype)

    chunks = [slice(c * blk, (c + 1) * blk) for c in range(n_chunks)]
    state_in = [jnp.where(bd, _dot_tn(vb[rs], k_tail[rs]), 0.0) for rs in chunks]
    yield
    two_sets = blk > sb
    if two_sets:
        tq2 = lax.broadcasted_iota(jnp.int32, (hb, 2 * blk), 0) & (blk - 1)
        col = lax.broadcasted_iota(jnp.int32, (hb, 2 * blk), 1)
        ts2 = col & (blk - 1)
        keep = (((col < blk) & (_div(tq2, sb) == _div(ts2, sb)) & (ts2 <= tq2))
                | ((col >= blk) & (_div(ts2, sb) < _div(tq2, sb))))
    else:
        keep = m_intra
    scores = []
    for rs in chunks:
        lhs = jnp.where(stack_head, jnp.concatenate([q_rel[rs]] * n_heads, axis=0), 0.0).astype(BF16)
        keys = jnp.concatenate([k_rel[rs], k_end[rs]], axis=0) if two_sets else k_rel[rs]
        scores.append(jnp.where(keep, _dot_nt(lhs, keys), 0.0).astype(BF16))
    yield
    outs = []
    st = R.state[...]
    for c, rs in enumerate(chunks):
        from_state = _dot_nt(q_abs[rs], st.astype(BF16))
        values = jnp.concatenate([vb[rs], vb[rs]], axis=0) if two_sets else vb[rs]
        stacked = jnp.where(stack_head, _dot(scores[c], values), 0.0)
        o_c = stacked[0:blk]
        for h in range(1, n_heads):
            o_c = o_c + stacked[h * blk:(h + 1) * blk]
        R.inter[pl.ds(rows.start + c * blk, blk), :] = from_state
        outs.append(o_c + from_state)
        decay_end = jnp.exp(blk_tot[c * blk:c * blk + 1, :])
        st = st * decay_end + state_in[c]
        yield
    R.state[...] = st
    write_output(outs[0] if len(outs) == 1 else jnp.concatenate(outs, axis=0))

    def fallback():
        @pl.when(jnp.min(sb_tot) <= DECAY_SAFE_LOG)
        def _():
            R.rows_b[...] = b
            R.rows_k[...] = k_in
            t_row = lax.broadcasted_iota(jnp.int32, (blk, 1), 0)

            def add_key_row(s, acc):
                parts = []
                for c, rs in enumerate(chunks):
                    r = c * blk + s
                    rel = jnp.minimum(b[rs] - R.rows_b[pl.ds(r, 1), :], 0.0)
                    term = jnp.where(t_row >= s, qf[rs] * jnp.exp(rel) * R.rows_k[pl.ds(r, 1), :], 0.0)
                    parts.append(_dot(term.astype(BF16), ones_bd) * R.i[pl.ds(rows.start + r, 1), :])
                return acc + (parts[0] if n_chunks == 1 else jnp.concatenate(parts, axis=0))

            write_output(lax.fori_loop(0, blk, add_key_row, R.inter[rows, :]))

    fallbacks.append(fallback)
    yield


def _run_staggered(chains, start_rounds):
    live = list(zip(start_rounds, chains))
    rnd = 0
    while live:
        for item in list(live):
            start, ch = item
            if rnd >= start:
                try:
                    next(ch)
                except StopIteration:
                    live.remove(item)
        rnd += 1


def _recur_kernel(layer, blk, pos0, n_heads, rtm, n_seq, u_ref, q_ref, f_ref, i_ref, g_ref, hist0_ref, st0_ref,
                  lbraw_ref, normw_ref, poolw_ref, pscale_ref, o_ref, st_ref, state, hist, dmask, inter,
                  rows_b, rows_k):
    t_idx = pl.program_id(1)
    seq_rows = u_ref.shape[0] // n_seq
    tiles = seq_rows // rtm

    @pl.when(t_idx == 0)
    def _():
        state[...] = st0_ref[...]
        hist[...] = hist0_ref[...]
        _build_decay_masks(dmask, blk, rtm)

    fallbacks, chains, starts = [], [], []
    for s in range(n_seq):
        R = _RecurRefs(u_ref, q_ref, f_ref, i_ref, g_ref, lbraw_ref, normw_ref, poolw_ref, pscale_ref, o_ref,
                       state.at[s], hist.at[s], dmask, inter, rows_b, rows_k)
        for k in range(tiles):
            r0 = s * seq_rows + k * rtm
            chains.append(_recur_tile(layer, blk, pos0, n_heads, R, slice(r0, r0 + rtm), t_idx * tiles + k, fallbacks))
            starts.append(k * (rtm // blk))
    _run_staggered(chains, starts)
    for fb in fallbacks:
        fb()

    @pl.when(t_idx == pl.num_programs(1) - 1)
    def _():
        st_ref[...] = state[...]


RECUR_TILE = 256
RECUR_STEP_ROWS = 1024
RECUR_STEP_SEQS = 8


def _recur_scratch(n_seq, rows, tm, width):
    return [pltpu.VMEM((n_seq, width, width), F32), pltpu.VMEM((n_seq, POOL_PAD, width), F32),
            pltpu.VMEM((2, tm, tm), BF16), pltpu.VMEM((rows, width), F32), pltpu.VMEM((tm, width), F32),
            pltpu.VMEM((tm, width), F32)]


def _recur_call(z, hist0, st0, lbraw, normw, poolw, pscale, layer, state_layer, batch, seq, blk, pos0, col0):
    width = st0.shape[-1]
    rtm = min(RECUR_TILE, seq)
    tm = min(RECUR_STEP_ROWS, seq)
    n_seq = 1 if tm < seq else min(RECUR_STEP_SEQS, max(1, RECUR_STEP_ROWS // seq), batch)
    assert seq % tm == 0 and tm % rtm == 0 and rtm % blk == 0 and batch % n_seq == 0
    nt = seq // tm
    col = lambda c: (lambda b, t: (b * nt + t, c))
    per_batch = lambda b, t: (state_layer, b, 0, 0)
    consts = (normw, poolw, pscale)
    return pl.pallas_call(
        functools.partial(_recur_kernel, layer, blk, pos0, C_HEADS, rtm, n_seq),
        grid=(batch // n_seq, nt),
        in_specs=[pl.BlockSpec((n_seq * tm, width), col(col0 + j)) for j in range(5)]
        + [pl.BlockSpec((None, n_seq, POOL_PAD, width), per_batch),
           pl.BlockSpec((None, n_seq, width, width), per_batch),
           pl.BlockSpec(lbraw.shape, lambda b, t: (0, 0), pipeline_mode=pl.Buffered(1))]
        + [_layer_spec(c, layer) for c in consts],
        out_specs=[pl.BlockSpec((n_seq * tm, 2 * width), lambda b, t: (b * nt + t, 0)),
                   pl.BlockSpec((n_seq, width, width), lambda b, t: (b, 0, 0))],
        out_shape=[jax.ShapeDtypeStruct((batch * seq, 2 * width), BF16),
                   jax.ShapeDtypeStruct((batch, width, width), F32)],
        scratch_shapes=_recur_scratch(n_seq, n_seq * tm, rtm, width),
        compiler_params=pltpu.CompilerParams(dimension_semantics=("parallel", "arbitrary"),
                                             vmem_limit_bytes=V7X_VMEM_LIMIT),
        name="pool_hgrn_mixer",
    )(z, z, z, z, z, hist0, st0, lbraw, *consts)


def _block_diag(blocks):
    g = blocks.shape[-3]
    zero = jnp.zeros_like(blocks[..., 0, :, :])
    rows = [jnp.concatenate([blocks[..., h, :, :] if j == h else zero for j in range(g)], axis=-1)
            for h in range(g)]
    return jnp.concatenate(rows, axis=-2)


def _state_from_kernel(st, n_heads):
    b, w, _ = st.shape
    hd = w // n_heads
    blocks = jnp.stack([st[:, h * hd:(h + 1) * hd, h * hd:(h + 1) * hd] for h in range(n_heads)], axis=1)
    return jnp.swapaxes(blocks, -1, -2)


def kernel(x_prompt, x_sample, p_prompt, p_sample, cache_attn_k, cache_attn_v, state_pool, state_hgrn, ffn1_w_gu, ffn1_w_down, w_in, attn_rel_bias, pool_w, pool_scale, hgrn_lower_bounds, hgrn_norm_w, w_out, ffn2_w_gu, ffn2_w_down, ple_w_gate, ple_w_proj, ln_g, ln_b):
    depth = w_in.shape[0]
    alpha = float((2 * depth) ** 0.25)
    nb, seq, d = x_prompt.shape
    db, dt, _ = x_sample.shape
    a_width = A_HEADS * HEAD_DIM
    c_width = C_HEADS * HEAD_DIM
    in_width = w_in.shape[-1]
    col0 = 3 * a_width // c_width
    past_rows = min(A_BACK * CHUNK, seq)
    assert dt >= POOL_HIST and seq >= POOL_HIST and PAST_LEN >= POOL_HIST

    bf = lambda w: w.astype(BF16)
    w1gu, w1dn, win, wout = bf(ffn1_w_gu), bf(ffn1_w_down), bf(w_in), bf(w_out)
    w2gu, w2dn, wg, wp = bf(ffn2_w_gu), bf(ffn2_w_down), bf(ple_w_gate), bf(ple_w_proj)
    poolw = bf(_block_diag(pool_w))
    pscale = pool_scale.astype(F32)[:, None, :]
    normw = jnp.tile(hgrn_norm_w.astype(F32), (1, C_HEADS))[:, None, :]
    lbraw = hgrn_lower_bounds.astype(F32)
    pp = p_prompt.reshape(depth, nb * seq, -1)
    ps = p_sample.reshape(depth, db * dt, -1)

    zero_hist = jnp.zeros((1, nb, POOL_PAD, c_width), F32)
    zero_state = jnp.zeros((1, nb, c_width, c_width), F32)
    hist_s = jnp.pad(state_pool.astype(F32), ((0, 0), (0, 0), (POOL_PAD - POOL_HIST, 0), (0, 0)))
    state_s = _block_diag(jnp.swapaxes(state_hgrn.astype(F32), -1, -2))

    lc = cache_attn_k.shape[2]
    cache_k = cache_attn_k.reshape(depth, db, lc, a_width)
    cache_v = cache_attn_v.reshape(depth, db, lc, a_width)

    xp = x_prompt.reshape(nb * seq, d)
    xs = x_sample.reshape(db * dt, d)
    outs = [[] for _ in range(8)]

    for i in range(depth):
        rel_table = attn_rel_bias[i].astype(F32)

        xp, zp = _pre_call(xp, w1gu, w1dn, ln_g, ln_b, win, i, alpha)
        oa = _attn_prompt_call(zp, rel_table, nb, seq, a_width)
        obc, st = _recur_call(zp, zero_hist, zero_state, lbraw, normw, poolw, pscale, i, 0, nb, seq, CHUNK, 0, col0)
        xp = _post_call(xp, oa, obc, pp, wout, w2gu, w2dn, wg, wp, ln_g, ln_b, i, alpha)
        zp3 = zp.reshape(nb, seq, in_width)
        outs[0].append(zp3[:, seq - past_rows:, a_width:2 * a_width].reshape(nb, past_rows, A_HEADS, HEAD_DIM))
        outs[1].append(zp3[:, seq - past_rows:, 2 * a_width:3 * a_width].reshape(nb, past_rows, A_HEADS, HEAD_DIM))
        outs[2].append(zp3[:, seq - POOL_HIST:, 3 * a_width:3 * a_width + c_width])
        outs[3].append(_state_from_kernel(st, C_HEADS))

        xs, zs = _pre_call(xs, w1gu, w1dn, ln_g, ln_b, win, i, alpha)
        oa = _attn_sample_call(zs, cache_k, cache_v, rel_table, i, db, dt, a_width)
        obc, st = _recur_call(zs, hist_s, state_s, lbraw, normw, poolw, pscale, i, i, db, dt, dt, PAST_LEN, col0)
        xs = _post_call(xs, oa, obc, ps, wout, w2gu, w2dn, wg, wp, ln_g, ln_b, i, alpha)
        zs3 = zs.reshape(db, dt, in_width)
        outs[4].append(zs3[:, :, a_width:2 * a_width].reshape(db, dt, A_HEADS, HEAD_DIM))
        outs[5].append(zs3[:, :, 2 * a_width:3 * a_width].reshape(db, dt, A_HEADS, HEAD_DIM))
        outs[6].append(zs3[:, dt - POOL_HIST:, 3 * a_width:3 * a_width + c_width])
        outs[7].append(_state_from_kernel(st, C_HEADS))

    stacked = [jnp.stack(o, axis=0) for o in outs]
    return (xp.reshape(nb, seq, d), xs.reshape(db, dt, d), *stacked)
```

```python
import collections
import functools

import jax
import jax.numpy as jnp
from jax import lax
from jax.experimental import pallas as pl
from jax.experimental.pallas import tpu as pltpu

F32 = jnp.float32
BF16 = jnp.bfloat16

PAST_LEN = 2048
CHUNK = 64
A_BACK = 8
HEAD_DIM = 64
A_HEADS = 8
C_HEADS = 4
REL_MAX = 128
POOL_WINDOWS = (2, 4, 8, 16)
POOL_HIST = max(POOL_WINDOWS) - 1
POOL_PAD = POOL_HIST + 1
SUB_BLOCK = 32
DECAY_SAFE_LOG = -60.0
TOKEN_SUB_ROWS = 256
ATTN_STEP_SEQS = 4
ATTN_Q_BLOCK_PASTS = 2
LN_EPS = 1e-5
RMS_EPS = 1e-6
NEG_INF = -1e30
LOG2E = 1.4426950408889634

V7X_LANES = 128
V7X_MXU_WIDTH = 256
V7X_VMEM_LIMIT = 56 * 1024 * 1024


def _dot(a, b):
    return jnp.dot(a, b, preferred_element_type=F32)


def _dot_nt(a, b):
    return lax.dot_general(a, b, (((1,), (1,)), ((), ())), preferred_element_type=F32)


def _dot_tn(a, b):
    return lax.dot_general(a, b, (((0,), (0,)), ((), ())), preferred_element_type=F32)


def _layer_norm(y, g, b):
    mu = jnp.mean(y, axis=-1, keepdims=True)
    d = y - mu
    var = jnp.mean(d * d, axis=-1, keepdims=True)
    return d * lax.rsqrt(var + LN_EPS) * g + b


def _sigmoid(x):
    return 1.0 / (1.0 + jnp.exp(-x))


def _log2(n):
    assert n > 0 and n & (n - 1) == 0, n
    return n.bit_length() - 1


def _div(x, n):
    return x >> _log2(n)


def _swiglu_stages(xb, wgu_ref, wdn_ref, n_split):
    hidden = wdn_ref.shape[0]
    groups = _hidden_groups(hidden, n_split)
    gate_up = []
    for lo, hi in groups:
        gate_up.append((_dot(xb, wgu_ref[:, lo:hi]), _dot(xb, wgu_ref[:, hidden + lo:hidden + hi])))
        yield
    acc = None
    for (lo, hi), (gate, up) in zip(groups, gate_up):
        act = (gate * _sigmoid(gate) * up).astype(BF16)
        part = _dot(act, wdn_ref[lo:hi, :])
        acc = part if acc is None else acc + part
        yield
    return acc


def _hidden_groups(hidden, n_split):
    if hidden % V7X_MXU_WIDTH:
        return [(0, hidden)]
    tiles = hidden // V7X_MXU_WIDTH
    bounds = [V7X_MXU_WIDTH * ((tiles * g + n_split - 1) // n_split) for g in range(n_split + 1)]
    return [(lo, hi) for lo, hi in zip(bounds[:-1], bounds[1:]) if hi > lo]


def _run_interleaved(chains):
    live = list(chains)
    while live:
        for ch in list(live):
            if ch not in live:
                continue
            try:
                next(ch)
            except StopIteration:
                live = [c for c in live if c is not ch]


def _sub_tiles(n_rows):
    sub = min(TOKEN_SUB_ROWS, n_rows)
    return [slice(r, r + sub) for r in range(0, n_rows, sub)]


def _pre_kernel(alpha, n_split, x_ref, wgu_ref, wdn_ref, lng_ref, lnb_ref, win_ref, xo_ref, z_ref):
    def chain(rows):
        x = x_ref[rows, :]
        ffn = yield from _swiglu_stages(x.astype(BF16), wgu_ref, wdn_ref, n_split)
        xn = _layer_norm(alpha * x + 0.5 * ffn, lng_ref[0:1, :], lnb_ref[0:1, :])
        xo_ref[rows, :] = xn
        z_ref[rows, :] = _dot(xn.astype(BF16), win_ref[...])
        yield

    _run_interleaved([chain(rows) for rows in _sub_tiles(x_ref.shape[0])])


def _post_chain(alpha, n_split, rows, x_ref, oa_ref, obc_ref, p_ref, wout_ref, wgu_ref, wdn_ref, wg_ref, wp_ref,
                lng_ref, lnb_ref, xo_ref):
    wa = oa_ref.shape[1]
    mixed = _dot(oa_ref[rows, :], wout_ref[0:wa, :]) + _dot(obc_ref[rows, :], wout_ref[wa:, :])
    proj = _dot(p_ref[rows, :].astype(BF16), wp_ref[...])
    yield
    x = _layer_norm(alpha * x_ref[rows, :] + mixed, lng_ref[1:2, :], lnb_ref[1:2, :])
    ffn = yield from _swiglu_stages(x.astype(BF16), wgu_ref, wdn_ref, n_split)
    x = _layer_norm(alpha * x + 0.5 * ffn, lng_ref[2:3, :], lnb_ref[2:3, :])
    emb = _sigmoid(_dot(x.astype(BF16), wg_ref[...])) * proj
    yield
    xo_ref[rows, :] = _layer_norm(alpha * x + emb, lng_ref[3:4, :], lnb_ref[3:4, :])


def _post_kernel(alpha, n_split, x_ref, oa_ref, obc_ref, p_ref, wout_ref, wgu_ref, wdn_ref, wg_ref, wp_ref,
                 lng_ref, lnb_ref, xo_ref):
    _run_interleaved([_post_chain(alpha, n_split, rows, x_ref, oa_ref, obc_ref, p_ref, wout_ref, wgu_ref, wdn_ref,
                                  wg_ref, wp_ref, lng_ref, lnb_ref, xo_ref) for rows in _sub_tiles(x_ref.shape[0])])


def _layer_spec(stacked, layer):
    rest = stacked.shape[1:]
    return pl.BlockSpec((None,) + rest, lambda *_: (layer,) + (0,) * len(rest), pipeline_mode=pl.Buffered(1))


def _token_tile(n):
    for tm in (2 * TOKEN_SUB_ROWS, TOKEN_SUB_ROWS, 128, 64, 32, 16, 8):
        if n % tm == 0:
            return tm
    raise ValueError(f"unsupported token count {n}")


def _pre_call(x, wgu, wdn, lng, lnb, win, layer, alpha):
    n, d = x.shape
    zw = win.shape[-1]
    tm = _token_tile(n)
    row = lambda i: (i, 0)
    consts = (wgu, wdn, lng, lnb, win)
    return pl.pallas_call(
        functools.partial(_pre_kernel, alpha, 2),
        grid=(n // tm,),
        in_specs=[pl.BlockSpec((tm, d), row)] + [_layer_spec(c, layer) for c in consts],
        out_specs=[pl.BlockSpec((tm, d), row), pl.BlockSpec((tm, zw), row)],
        out_shape=[jax.ShapeDtypeStruct((n, d), F32), jax.ShapeDtypeStruct((n, zw), F32)],
        compiler_params=pltpu.CompilerParams(dimension_semantics=("parallel",),
                                             vmem_limit_bytes=V7X_VMEM_LIMIT),
        name="pre_mixer_tokens",
    )(x, *consts)


def _post_call(x, oa, obc, p, wout, wgu, wdn, wg, wp, lng, lnb, layer, alpha):
    n, d = x.shape
    tm = _token_tile(n)
    row = lambda i: (i, 0)
    consts = (wout, wgu, wdn, wg, wp, lng, lnb)
    return pl.pallas_call(
        functools.partial(_post_kernel, alpha, 2),
        grid=(n // tm,),
        in_specs=[pl.BlockSpec((tm, d), row), pl.BlockSpec((tm, oa.shape[1]), row),
                  pl.BlockSpec((tm, obc.shape[1]), row),
                  pl.BlockSpec((None, tm, p.shape[-1]), lambda i: (layer, i, 0))]
        + [_layer_spec(c, layer) for c in consts],
        out_specs=pl.BlockSpec((tm, d), row),
        out_shape=jax.ShapeDtypeStruct((n, d), F32),
        compiler_params=pltpu.CompilerParams(dimension_semantics=("parallel",),
                                             vmem_limit_bytes=V7X_VMEM_LIMIT),
        name="post_mixer_tokens",
    )(x, oa, obc, p, *consts)


def _qk_scores(q_ref, kcat, s_out, r0, n_q, k0, n_k):
    scale = HEAD_DIM ** -0.5 * LOG2E
    low = lax.broadcasted_iota(jnp.int32, (1, V7X_LANES), 1) < HEAD_DIM
    for hp in range(q_ref.shape[1] // V7X_LANES):
        ls = slice(hp * V7X_LANES, (hp + 1) * V7X_LANES)
        q = q_ref[r0:r0 + n_q, ls] * scale
        lhs = jnp.concatenate([jnp.where(low, q, 0.0), jnp.where(low, 0.0, q)], axis=0).astype(BF16)
        s_out[2 * hp * n_q:2 * (hp + 1) * n_q, :] = _dot_nt(lhs, kcat[k0:k0 + n_k, ls])


def _softmax_weights(s_in, bias_ref, key_valid, p_out):
    s = s_in[...] + bias_ref[...]
    if key_valid is not None:
        s = jnp.where(key_valid, s, NEG_INF)
    p_out[...] = jnp.exp2(s - jnp.max(s, axis=-1, keepdims=True)).astype(BF16)


def _weighted_values(p_in, vcat, o_ref, r0, n_q, k0, n_k):
    low = lax.broadcasted_iota(jnp.int32, (1, V7X_LANES), 1) < HEAD_DIM
    ones = jnp.ones((n_k, V7X_LANES), BF16)
    for hp in range(o_ref.shape[1] // V7X_LANES):
        ls = slice(hp * V7X_LANES, (hp + 1) * V7X_LANES)
        rows = slice(2 * hp * n_q, 2 * (hp + 1) * n_q)
        pv = _dot(p_in[rows, :], jnp.concatenate([vcat[k0:k0 + n_k, ls], ones], axis=1))
        o2 = pv[:, :V7X_LANES] * (1.0 / pv[:, V7X_LANES:])
        o_ref[r0:r0 + n_q, ls] = jnp.where(low, o2[:n_q], o2[n_q:]).astype(o_ref.dtype)


def _build_rel_bias(table_ref, bias_ref, n_q, n_k, offset):
    n_heads = table_ref.shape[0]
    clip = lambda d: min(max(d, -(CHUNK - 1)), REL_MAX) + (CHUNK - 1)
    heads_per_pass = 4
    for j0 in range(0, n_k, V7X_LANES):
        jw = min(V7X_LANES, n_k - j0)
        r_lo, r_hi = clip(offset - (j0 + jw - 1)), clip(offset + n_q - 1 - j0)
        d = (offset - j0 + lax.broadcasted_iota(jnp.int32, (n_q, jw), 0)
             - lax.broadcasted_iota(jnp.int32, (n_q, jw), 1))
        idx = jnp.clip(d, -(CHUNK - 1), REL_MAX) + (CHUNK - 1)
        for h0 in range(0, n_heads, heads_per_pass):
            heads = range(h0, min(h0 + heads_per_pass, n_heads))
            if r_lo == r_hi:
                vals = [jnp.full((n_q, jw), table_ref[h, r_lo], F32) for h in heads]
            else:
                def pick(r, acc, heads=heads, idx=idx):
                    hit = idx == r
                    return tuple(jnp.where(hit, table_ref[h, r], a) for h, a in zip(heads, acc))
                vals = lax.fori_loop(r_lo, r_hi + 1, pick, tuple(jnp.zeros((n_q, jw), F32) for _ in heads))
            for h, v in zip(heads, vals):
                bias_ref[h * n_q:(h + 1) * n_q, j0:j0 + jw] = v * LOG2E


def _attn_prompt_kernel(qb_rows, table_ref, q_ref, kp_ref, kc_ref, vp_ref, vc_ref, o_ref,
                        kcat, vcat, bias_ref, s_buf, p_buf):
    i = pl.program_id(1)
    band = (A_BACK + 1) * CHUNK
    past = A_BACK * CHUNK
    n_chunks = qb_rows // CHUNK

    @pl.when((pl.program_id(0) == 0) & (i == 0))
    def _():
        _build_rel_bias(table_ref, bias_ref, CHUNK, band, past)

    def cast_past(j, carry):
        r = pl.multiple_of(j * CHUNK, CHUNK)
        kcat[pl.ds(r, CHUNK), :] = kp_ref[pl.ds(r, CHUNK), :].astype(BF16)
        vcat[pl.ds(r, CHUNK), :] = vp_ref[pl.ds(r, CHUNK), :].astype(BF16)
        return carry

    def cast_own(j, carry):
        r = pl.multiple_of(j * CHUNK, CHUNK)
        kcat[pl.ds(past + r, CHUNK), :] = kc_ref[pl.ds(r, CHUNK), :].astype(BF16)
        vcat[pl.ds(past + r, CHUNK), :] = vc_ref[pl.ds(r, CHUNK), :].astype(BF16)
        return carry

    lax.fori_loop(0, past // CHUNK, cast_past, 0)
    lax.fori_loop(0, n_chunks, cast_own, 0)
    key_idx = lax.broadcasted_iota(jnp.int32, (1, band), 1)

    _qk_scores(q_ref, kcat, s_buf.at[0], 0, CHUNK, 0, band)
    for c in range(n_chunks + 1):
        if c + 1 < n_chunks:
            _qk_scores(q_ref, kcat, s_buf.at[(c + 1) % 2], (c + 1) * CHUNK, CHUNK, (c + 1) * CHUNK, band)
        if c < n_chunks:
            key_valid = key_idx >= past - c * CHUNK - i * qb_rows
            _softmax_weights(s_buf.at[c % 2], bias_ref, key_valid, p_buf.at[c % 2])
        if c >= 1:
            _weighted_values(p_buf.at[(c - 1) % 2], vcat, o_ref,
                             (c - 1) * CHUNK, CHUNK, (c - 1) * CHUNK, band)


def _attn_prompt_call(z, table, batch, seq, a_width):
    past = A_BACK * CHUNK
    qb_rows = next(m * past for m in (ATTN_Q_BLOCK_PASTS, 1) if seq % (m * past) == 0)
    assert a_width % V7X_LANES == 0
    nq = seq // qb_rows
    per_q = qb_rows // past
    band = (A_BACK + 1) * CHUNK
    rows = A_HEADS * CHUNK
    cur = lambda col: (lambda b, i: (b * nq + i, col))
    prev = lambda col: (lambda b, i: (b * nq * per_q + jnp.maximum(i * per_q - 1, 0), col))
    blk = (qb_rows, a_width)
    pblk = (past, a_width)
    return pl.pallas_call(
        functools.partial(_attn_prompt_kernel, qb_rows),
        grid=(batch, nq),
        in_specs=[pl.BlockSpec(memory_space=pltpu.SMEM),
                  pl.BlockSpec(blk, cur(0)), pl.BlockSpec(pblk, prev(1)), pl.BlockSpec(blk, cur(1)),
                  pl.BlockSpec(pblk, prev(2)), pl.BlockSpec(blk, cur(2))],
        out_specs=pl.BlockSpec(blk, lambda b, i: (b * nq + i, 0)),
        out_shape=jax.ShapeDtypeStruct((batch * seq, a_width), BF16),
        scratch_shapes=[pltpu.VMEM((past + qb_rows, a_width), BF16), pltpu.VMEM((past + qb_rows, a_width), BF16),
                        pltpu.VMEM((rows, band), F32), pltpu.VMEM((2, rows, band), F32),
                        pltpu.VMEM((2, rows, band), BF16)],
        compiler_params=pltpu.CompilerParams(dimension_semantics=("arbitrary", "arbitrary"),
                                             vmem_limit_bytes=V7X_VMEM_LIMIT),
        name="band_attention_prompt",
    )(table, z, z, z, z, z)


def _attn_sample_kernel(n_seq, table_ref, q_ref, k_ref, v_ref, ck_ref, cv_ref, o_ref, kcat, vcat, bias_ref,
                        s_buf, p_buf):
    lc = ck_ref.shape[1]
    t = q_ref.shape[0] // n_seq
    n_k = lc + t

    @pl.when(pl.program_id(0) == 0)
    def _():
        _build_rel_bias(table_ref, bias_ref, t, n_k, lc)

    for s in range(n_seq):
        new = slice(s * t, (s + 1) * t)
        kcat[s, 0:lc, :] = ck_ref[s].astype(BF16)
        kcat[s, lc:, :] = k_ref[new, :].astype(BF16)
        vcat[s, 0:lc, :] = cv_ref[s].astype(BF16)
        vcat[s, lc:, :] = v_ref[new, :].astype(BF16)

    _qk_scores(q_ref, kcat.at[0], s_buf.at[0], 0, t, 0, n_k)
    for s in range(n_seq + 1):
        if s + 1 < n_seq:
            _qk_scores(q_ref, kcat.at[s + 1], s_buf.at[s + 1], (s + 1) * t, t, 0, n_k)
        if s < n_seq:
            _softmax_weights(s_buf.at[s], bias_ref, None, p_buf.at[s])
        if s >= 1:
            _weighted_values(p_buf.at[s - 1], vcat.at[s - 1], o_ref, (s - 1) * t, t, 0, n_k)


def _attn_sample_call(z, cache_k, cache_v, table, layer, batch, t, a_width):
    lc = cache_k.shape[2]
    n_seq = ATTN_STEP_SEQS if batch % ATTN_STEP_SEQS == 0 else 1
    rows = A_HEADS * t
    blk = (n_seq * t, a_width)
    cblk = (None, n_seq, lc, a_width)
    cidx = lambda b: (layer, b, 0, 0)
    return pl.pallas_call(
        functools.partial(_attn_sample_kernel, n_seq),
        grid=(batch // n_seq,),
        in_specs=[pl.BlockSpec(memory_space=pltpu.SMEM),
                  pl.BlockSpec(blk, lambda b: (b, 0)), pl.BlockSpec(blk, lambda b: (b, 1)),
                  pl.BlockSpec(blk, lambda b: (b, 2)),
                  pl.BlockSpec(cblk, cidx), pl.BlockSpec(cblk, cidx)],
        out_specs=pl.BlockSpec(blk, lambda b: (b, 0)),
        out_shape=jax.ShapeDtypeStruct((batch * t, a_width), BF16),
        scratch_shapes=[pltpu.VMEM((n_seq, lc + t, a_width), BF16), pltpu.VMEM((n_seq, lc + t, a_width), BF16),
                        pltpu.VMEM((rows, lc + t), F32), pltpu.VMEM((n_seq, rows, lc + t), F32),
                        pltpu.VMEM((n_seq, rows, lc + t), BF16)],
        compiler_params=pltpu.CompilerParams(dimension_semantics=("arbitrary",),
                                             vmem_limit_bytes=V7X_VMEM_LIMIT),
        name="band_attention_sample",
    )(table, z, z, z, cache_k, cache_v)


def _split2(x):
    a = x.astype(BF16)
    return a, (x - a.astype(F32)).astype(BF16)


_RecurRefs = collections.namedtuple(
    "_RecurRefs", "u q f i g lbraw normw poolw pscale out state hist dmask inter rows_b rows_k")


def _build_decay_masks(dmask, blk, tm):
    sb = min(SUB_BLOCK, blk)
    ri = lax.broadcasted_iota(jnp.int32, (tm, tm), 0)
    ci = lax.broadcasted_iota(jnp.int32, (tm, tm), 1)
    same_sb = _div(ri, sb) == _div(ci, sb)
    dmask[0] = jnp.where(same_sb & (ci <= ri), 1.0, 0.0).astype(BF16)
    dmask[1] = jnp.where(same_sb, 1.0, 0.0).astype(BF16)


def _recur_tile(layer, blk, pos0, n_heads, R, rows, t_tile, fallbacks):
    tm = rows.stop - rows.start
    width = R.u.shape[1]
    hd = width // n_heads
    sb = min(SUB_BLOCK, blk)
    assert blk // sb in (1, 2) and tm % blk == 0

    raw = R.lbraw[...]
    sm = jnp.exp(raw - jnp.max(raw, axis=0, keepdims=True))
    sm = sm / jnp.sum(sm, axis=0, keepdims=True)
    lb = jnp.zeros((1, width), F32)
    for j in range(1, layer + 1):
        lb = lb + sm[j:j + 1, :]
    forget = lb + (1.0 - lb) * _sigmoid(R.f[rows, :])
    log_f = jnp.log(forget)
    k_in = 1.0 - forget
    qx = R.q[rows, :]
    qf = qx * _sigmoid(qx)
    vb = R.i[rows, :].astype(BF16)

    lf2 = jnp.concatenate(_split2(log_f), axis=1)

    def decay_sum(mask01):
        r = _dot(mask01, lf2)
        return r[:, 0:width] + r[:, width:]

    b_rel = decay_sum(R.dmask[0])
    sb_tot = decay_sum(R.dmask[1])
    yield
    if blk > sb:
        second = (lax.broadcasted_iota(jnp.int32, (tm, 1), 0) & (blk - 1)) >= sb
        prev_tot = pltpu.roll(sb_tot, sb, 0)
        next_tot = pltpu.roll(sb_tot, tm - sb, 0)
        b = b_rel + jnp.where(second, prev_tot, 0.0)
        blk_tot = sb_tot + jnp.where(second, prev_tot, next_tot)
    else:
        b, blk_tot = b_rel, sb_tot

    ext = jnp.concatenate([R.hist[...], R.u[rows, :]], axis=0)
    R.hist[...] = ext[tm:, :]
    s2 = ext + pltpu.roll(ext, 1, 0)
    s4 = s2 + pltpu.roll(s2, 2, 0)
    s8 = s4 + pltpu.roll(s4, 4, 0)
    s16 = s8 + pltpu.roll(s8, 8, 0)
    lane = lax.broadcasted_iota(jnp.int32, (1, width), 1)
    grp = _div(lane, width // len(POOL_WINDOWS))
    wsum = jnp.where(grp == 0, s2, jnp.where(grp == 1, s4, jnp.where(grp == 2, s8, s16)))
    wlen = jnp.where(grp == 0, 2.0, jnp.where(grp == 1, 4.0, jnp.where(grp == 2, 8.0, 16.0)))
    row = lax.broadcasted_iota(jnp.int32, (POOL_PAD + tm, 1), 0)
    pos = (row + (pos0 - POOL_PAD) + t_tile * tm).astype(F32)
    cnt = jnp.maximum(jnp.minimum(pos + 1.0, wlen), 1.0)
    dev = (wsum / cnt - ext)[POOL_PAD:, :]
    o_pool = _dot(dev.astype(BF16), R.poolw[...]) * R.pscale[...]
    R.out[rows, 0:width] = o_pool.astype(R.out.dtype)

    q_rel = qf * jnp.exp(b_rel)
    k_rel = (k_in * jnp.exp(-b_rel)).astype(BF16)
    k_end = (k_in * jnp.exp(sb_tot - b_rel)).astype(BF16)
    q_abs = (qf * jnp.exp(b)).astype(BF16)
    k_tail = (k_in * jnp.exp(blk_tot - b)).astype(BF16)

    hb = n_heads * blk
    n_chunks = tm // blk
    stack_head = _div(lax.broadcasted_iota(jnp.int32, (hb, 1), 0), blk) == _div(lane, hd)
    tq = lax.broadcasted_iota(jnp.int32, (hb, blk), 0) & (blk - 1)
    ts = lax.broadcasted_iota(jnp.int32, (hb, blk), 1)
    m_intra = (_div(tq, sb) == _div(ts, sb)) & (ts <= tq)
    m_cross = _div(ts, sb) < _div(tq, sb)
    bd = (_div(lax.broadcasted_iota(jnp.int32, (width, width), 0), hd)
          == _div(lax.broadcasted_iota(jnp.int32, (width, width), 1), hd))
    ones_bd = jnp.where(bd, 1.0, 0.0).astype(BF16)
    gx = R.g[rows, :]
    out_gate = R.normw[...] * (gx * _sigmoid(gx))

    def write_output(o):
        sq_hi, sq_lo = _split2(o * o)
        ms = (_dot(sq_hi, ones_bd) + _dot(sq_lo, ones_bd)) * (1.0 / hd)
        R.out[rows, width:] = (o * lax.rsqrt(ms + RMS_EPS) * out_gate).astype(R.out.dtype)

    chunks = [slice(c * blk, (c + 1) * blk) for c in range(n_chunks)]
    state_in = [jnp.where(bd, _dot_tn(vb[rs], k_tail[rs]), 0.0) for rs in chunks]
    yield
    two_sets = blk > sb
    if two_sets:
        tq2 = lax.broadcasted_iota(jnp.int32, (hb, 2 * blk), 0) & (blk - 1)
        col = lax.broadcasted_iota(jnp.int32, (hb, 2 * blk), 1)
        ts2 = col & (blk - 1)
        keep = (((col < blk) & (_div(tq2, sb) == _div(ts2, sb)) & (ts2 <= tq2))
                | ((col >= blk) & (_div(ts2, sb) < _div(tq2, sb))))
    else:
        keep = m_intra
    scores = []
    for rs in chunks:
        lhs = jnp.where(stack_head, jnp.concatenate([q_rel[rs]] * n_heads, axis=0), 0.0).astype(BF16)
        keys = jnp.concatenate([k_rel[rs], k_end[rs]], axis=0) if two_sets else k_rel[rs]
        scores.append(jnp.where(keep, _dot_nt(lhs, keys), 0.0).astype(BF16))
    yield
    outs = []
    st = R.state[...]
    for c, rs in enumerate(chunks):
        from_state = _dot_nt(q_abs[rs], st.astype(BF16))
        values = jnp.concatenate([vb[rs], vb[rs]], axis=0) if two_sets else vb[rs]
        stacked = jnp.where(stack_head, _dot(scores[c], values), 0.0)
        o_c = stacked[0:blk]
        for h in range(1, n_heads):
            o_c = o_c + stacked[h * blk:(h + 1) * blk]
        R.inter[pl.ds(rows.start + c * blk, blk), :] = from_state
        outs.append(o_c + from_state)
        decay_end = jnp.exp(blk_tot[c * blk:c * blk + 1, :])
        st = st * decay_end + state_in[c]
        yield
    R.state[...] = st
    write_output(outs[0] if len(outs) == 1 else jnp.concatenate(outs, axis=0))

    def fallback():
        @pl.when(jnp.min(sb_tot) <= DECAY_SAFE_LOG)
        def _():
            R.rows_b[...] = b
            R.rows_k[...] = k_in
            t_row = lax.broadcasted_iota(jnp.int32, (blk, 1), 0)

            def add_key_row(s, acc):
                parts = []
                for c, rs in enumerate(chunks):
                    r = c * blk + s
                    rel = jnp.minimum(b[rs] - R.rows_b[pl.ds(r, 1), :], 0.0)
                    term = jnp.where(t_row >= s, qf[rs] * jnp.exp(rel) * R.rows_k[pl.ds(r, 1), :], 0.0)
                    parts.append(_dot(term.astype(BF16), ones_bd) * R.i[pl.ds(rows.start + r, 1), :])
                return acc + (parts[0] if n_chunks == 1 else jnp.concatenate(parts, axis=0))

            write_output(lax.fori_loop(0, blk, add_key_row, R.inter[rows, :]))

    fallbacks.append(fallback)
    yield


def _run_staggered(chains, start_rounds):
    live = list(zip(start_rounds, chains))
    rnd = 0
    while live:
        for item in list(live):
            start, ch = item
            if rnd >= start:
                try:
                    next(ch)
                except StopIteration:
                    live.remove(item)
        rnd += 1


def _recur_kernel(layer, blk, pos0, n_heads, rtm, n_seq, u_ref, q_ref, f_ref, i_ref, g_ref, hist0_ref, st0_ref,
                  lbraw_ref, normw_ref, poolw_ref, pscale_ref, o_ref, st_ref, state, hist, dmask, inter,
                  rows_b, rows_k):
    t_idx = pl.program_id(1)
    seq_rows = u_ref.shape[0] // n_seq
    tiles = seq_rows // rtm

    @pl.when(t_idx == 0)
    def _():
        state[...] = st0_ref[...]
        hist[...] = hist0_ref[...]
        _build_decay_masks(dmask, blk, rtm)

    fallbacks, chains, starts = [], [], []
    for s in range(n_seq):
        R = _RecurRefs(u_ref, q_ref, f_ref, i_ref, g_ref, lbraw_ref, normw_ref, poolw_ref, pscale_ref, o_ref,
                       state.at[s], hist.at[s], dmask, inter, rows_b, rows_k)
        for k in range(tiles):
            r0 = s * seq_rows + k * rtm
            chains.append(_recur_tile(layer, blk, pos0, n_heads, R, slice(r0, r0 + rtm), t_idx * tiles + k, fallbacks))
            starts.append(k * (rtm // blk))
    _run_staggered(chains, starts)
    for fb in fallbacks:
        fb()

    @pl.when(t_idx == pl.num_programs(1) - 1)
    def _():
        st_ref[...] = state[...]


RECUR_TILE = 256
RECUR_STEP_ROWS = 1024
RECUR_STEP_SEQS = 8


def _recur_scratch(n_seq, rows, tm, width):
    return [pltpu.VMEM((n_seq, width, width), F32), pltpu.VMEM((n_seq, POOL_PAD, width), F32),
            pltpu.VMEM((2, tm, tm), BF16), pltpu.VMEM((rows, width), F32), pltpu.VMEM((tm, width), F32),
            pltpu.VMEM((tm, width), F32)]


def _recur_call(z, hist0, st0, lbraw, normw, poolw, pscale, layer, state_layer, batch, seq, blk, pos0, col0):
    width = st0.shape[-1]
    rtm = min(RECUR_TILE, seq)
    tm = min(RECUR_STEP_ROWS, seq)
    n_seq = 1 if tm < seq else min(RECUR_STEP_SEQS, max(1, RECUR_STEP_ROWS // seq), batch)
    assert seq % tm == 0 and tm % rtm == 0 and rtm % blk == 0 and batch % n_seq == 0
    nt = seq // tm
    col = lambda c: (lambda b, t: (b * nt + t, c))
    per_batch = lambda b, t: (state_layer, b, 0, 0)
    consts = (normw, poolw, pscale)
    return pl.pallas_call(
        functools.partial(_recur_kernel, layer, blk, pos0, C_HEADS, rtm, n_seq),
        grid=(batch // n_seq, nt),
        in_specs=[pl.BlockSpec((n_seq * tm, width), col(col0 + j)) for j in range(5)]
        + [pl.BlockSpec((None, n_seq, POOL_PAD, width), per_batch),
           pl.BlockSpec((None, n_seq, width, width), per_batch),
           pl.BlockSpec(lbraw.shape, lambda b, t: (0, 0), pipeline_mode=pl.Buffered(1))]
        + [_layer_spec(c, layer) for c in consts],
        out_specs=[pl.BlockSpec((n_seq * tm, 2 * width), lambda b, t: (b * nt + t, 0)),
                   pl.BlockSpec((n_seq, width, width), lambda b, t: (b, 0, 0))],
        out_shape=[jax.ShapeDtypeStruct((batch * seq, 2 * width), BF16),
                   jax.ShapeDtypeStruct((batch, width, width), F32)],
        scratch_shapes=_recur_scratch(n_seq, n_seq * tm, rtm, width),
        compiler_params=pltpu.CompilerParams(dimension_semantics=("parallel", "arbitrary"),
                                             vmem_limit_bytes=V7X_VMEM_LIMIT),
        name="pool_hgrn_mixer",
    )(z, z, z, z, z, hist0, st0, lbraw, *consts)


def _block_diag(blocks):
    g = blocks.shape[-3]
    zero = jnp.zeros_like(blocks[..., 0, :, :])
    rows = [jnp.concatenate([blocks[..., h, :, :] if j == h else zero for j in range(g)], axis=-1)
            for h in range(g)]
    return jnp.concatenate(rows, axis=-2)


def _state_from_kernel(st, n_heads):
    b, w, _ = st.shape
    hd = w // n_heads
    blocks = jnp.stack([st[:, h * hd:(h + 1) * hd, h * hd:(h + 1) * hd] for h in range(n_heads)], axis=1)
    return jnp.swapaxes(blocks, -1, -2)


def kernel(x_prompt, x_sample, p_prompt, p_sample, cache_attn_k, cache_attn_v, state_pool, state_hgrn, ffn1_w_gu, ffn1_w_down, w_in, attn_rel_bias, pool_w, pool_scale, hgrn_lower_bounds, hgrn_norm_w, w_out, ffn2_w_gu, ffn2_w_down, ple_w_gate, ple_w_proj, ln_g, ln_b):
    depth = w_in.shape[0]
    alpha = float((2 * depth) ** 0.25)
    nb, seq, d = x_prompt.shape
    db, dt, _ = x_sample.shape
    a_width = A_HEADS * HEAD_DIM
    c_width = C_HEADS * HEAD_DIM
    in_width = w_in.shape[-1]
    col0 = 3 * a_width // c_width
    past_rows = min(A_BACK * CHUNK, seq)
    assert dt >= POOL_HIST and seq >= POOL_HIST and PAST_LEN >= POOL_HIST

    bf = lambda w: w.astype(BF16)
    w1gu, w1dn, win, wout = bf(ffn1_w_gu), bf(ffn1_w_down), bf(w_in), bf(w_out)
    w2gu, w2dn, wg, wp = bf(ffn2_w_gu), bf(ffn2_w_down), bf(ple_w_gate), bf(ple_w_proj)
    poolw = bf(_block_diag(pool_w))
    pscale = pool_scale.astype(F32)[:, None, :]
    normw = jnp.tile(hgrn_norm_w.astype(F32), (1, C_HEADS))[:, None, :]
    lbraw = hgrn_lower_bounds.astype(F32)
    pp = p_prompt.reshape(depth, nb * seq, -1)
    ps = p_sample.reshape(depth, db * dt, -1)

    zero_hist = jnp.zeros((1, nb, POOL_PAD, c_width), F32)
    zero_state = jnp.zeros((1, nb, c_width, c_width), F32)
    hist_s = jnp.pad(state_pool.astype(F32), ((0, 0), (0, 0), (POOL_PAD - POOL_HIST, 0), (0, 0)))
    state_s = _block_diag(jnp.swapaxes(state_hgrn.astype(F32), -1, -2))

    lc = cache_attn_k.shape[2]
    cache_k = cache_attn_k.reshape(depth, db, lc, a_width)
    cache_v = cache_attn_v.reshape(depth, db, lc, a_width)

    xp = x_prompt.reshape(nb * seq, d)
    xs = x_sample.reshape(db * dt, d)
    outs = [[] for _ in range(8)]

    for i in range(depth):
        rel_table = attn_rel_bias[i].astype(F32)

        xp, zp = _pre_call(xp, w1gu, w1dn, ln_g, ln_b, win, i, alpha)
        oa = _attn_prompt_call(zp, rel_table, nb, seq, a_width)
        obc, st = _recur_call(zp, zero_hist, zero_state, lbraw, normw, poolw, pscale, i, 0, nb, seq, CHUNK, 0, col0)
        xp = _post_call(xp, oa, obc, pp, wout, w2gu, w2dn, wg, wp, ln_g, ln_b, i, alpha)
        zp3 = zp.reshape(nb, seq, in_width)
        outs[0].append(zp3[:, seq - past_rows:, a_width:2 * a_width].reshape(nb, past_rows, A_HEADS, HEAD_DIM))
        outs[1].append(zp3[:, seq - past_rows:, 2 * a_width:3 * a_width].reshape(nb, past_rows, A_HEADS, HEAD_DIM))
        outs[2].append(zp3[:, seq - POOL_HIST:, 3 * a_width:3 * a_width + c_width])
        outs[3].append(_state_from_kernel(st, C_HEADS))

        xs, zs = _pre_call(xs, w1gu, w1dn, ln_g, ln_b, win, i, alpha)
        oa = _attn_sample_call(zs, cache_k, cache_v, rel_table, i, db, dt, a_width)
        obc, st = _recur_call(zs, hist_s, state_s, lbraw, normw, poolw, pscale, i, i, db, dt, dt, PAST_LEN, col0)
        xs = _post_call(xs, oa, obc, ps, wout, w2gu, w2dn, wg, wp, ln_g, ln_b, i, alpha)
        zs3 = zs.reshape(db, dt, in_width)
        outs[4].append(zs3[:, :, a_width:2 * a_width].reshape(db, dt, A_HEADS, HEAD_DIM))
        outs[5].append(zs3[:, :, 2 * a_width:3 * a_width].reshape(db, dt, A_HEADS, HEAD_DIM))
        outs[6].append(zs3[:, dt - POOL_HIST:, 3 * a_width:3 * a_width + c_width])
        outs[7].append(_state_from_kernel(st, C_HEADS))

    stacked = [jnp.stack(o, axis=0) for o in outs]
    return (xp.reshape(nb, seq, d), xs.reshape(db, dt, d), *stacked)
```

```python
import collections
import functools

import jax
import jax.numpy as jnp
from jax import lax
from jax.experimental import pallas as pl
from jax.experimental.pallas import tpu as pltpu

F32 = jnp.float32
BF16 = jnp.bfloat16

PAST_LEN = 2048
CHUNK = 64
A_BACK = 8
HEAD_DIM = 64
A_HEADS = 8
C_HEADS = 4
REL_MAX = 128
POOL_WINDOWS = (2, 4, 8, 16)
POOL_HIST = max(POOL_WINDOWS) - 1
POOL_PAD = POOL_HIST + 1
SUB_BLOCK = 32
DECAY_SAFE_LOG = -60.0
TOKEN_SUB_ROWS = 256
ATTN_STEP_SEQS = 4
ATTN_Q_BLOCK_PASTS = 4
LN_EPS = 1e-5
RMS_EPS = 1e-6
NEG_INF = -1e30
LOG2E = 1.4426950408889634

V7X_LANES = 128
V7X_MXU_WIDTH = 256
V7X_VMEM_LIMIT = 56 * 1024 * 1024


def _dot(a, b):
    return jnp.dot(a, b, preferred_element_type=F32)


def _dot_nt(a, b):
    return lax.dot_general(a, b, (((1,), (1,)), ((), ())), preferred_element_type=F32)


def _dot_tn(a, b):
    return lax.dot_general(a, b, (((0,), (0,)), ((), ())), preferred_element_type=F32)


def _layer_norm(y, g, b):
    mu = jnp.mean(y, axis=-1, keepdims=True)
    d = y - mu
    var = jnp.mean(d * d, axis=-1, keepdims=True)
    return d * lax.rsqrt(var + LN_EPS) * g + b


def _sigmoid(x):
    return 1.0 / (1.0 + jnp.exp(-x))


def _log2(n):
    assert n > 0 and n & (n - 1) == 0, n
    return n.bit_length() - 1


def _div(x, n):
    return x >> _log2(n)


def _swiglu_stages(xb, wgu_ref, wdn_ref, n_split):
    hidden = wdn_ref.shape[0]
    groups = _hidden_groups(hidden, n_split)
    gate_up = []
    for lo, hi in groups:
        gate_up.append((_dot(xb, wgu_ref[:, lo:hi]), _dot(xb, wgu_ref[:, hidden + lo:hidden + hi])))
        yield
    acc = None
    for (lo, hi), (gate, up) in zip(groups, gate_up):
        act = (gate * _sigmoid(gate) * up).astype(BF16)
        part = _dot(act, wdn_ref[lo:hi, :])
        acc = part if acc is None else acc + part
        yield
    return acc


def _hidden_groups(hidden, n_split):
    if hidden % V7X_MXU_WIDTH:
        return [(0, hidden)]
    tiles = hidden // V7X_MXU_WIDTH
    bounds = [V7X_MXU_WIDTH * ((tiles * g + n_split - 1) // n_split) for g in range(n_split + 1)]
    return [(lo, hi) for lo, hi in zip(bounds[:-1], bounds[1:]) if hi > lo]


def _run_interleaved(chains):
    live = list(chains)
    while live:
        for ch in list(live):
            if ch not in live:
                continue
            try:
                next(ch)
            except StopIteration:
                live = [c for c in live if c is not ch]


def _sub_tiles(n_rows):
    sub = min(TOKEN_SUB_ROWS, n_rows)
    return [slice(r, r + sub) for r in range(0, n_rows, sub)]


def _pre_kernel(alpha, n_split, x_ref, wgu_ref, wdn_ref, lng_ref, lnb_ref, win_ref, xo_ref, z_ref):
    def chain(rows):
        x = x_ref[rows, :]
        ffn = yield from _swiglu_stages(x.astype(BF16), wgu_ref, wdn_ref, n_split)
        xn = _layer_norm(alpha * x + 0.5 * ffn, lng_ref[0:1, :], lnb_ref[0:1, :])
        xo_ref[rows, :] = xn
        z_ref[rows, :] = _dot(xn.astype(BF16), win_ref[...])
        yield

    _run_interleaved([chain(rows) for rows in _sub_tiles(x_ref.shape[0])])


def _post_chain(alpha, n_split, rows, x_ref, oa_ref, obc_ref, p_ref, wout_ref, wgu_ref, wdn_ref, wg_ref, wp_ref,
                lng_ref, lnb_ref, xo_ref):
    wa = oa_ref.shape[1]
    mixed = _dot(oa_ref[rows, :], wout_ref[0:wa, :]) + _dot(obc_ref[rows, :], wout_ref[wa:, :])
    proj = _dot(p_ref[rows, :].astype(BF16), wp_ref[...])
    yield
    x = _layer_norm(alpha * x_ref[rows, :] + mixed, lng_ref[1:2, :], lnb_ref[1:2, :])
    ffn = yield from _swiglu_stages(x.astype(BF16), wgu_ref, wdn_ref, n_split)
    x = _layer_norm(alpha * x + 0.5 * ffn, lng_ref[2:3, :], lnb_ref[2:3, :])
    emb = _sigmoid(_dot(x.astype(BF16), wg_ref[...])) * proj
    yield
    xo_ref[rows, :] = _layer_norm(alpha * x + emb, lng_ref[3:4, :], lnb_ref[3:4, :])


def _post_kernel(alpha, n_split, x_ref, oa_ref, obc_ref, p_ref, wout_ref, wgu_ref, wdn_ref, wg_ref, wp_ref,
                 lng_ref, lnb_ref, xo_ref):
    _run_interleaved([_post_chain(alpha, n_split, rows, x_ref, oa_ref, obc_ref, p_ref, wout_ref, wgu_ref, wdn_ref,
                                  wg_ref, wp_ref, lng_ref, lnb_ref, xo_ref) for rows in _sub_tiles(x_ref.shape[0])])


def _layer_spec(stacked, layer):
    rest = stacked.shape[1:]
    return pl.BlockSpec((None,) + rest, lambda *_: (layer,) + (0,) * len(rest), pipeline_mode=pl.Buffered(1))


def _token_tile(n):
    for tm in (2 * TOKEN_SUB_ROWS, TOKEN_SUB_ROWS, 128, 64, 32, 16, 8):
        if n % tm == 0:
            return tm
    raise ValueError(f"unsupported token count {n}")


def _pre_call(x, wgu, wdn, lng, lnb, win, layer, alpha):
    n, d = x.shape
    zw = win.shape[-1]
    tm = _token_tile(n)
    row = lambda i: (i, 0)
    consts = (wgu, wdn, lng, lnb, win)
    return pl.pallas_call(
        functools.partial(_pre_kernel, alpha, 2),
        grid=(n // tm,),
        in_specs=[pl.BlockSpec((tm, d), row)] + [_layer_spec(c, layer) for c in consts],
        out_specs=[pl.BlockSpec((tm, d), row), pl.BlockSpec((tm, zw), row)],
        out_shape=[jax.ShapeDtypeStruct((n, d), F32), jax.ShapeDtypeStruct((n, zw), F32)],
        compiler_params=pltpu.CompilerParams(dimension_semantics=("parallel",),
                                             vmem_limit_bytes=V7X_VMEM_LIMIT),
        name="pre_mixer_tokens",
    )(x, *consts)


def _post_call(x, oa, obc, p, wout, wgu, wdn, wg, wp, lng, lnb, layer, alpha):
    n, d = x.shape
    tm = _token_tile(n)
    row = lambda i: (i, 0)
    consts = (wout, wgu, wdn, wg, wp, lng, lnb)
    return pl.pallas_call(
        functools.partial(_post_kernel, alpha, 2),
        grid=(n // tm,),
        in_specs=[pl.BlockSpec((tm, d), row), pl.BlockSpec((tm, oa.shape[1]), row),
                  pl.BlockSpec((tm, obc.shape[1]), row),
                  pl.BlockSpec((None, tm, p.shape[-1]), lambda i: (layer, i, 0))]
        + [_layer_spec(c, layer) for c in consts],
        out_specs=pl.BlockSpec((tm, d), row),
        out_shape=jax.ShapeDtypeStruct((n, d), F32),
        compiler_params=pltpu.CompilerParams(dimension_semantics=("parallel",),
                                             vmem_limit_bytes=V7X_VMEM_LIMIT),
        name="post_mixer_tokens",
    )(x, oa, obc, p, *consts)


def _qk_scores(q_ref, kcat, s_out, r0, n_q, k0, n_k):
    scale = HEAD_DIM ** -0.5 * LOG2E
    low = lax.broadcasted_iota(jnp.int32, (1, V7X_LANES), 1) < HEAD_DIM
    for hp in range(q_ref.shape[1] // V7X_LANES):
        ls = slice(hp * V7X_LANES, (hp + 1) * V7X_LANES)
        q = q_ref[r0:r0 + n_q, ls] * scale
        lhs = jnp.concatenate([jnp.where(low, q, 0.0), jnp.where(low, 0.0, q)], axis=0).astype(BF16)
        s_out[2 * hp * n_q:2 * (hp + 1) * n_q, :] = _dot_nt(lhs, kcat[k0:k0 + n_k, ls])


def _softmax_weights(s_in, bias_ref, key_valid, p_out):
    s = s_in[...] + bias_ref[...]
    if key_valid is not None:
        s = jnp.where(key_valid, s, NEG_INF)
    p_out[...] = jnp.exp2(s - jnp.max(s, axis=-1, keepdims=True)).astype(BF16)


def _weighted_values(p_in, vcat, o_ref, r0, n_q, k0, n_k):
    low = lax.broadcasted_iota(jnp.int32, (1, V7X_LANES), 1) < HEAD_DIM
    ones = jnp.ones((n_k, V7X_LANES), BF16)
    for hp in range(o_ref.shape[1] // V7X_LANES):
        ls = slice(hp * V7X_LANES, (hp + 1) * V7X_LANES)
        rows = slice(2 * hp * n_q, 2 * (hp + 1) * n_q)
        pv = _dot(p_in[rows, :], jnp.concatenate([vcat[k0:k0 + n_k, ls], ones], axis=1))
        o2 = pv[:, :V7X_LANES] * (1.0 / pv[:, V7X_LANES:])
        o_ref[r0:r0 + n_q, ls] = jnp.where(low, o2[:n_q], o2[n_q:]).astype(o_ref.dtype)


def _build_rel_bias(table_ref, bias_ref, n_q, n_k, offset):
    n_heads = table_ref.shape[0]
    clip = lambda d: min(max(d, -(CHUNK - 1)), REL_MAX) + (CHUNK - 1)
    heads_per_pass = 4
    for j0 in range(0, n_k, V7X_LANES):
        jw = min(V7X_LANES, n_k - j0)
        r_lo, r_hi = clip(offset - (j0 + jw - 1)), clip(offset + n_q - 1 - j0)
        d = (offset - j0 + lax.broadcasted_iota(jnp.int32, (n_q, jw), 0)
             - lax.broadcasted_iota(jnp.int32, (n_q, jw), 1))
        idx = jnp.clip(d, -(CHUNK - 1), REL_MAX) + (CHUNK - 1)
        for h0 in range(0, n_heads, heads_per_pass):
            heads = range(h0, min(h0 + heads_per_pass, n_heads))
            if r_lo == r_hi:
                vals = [jnp.full((n_q, jw), table_ref[h, r_lo], F32) for h in heads]
            else:
                def pick(r, acc, heads=heads, idx=idx):
                    hit = idx == r
                    return tuple(jnp.where(hit, table_ref[h, r], a) for h, a in zip(heads, acc))
                vals = lax.fori_loop(r_lo, r_hi + 1, pick, tuple(jnp.zeros((n_q, jw), F32) for _ in heads))
            for h, v in zip(heads, vals):
                bias_ref[h * n_q:(h + 1) * n_q, j0:j0 + jw] = v * LOG2E


def _attn_prompt_kernel(qb_rows, table_ref, q_ref, kp_ref, kc_ref, vp_ref, vc_ref, o_ref,
                        kcat, vcat, bias_ref, s_buf, p_buf):
    i = pl.program_id(1)
    band = (A_BACK + 1) * CHUNK
    past = A_BACK * CHUNK
    n_chunks = qb_rows // CHUNK

    @pl.when((pl.program_id(0) == 0) & (i == 0))
    def _():
        _build_rel_bias(table_ref, bias_ref, CHUNK, band, past)

    def cast_past(j, carry):
        r = pl.multiple_of(j * CHUNK, CHUNK)
        kcat[pl.ds(r, CHUNK), :] = kp_ref[pl.ds(r, CHUNK), :].astype(BF16)
        vcat[pl.ds(r, CHUNK), :] = vp_ref[pl.ds(r, CHUNK), :].astype(BF16)
        return carry

    def cast_own(j, carry):
        r = pl.multiple_of(j * CHUNK, CHUNK)
        kcat[pl.ds(past + r, CHUNK), :] = kc_ref[pl.ds(r, CHUNK), :].astype(BF16)
        vcat[pl.ds(past + r, CHUNK), :] = vc_ref[pl.ds(r, CHUNK), :].astype(BF16)
        return carry

    lax.fori_loop(0, past // CHUNK, cast_past, 0)
    lax.fori_loop(0, n_chunks, cast_own, 0)
    key_idx = lax.broadcasted_iota(jnp.int32, (1, band), 1)

    _qk_scores(q_ref, kcat, s_buf.at[0], 0, CHUNK, 0, band)
    for c in range(n_chunks + 1):
        if c + 1 < n_chunks:
            _qk_scores(q_ref, kcat, s_buf.at[(c + 1) % 2], (c + 1) * CHUNK, CHUNK, (c + 1) * CHUNK, band)
        if c < n_chunks:
            key_valid = key_idx >= past - c * CHUNK - i * qb_rows
            _softmax_weights(s_buf.at[c % 2], bias_ref, key_valid, p_buf.at[c % 2])
        if c >= 1:
            _weighted_values(p_buf.at[(c - 1) % 2], vcat, o_ref,
                             (c - 1) * CHUNK, CHUNK, (c - 1) * CHUNK, band)


def _attn_prompt_call(z, table, batch, seq, a_width):
    past = A_BACK * CHUNK
    qb_rows = next(m * past for m in (ATTN_Q_BLOCK_PASTS, 1) if seq % (m * past) == 0)
    assert a_width % V7X_LANES == 0
    nq = seq // qb_rows
    per_q = qb_rows // past
    band = (A_BACK + 1) * CHUNK
    rows = A_HEADS * CHUNK
    cur = lambda col: (lambda b, i: (b * nq + i, col))
    prev = lambda col: (lambda b, i: (b * nq * per_q + jnp.maximum(i * per_q - 1, 0), col))
    blk = (qb_rows, a_width)
    pblk = (past, a_width)
    return pl.pallas_call(
        functools.partial(_attn_prompt_kernel, qb_rows),
        grid=(batch, nq),
        in_specs=[pl.BlockSpec(memory_space=pltpu.SMEM),
                  pl.BlockSpec(blk, cur(0)), pl.BlockSpec(pblk, prev(1)), pl.BlockSpec(blk, cur(1)),
                  pl.BlockSpec(pblk, prev(2)), pl.BlockSpec(blk, cur(2))],
        out_specs=pl.BlockSpec(blk, lambda b, i: (b * nq + i, 0)),
        out_shape=jax.ShapeDtypeStruct((batch * seq, a_width), BF16),
        scratch_shapes=[pltpu.VMEM((past + qb_rows, a_width), BF16), pltpu.VMEM((past + qb_rows, a_width), BF16),
                        pltpu.VMEM((rows, band), F32), pltpu.VMEM((2, rows, band), F32),
                        pltpu.VMEM((2, rows, band), BF16)],
        compiler_params=pltpu.CompilerParams(dimension_semantics=("arbitrary", "arbitrary"),
                                             vmem_limit_bytes=V7X_VMEM_LIMIT),
        name="band_attention_prompt",
    )(table, z, z, z, z, z)


def _attn_sample_kernel(n_seq, table_ref, q_ref, k_ref, v_ref, ck_ref, cv_ref, o_ref, kcat, vcat, bias_ref,
                        s_buf, p_buf):
    lc = ck_ref.shape[1]
    t = q_ref.shape[0] // n_seq
    n_k = lc + t

    @pl.when(pl.program_id(0) == 0)
    def _():
        _build_rel_bias(table_ref, bias_ref, t, n_k, lc)

    for s in range(n_seq):
        new = slice(s * t, (s + 1) * t)
        kcat[s, 0:lc, :] = ck_ref[s].astype(BF16)
        kcat[s, lc:, :] = k_ref[new, :].astype(BF16)
        vcat[s, 0:lc, :] = cv_ref[s].astype(BF16)
        vcat[s, lc:, :] = v_ref[new, :].astype(BF16)

    _qk_scores(q_ref, kcat.at[0], s_buf.at[0], 0, t, 0, n_k)
    for s in range(n_seq + 1):
        if s + 1 < n_seq:
            _qk_scores(q_ref, kcat.at[s + 1], s_buf.at[s + 1], (s + 1) * t, t, 0, n_k)
        if s < n_seq:
            _softmax_weights(s_buf.at[s], bias_ref, None, p_buf.at[s])
        if s >= 1:
            _weighted_values(p_buf.at[s - 1], vcat.at[s - 1], o_ref, (s - 1) * t, t, 0, n_k)


def _attn_sample_call(z, cache_k, cache_v, table, layer, batch, t, a_width):
    lc = cache_k.shape[2]
    n_seq = ATTN_STEP_SEQS if batch % ATTN_STEP_SEQS == 0 else 1
    rows = A_HEADS * t
    blk = (n_seq * t, a_width)
    cblk = (None, n_seq, lc, a_width)
    cidx = lambda b: (layer, b, 0, 0)
    return pl.pallas_call(
        functools.partial(_attn_sample_kernel, n_seq),
        grid=(batch // n_seq,),
        in_specs=[pl.BlockSpec(memory_space=pltpu.SMEM),
                  pl.BlockSpec(blk, lambda b: (b, 0)), pl.BlockSpec(blk, lambda b: (b, 1)),
                  pl.BlockSpec(blk, lambda b: (b, 2)),
                  pl.BlockSpec(cblk, cidx), pl.BlockSpec(cblk, cidx)],
        out_specs=pl.BlockSpec(blk, lambda b: (b, 0)),
        out_shape=jax.ShapeDtypeStruct((batch * t, a_width), BF16),
        scratch_shapes=[pltpu.VMEM((n_seq, lc + t, a_width), BF16), pltpu.VMEM((n_seq, lc + t, a_width), BF16),
                        pltpu.VMEM((rows, lc + t), F32), pltpu.VMEM((n_seq, rows, lc + t), F32),
                        pltpu.VMEM((n_seq, rows, lc + t), BF16)],
        compiler_params=pltpu.CompilerParams(dimension_semantics=("arbitrary",),
                                             vmem_limit_bytes=V7X_VMEM_LIMIT),
        name="band_attention_sample",
    )(table, z, z, z, cache_k, cache_v)


def _split2(x):
    a = x.astype(BF16)
    return a, (x - a.astype(F32)).astype(BF16)


_RecurRefs = collections.namedtuple(
    "_RecurRefs", "u q f i g lbraw normw poolw pscale out state hist dmask inter rows_b rows_k")


def _build_decay_masks(dmask, blk, tm):
    sb = min(SUB_BLOCK, blk)
    ri = lax.broadcasted_iota(jnp.int32, (tm, tm), 0)
    ci = lax.broadcasted_iota(jnp.int32, (tm, tm), 1)
    same_sb = _div(ri, sb) == _div(ci, sb)
    dmask[0] = jnp.where(same_sb & (ci <= ri), 1.0, 0.0).astype(BF16)
    dmask[1] = jnp.where(same_sb, 1.0, 0.0).astype(BF16)


def _recur_tile(layer, blk, pos0, n_heads, R, rows, t_tile, fallbacks):
    tm = rows.stop - rows.start
    width = R.u.shape[1]
    hd = width // n_heads
    sb = min(SUB_BLOCK, blk)
    assert blk // sb in (1, 2) and tm % blk == 0

    raw = R.lbraw[...]
    sm = jnp.exp(raw - jnp.max(raw, axis=0, keepdims=True))
    sm = sm / jnp.sum(sm, axis=0, keepdims=True)
    lb = jnp.zeros((1, width), F32)
    for j in range(1, layer + 1):
        lb = lb + sm[j:j + 1, :]
    forget = lb + (1.0 - lb) * _sigmoid(R.f[rows, :])
    log_f = jnp.log(forget)
    k_in = 1.0 - forget
    qx = R.q[rows, :]
    qf = qx * _sigmoid(qx)
    vb = R.i[rows, :].astype(BF16)

    lf2 = jnp.concatenate(_split2(log_f), axis=1)

    def decay_sum(mask01):
        r = _dot(mask01, lf2)
        return r[:, 0:width] + r[:, width:]

    b_rel = decay_sum(R.dmask[0])
    sb_tot = decay_sum(R.dmask[1])
    yield
    if blk > sb:
        second = (lax.broadcasted_iota(jnp.int32, (tm, 1), 0) & (blk - 1)) >= sb
        prev_tot = pltpu.roll(sb_tot, sb, 0)
        next_tot = pltpu.roll(sb_tot, tm - sb, 0)
        b = b_rel + jnp.where(second, prev_tot, 0.0)
        blk_tot = sb_tot + jnp.where(second, prev_tot, next_tot)
    else:
        b, blk_tot = b_rel, sb_tot

    ext = jnp.concatenate([R.hist[...], R.u[rows, :]], axis=0)
    R.hist[...] = ext[tm:, :]
    s2 = ext + pltpu.roll(ext, 1, 0)
    s4 = s2 + pltpu.roll(s2, 2, 0)
    s8 = s4 + pltpu.roll(s4, 4, 0)
    s16 = s8 + pltpu.roll(s8, 8, 0)
    lane = lax.broadcasted_iota(jnp.int32, (1, width), 1)
    grp = _div(lane, width // len(POOL_WINDOWS))
    wsum = jnp.where(grp == 0, s2, jnp.where(grp == 1, s4, jnp.where(grp == 2, s8, s16)))
    wlen = jnp.where(grp == 0, 2.0, jnp.where(grp == 1, 4.0, jnp.where(grp == 2, 8.0, 16.0)))
    row = lax.broadcasted_iota(jnp.int32, (POOL_PAD + tm, 1), 0)
    pos = (row + (pos0 - POOL_PAD) + t_tile * tm).astype(F32)
    cnt = jnp.maximum(jnp.minimum(pos + 1.0, wlen), 1.0)
    dev = (wsum / cnt - ext)[POOL_PAD:, :]
    o_pool = _dot(dev.astype(BF16), R.poolw[...]) * R.pscale[...]
    R.out[rows, 0:width] = o_pool.astype(R.out.dtype)

    q_rel = qf * jnp.exp(b_rel)
    k_rel = (k_in * jnp.exp(-b_rel)).astype(BF16)
    k_end = (k_in * jnp.exp(sb_tot - b_rel)).astype(BF16)
    q_abs = (qf * jnp.exp(b)).astype(BF16)
    k_tail = (k_in * jnp.exp(blk_tot - b)).astype(BF16)

    hb = n_heads * blk
    n_chunks = tm // blk
    stack_head = _div(lax.broadcasted_iota(jnp.int32, (hb, 1), 0), blk) == _div(lane, hd)
    tq = lax.broadcasted_iota(jnp.int32, (hb, blk), 0) & (blk - 1)
    ts = lax.broadcasted_iota(jnp.int32, (hb, blk), 1)
    m_intra = (_div(tq, sb) == _div(ts, sb)) & (ts <= tq)
    m_cross = _div(ts, sb) < _div(tq, sb)
    bd = (_div(lax.broadcasted_iota(jnp.int32, (width, width), 0), hd)
          == _div(lax.broadcasted_iota(jnp.int32, (width, width), 1), hd))
    ones_bd = jnp.where(bd, 1.0, 0.0).astype(BF16)
    gx = R.g[rows, :]
    out_gate = R.normw[...] * (gx * _sigmoid(gx))

    def write_output(o):
        sq_hi, sq_lo = _split2(o * o)
        ms = (_dot(sq_hi, ones_bd) + _dot(sq_lo, ones_bd)) * (1.0 / hd)
        R.out[rows, width:] = (o * lax.rsqrt(ms + RMS_EPS) * out_gate).astype(R.out.dtype)

    chunks = [slice(c * blk, (c + 1) * blk) for c in range(n_chunks)]
    state_in = [jnp.where(bd, _dot_tn(vb[rs], k_tail[rs]), 0.0) for rs in chunks]
    yield
    two_sets = blk > sb
    if two_sets:
        tq2 = lax.broadcasted_iota(jnp.int32, (hb, 2 * blk), 0) & (blk - 1)
        col = lax.broadcasted_iota(jnp.int32, (hb, 2 * blk), 1)
        ts2 = col & (blk - 1)
        keep = (((col < blk) & (_div(tq2, sb) == _div(ts2, sb)) & (ts2 <= tq2))
                | ((col >= blk) & (_div(ts2, sb) < _div(tq2, sb))))
    else:
        keep = m_intra
    scores = []
    for rs in chunks:
        lhs = jnp.where(stack_head, jnp.concatenate([q_rel[rs]] * n_heads, axis=0), 0.0).astype(BF16)
        keys = jnp.concatenate([k_rel[rs], k_end[rs]], axis=0) if two_sets else k_rel[rs]
        scores.append(jnp.where(keep, _dot_nt(lhs, keys), 0.0).astype(BF16))
    yield
    outs = []
    st = R.state[...]
    for c, rs in enumerate(chunks):
        from_state = _dot_nt(q_abs[rs], st.astype(BF16))
        values = jnp.concatenate([vb[rs], vb[rs]], axis=0) if two_sets else vb[rs]
        stacked = jnp.where(stack_head, _dot(scores[c], values), 0.0)
        o_c = stacked[0:blk]
        for h in range(1, n_heads):
            o_c = o_c + stacked[h * blk:(h + 1) * blk]
        R.inter[pl.ds(rows.start + c * blk, blk), :] = from_state
        outs.append(o_c + from_state)
        decay_end = jnp.exp(blk_tot[c * blk:c * blk + 1, :])
        st = st * decay_end + state_in[c]
        yield
    R.state[...] = st
    write_output(outs[0] if len(outs) == 1 else jnp.concatenate(outs, axis=0))

    def fallback():
        @pl.when(jnp.min(sb_tot) <= DECAY_SAFE_LOG)
        def _():
            R.rows_b[...] = b
            R.rows_k[...] = k_in
            t_row = lax.broadcasted_iota(jnp.int32, (blk, 1), 0)

            def add_key_row(s, acc):
                parts = []
                for c, rs in enumerate(chunks):
                    r = c * blk + s
                    rel = jnp.minimum(b[rs] - R.rows_b[pl.ds(r, 1), :], 0.0)
                    term = jnp.where(t_row >= s, qf[rs] * jnp.exp(rel) * R.rows_k[pl.ds(r, 1), :], 0.0)
                    parts.append(_dot(term.astype(BF16), ones_bd) * R.i[pl.ds(rows.start + r, 1), :])
                return acc + (parts[0] if n_chunks == 1 else jnp.concatenate(parts, axis=0))

            write_output(lax.fori_loop(0, blk, add_key_row, R.inter[rows, :]))

    fallbacks.append(fallback)
    yield


def _run_staggered(chains, start_rounds):
    live = list(zip(start_rounds, chains))
    rnd = 0
    while live:
        for item in list(live):
            start, ch = item
            if rnd >= start:
                try:
                    next(ch)
                except StopIteration:
                    live.remove(item)
        rnd += 1


def _recur_kernel(layer, blk, pos0, n_heads, rtm, n_seq, u_ref, q_ref, f_ref, i_ref, g_ref, hist0_ref, st0_ref,
                  lbraw_ref, normw_ref, poolw_ref, pscale_ref, o_ref, st_ref, state, hist, dmask, inter,
                  rows_b, rows_k):
    t_idx = pl.program_id(1)
    seq_rows = u_ref.shape[0] // n_seq
    tiles = seq_rows // rtm

    @pl.when(t_idx == 0)
    def _():
        state[...] = st0_ref[...]
        hist[...] = hist0_ref[...]
        _build_decay_masks(dmask, blk, rtm)

    fallbacks, chains, starts = [], [], []
    for s in range(n_seq):
        R = _RecurRefs(u_ref, q_ref, f_ref, i_ref, g_ref, lbraw_ref, normw_ref, poolw_ref, pscale_ref, o_ref,
                       state.at[s], hist.at[s], dmask, inter, rows_b, rows_k)
        for k in range(tiles):
            r0 = s * seq_rows + k * rtm
            chains.append(_recur_tile(layer, blk, pos0, n_heads, R, slice(r0, r0 + rtm), t_idx * tiles + k, fallbacks))
            starts.append(k * (rtm // blk))
    _run_staggered(chains, starts)
    for fb in fallbacks:
        fb()

    @pl.when(t_idx == pl.num_programs(1) - 1)
    def _():
        st_ref[...] = state[...]


RECUR_TILE = 256
RECUR_STEP_ROWS = 2048
RECUR_STEP_SEQS = 8


def _recur_scratch(n_seq, rows, tm, width):
    return [pltpu.VMEM((n_seq, width, width), F32), pltpu.VMEM((n_seq, POOL_PAD, width), F32),
            pltpu.VMEM((2, tm, tm), BF16), pltpu.VMEM((rows, width), F32), pltpu.VMEM((tm, width), F32),
            pltpu.VMEM((tm, width), F32)]


def _recur_call(z, hist0, st0, lbraw, normw, poolw, pscale, layer, state_layer, batch, seq, blk, pos0, col0):
    width = st0.shape[-1]
    rtm = min(RECUR_TILE, seq)
    tm = min(RECUR_STEP_ROWS, seq)
    n_seq = 1 if tm < seq else min(RECUR_STEP_SEQS, max(1, RECUR_STEP_ROWS // seq), batch)
    assert seq % tm == 0 and tm % rtm == 0 and rtm % blk == 0 and batch % n_seq == 0
    nt = seq // tm
    col = lambda c: (lambda b, t: (b * nt + t, c))
    per_batch = lambda b, t: (state_layer, b, 0, 0)
    consts = (normw, poolw, pscale)
    return pl.pallas_call(
        functools.partial(_recur_kernel, layer, blk, pos0, C_HEADS, rtm, n_seq),
        grid=(batch // n_seq, nt),
        in_specs=[pl.BlockSpec((n_seq * tm, width), col(col0 + j)) for j in range(5)]
        + [pl.BlockSpec((None, n_seq, POOL_PAD, width), per_batch),
           pl.BlockSpec((None, n_seq, width, width), per_batch),
           pl.BlockSpec(lbraw.shape, lambda b, t: (0, 0), pipeline_mode=pl.Buffered(1))]
        + [_layer_spec(c, layer) for c in consts],
        out_specs=[pl.BlockSpec((n_seq * tm, 2 * width), lambda b, t: (b * nt + t, 0)),
                   pl.BlockSpec((n_seq, width, width), lambda b, t: (b, 0, 0))],
        out_shape=[jax.ShapeDtypeStruct((batch * seq, 2 * width), BF16),
                   jax.ShapeDtypeStruct((batch, width, width), F32)],
        scratch_shapes=_recur_scratch(n_seq, n_seq * tm, rtm, width),
        compiler_params=pltpu.CompilerParams(dimension_semantics=("parallel", "arbitrary"),
                                             vmem_limit_bytes=V7X_VMEM_LIMIT),
        name="pool_hgrn_mixer",
    )(z, z, z, z, z, hist0, st0, lbraw, *consts)


def _block_diag(blocks):
    g = blocks.shape[-3]
    zero = jnp.zeros_like(blocks[..., 0, :, :])
    rows = [jnp.concatenate([blocks[..., h, :, :] if j == h else zero for j in range(g)], axis=-1)
            for h in range(g)]
    return jnp.concatenate(rows, axis=-2)


def _state_from_kernel(st, n_heads):
    b, w, _ = st.shape
    hd = w // n_heads
    blocks = jnp.stack([st[:, h * hd:(h + 1) * hd, h * hd:(h + 1) * hd] for h in range(n_heads)], axis=1)
    return jnp.swapaxes(blocks, -1, -2)


def kernel(x_prompt, x_sample, p_prompt, p_sample, cache_attn_k, cache_attn_v, state_pool, state_hgrn, ffn1_w_gu, ffn1_w_down, w_in, attn_rel_bias, pool_w, pool_scale, hgrn_lower_bounds, hgrn_norm_w, w_out, ffn2_w_gu, ffn2_w_down, ple_w_gate, ple_w_proj, ln_g, ln_b):
    depth = w_in.shape[0]
    alpha = float((2 * depth) ** 0.25)
    nb, seq, d = x_prompt.shape
    db, dt, _ = x_sample.shape
    a_width = A_HEADS * HEAD_DIM
    c_width = C_HEADS * HEAD_DIM
    in_width = w_in.shape[-1]
    col0 = 3 * a_width // c_width
    past_rows = min(A_BACK * CHUNK, seq)
    assert dt >= POOL_HIST and seq >= POOL_HIST and PAST_LEN >= POOL_HIST

    bf = lambda w: w.astype(BF16)
    w1gu, w1dn, win, wout = bf(ffn1_w_gu), bf(ffn1_w_down), bf(w_in), bf(w_out)
    w2gu, w2dn, wg, wp = bf(ffn2_w_gu), bf(ffn2_w_down), bf(ple_w_gate), bf(ple_w_proj)
    poolw = bf(_block_diag(pool_w))
    pscale = pool_scale.astype(F32)[:, None, :]
    normw = jnp.tile(hgrn_norm_w.astype(F32), (1, C_HEADS))[:, None, :]
    lbraw = hgrn_lower_bounds.astype(F32)
    pp = p_prompt.reshape(depth, nb * seq, -1)
    ps = p_sample.reshape(depth, db * dt, -1)

    zero_hist = jnp.zeros((1, nb, POOL_PAD, c_width), F32)
    zero_state = jnp.zeros((1, nb, c_width, c_width), F32)
    hist_s = jnp.pad(state_pool.astype(F32), ((0, 0), (0, 0), (POOL_PAD - POOL_HIST, 0), (0, 0)))
    state_s = _block_diag(jnp.swapaxes(state_hgrn.astype(F32), -1, -2))

    lc = cache_attn_k.shape[2]
    cache_k = cache_attn_k.reshape(depth, db, lc, a_width)
    cache_v = cache_attn_v.reshape(depth, db, lc, a_width)

    xp = x_prompt.reshape(nb * seq, d)
    xs = x_sample.reshape(db * dt, d)
    outs = [[] for _ in range(8)]

    for i in range(depth):
        rel_table = attn_rel_bias[i].astype(F32)

        xp, zp = _pre_call(xp, w1gu, w1dn, ln_g, ln_b, win, i, alpha)
        oa = _attn_prompt_call(zp, rel_table, nb, seq, a_width)
        obc, st = _recur_call(zp, zero_hist, zero_state, lbraw, normw, poolw, pscale, i, 0, nb, seq, CHUNK, 0, col0)
        xp = _post_call(xp, oa, obc, pp, wout, w2gu, w2dn, wg, wp, ln_g, ln_b, i, alpha)
        zp3 = zp.reshape(nb, seq, in_width)
        outs[0].append(zp3[:, seq - past_rows:, a_width:2 * a_width].reshape(nb, past_rows, A_HEADS, HEAD_DIM))
        outs[1].append(zp3[:, seq - past_rows:, 2 * a_width:3 * a_width].reshape(nb, past_rows, A_HEADS, HEAD_DIM))
        outs[2].append(zp3[:, seq - POOL_HIST:, 3 * a_width:3 * a_width + c_width])
        outs[3].append(_state_from_kernel(st, C_HEADS))

        xs, zs = _pre_call(xs, w1gu, w1dn, ln_g, ln_b, win, i, alpha)
        oa = _attn_sample_call(zs, cache_k, cache_v, rel_table, i, db, dt, a_width)
        obc, st = _recur_call(zs, hist_s, state_s, lbraw, normw, poolw, pscale, i, i, db, dt, dt, PAST_LEN, col0)
        xs = _post_call(xs, oa, obc, ps, wout, w2gu, w2dn, wg, wp, ln_g, ln_b, i, alpha)
        zs3 = zs.reshape(db, dt, in_width)
        outs[4].append(zs3[:, :, a_width:2 * a_width].reshape(db, dt, A_HEADS, HEAD_DIM))
        outs[5].append(zs3[:, :, 2 * a_width:3 * a_width].reshape(db, dt, A_HEADS, HEAD_DIM))
        outs[6].append(zs3[:, dt - POOL_HIST:, 3 * a_width:3 * a_width + c_width])
        outs[7].append(_state_from_kernel(st, C_HEADS))

    stacked = [jnp.stack(o, axis=0) for o in outs]
    return (xp.reshape(nb, seq, d), xs.reshape(db, dt, d), *stacked)
```

```python
import collections
import functools

import jax
import jax.numpy as jnp
from jax import lax
from jax.experimental import pallas as pl
from jax.experimental.pallas import tpu as pltpu

F32 = jnp.float32
BF16 = jnp.bfloat16

PAST_LEN = 2048
CHUNK = 64
A_BACK = 8
HEAD_DIM = 64
A_HEADS = 8
C_HEADS = 4
REL_MAX = 128
POOL_WINDOWS = (2, 4, 8, 16)
POOL_HIST = max(POOL_WINDOWS) - 1
POOL_PAD = POOL_HIST + 1
SUB_BLOCK = 32
DECAY_SAFE_LOG = -60.0
TOKEN_SUB_ROWS = 256
ATTN_STEP_SEQS = 8
ATTN_Q_BLOCK_PASTS = 2
LN_EPS = 1e-5
RMS_EPS = 1e-6
NEG_INF = -1e30
LOG2E = 1.4426950408889634

V7X_LANES = 128
V7X_MXU_WIDTH = 256
V7X_VMEM_LIMIT = 56 * 1024 * 1024


def _dot(a, b):
    return jnp.dot(a, b, preferred_element_type=F32)


def _dot_nt(a, b):
    return lax.dot_general(a, b, (((1,), (1,)), ((), ())), preferred_element_type=F32)


def _dot_tn(a, b):
    return lax.dot_general(a, b, (((0,), (0,)), ((), ())), preferred_element_type=F32)


def _layer_norm(y, g, b):
    mu = jnp.mean(y, axis=-1, keepdims=True)
    d = y - mu
    var = jnp.mean(d * d, axis=-1, keepdims=True)
    return d * lax.rsqrt(var + LN_EPS) * g + b


def _sigmoid(x):
    return 1.0 / (1.0 + jnp.exp(-x))


def _log2(n):
    assert n > 0 and n & (n - 1) == 0, n
    return n.bit_length() - 1


def _div(x, n):
    return x >> _log2(n)


def _swiglu_stages(xb, wgu_ref, wdn_ref, n_split):
    hidden = wdn_ref.shape[0]
    groups = _hidden_groups(hidden, n_split)
    gate_up = []
    for lo, hi in groups:
        gate_up.append((_dot(xb, wgu_ref[:, lo:hi]), _dot(xb, wgu_ref[:, hidden + lo:hidden + hi])))
        yield
    acc = None
    for (lo, hi), (gate, up) in zip(groups, gate_up):
        act = (gate * _sigmoid(gate) * up).astype(BF16)
        part = _dot(act, wdn_ref[lo:hi, :])
        acc = part if acc is None else acc + part
        yield
    return acc


def _hidden_groups(hidden, n_split):
    if hidden % V7X_MXU_WIDTH:
        return [(0, hidden)]
    tiles = hidden // V7X_MXU_WIDTH
    bounds = [V7X_MXU_WIDTH * ((tiles * g + n_split - 1) // n_split) for g in range(n_split + 1)]
    return [(lo, hi) for lo, hi in zip(bounds[:-1], bounds[1:]) if hi > lo]


def _run_interleaved(chains):
    live = list(chains)
    while live:
        for ch in list(live):
            if ch not in live:
                continue
            try:
                next(ch)
            except StopIteration:
                live = [c for c in live if c is not ch]


def _sub_tiles(n_rows):
    sub = min(TOKEN_SUB_ROWS, n_rows)
    return [slice(r, r + sub) for r in range(0, n_rows, sub)]


def _pre_kernel(alpha, n_split, x_ref, wgu_ref, wdn_ref, lng_ref, lnb_ref, win_ref, xo_ref, z_ref):
    def chain(rows):
        x = x_ref[rows, :]
        ffn = yield from _swiglu_stages(x.astype(BF16), wgu_ref, wdn_ref, n_split)
        xn = _layer_norm(alpha * x + 0.5 * ffn, lng_ref[0:1, :], lnb_ref[0:1, :])
        xo_ref[rows, :] = xn
        z_ref[rows, :] = _dot(xn.astype(BF16), win_ref[...])
        yield

    _run_interleaved([chain(rows) for rows in _sub_tiles(x_ref.shape[0])])


def _post_chain(alpha, n_split, rows, x_ref, oa_ref, obc_ref, p_ref, wout_ref, wgu_ref, wdn_ref, wg_ref, wp_ref,
                lng_ref, lnb_ref, xo_ref):
    wa = oa_ref.shape[1]
    mixed = _dot(oa_ref[rows, :], wout_ref[0:wa, :]) + _dot(obc_ref[rows, :], wout_ref[wa:, :])
    proj = _dot(p_ref[rows, :].astype(BF16), wp_ref[...])
    yield
    x = _layer_norm(alpha * x_ref[rows, :] + mixed, lng_ref[1:2, :], lnb_ref[1:2, :])
    ffn = yield from _swiglu_stages(x.astype(BF16), wgu_ref, wdn_ref, n_split)
    x = _layer_norm(alpha * x + 0.5 * ffn, lng_ref[2:3, :], lnb_ref[2:3, :])
    emb = _sigmoid(_dot(x.astype(BF16), wg_ref[...])) * proj
    yield
    xo_ref[rows, :] = _layer_norm(alpha * x + emb, lng_ref[3:4, :], lnb_ref[3:4, :])


def _post_kernel(alpha, n_split, x_ref, oa_ref, obc_ref, p_ref, wout_ref, wgu_ref, wdn_ref, wg_ref, wp_ref,
                 lng_ref, lnb_ref, xo_ref):
    _run_interleaved([_post_chain(alpha, n_split, rows, x_ref, oa_ref, obc_ref, p_ref, wout_ref, wgu_ref, wdn_ref,
                                  wg_ref, wp_ref, lng_ref, lnb_ref, xo_ref) for rows in _sub_tiles(x_ref.shape[0])])


def _layer_spec(stacked, layer):
    rest = stacked.shape[1:]
    return pl.BlockSpec((None,) + rest, lambda *_: (layer,) + (0,) * len(rest), pipeline_mode=pl.Buffered(1))


def _token_tile(n):
    for tm in (2 * TOKEN_SUB_ROWS, TOKEN_SUB_ROWS, 128, 64, 32, 16, 8):
        if n % tm == 0:
            return tm
    raise ValueError(f"unsupported token count {n}")


def _pre_call(x, wgu, wdn, lng, lnb, win, layer, alpha):
    n, d = x.shape
    zw = win.shape[-1]
    tm = _token_tile(n)
    row = lambda i: (i, 0)
    consts = (wgu, wdn, lng, lnb, win)
    return pl.pallas_call(
        functools.partial(_pre_kernel, alpha, 2),
        grid=(n // tm,),
        in_specs=[pl.BlockSpec((tm, d), row)] + [_layer_spec(c, layer) for c in consts],
        out_specs=[pl.BlockSpec((tm, d), row), pl.BlockSpec((tm, zw), row)],
        out_shape=[jax.ShapeDtypeStruct((n, d), F32), jax.ShapeDtypeStruct((n, zw), F32)],
        compiler_params=pltpu.CompilerParams(dimension_semantics=("parallel",),
                                             vmem_limit_bytes=V7X_VMEM_LIMIT),
        name="pre_mixer_tokens",
    )(x, *consts)


def _post_call(x, oa, obc, p, wout, wgu, wdn, wg, wp, lng, lnb, layer, alpha):
    n, d = x.shape
    tm = _token_tile(n)
    row = lambda i: (i, 0)
    consts = (wout, wgu, wdn, wg, wp, lng, lnb)
    return pl.pallas_call(
        functools.partial(_post_kernel, alpha, 2),
        grid=(n // tm,),
        in_specs=[pl.BlockSpec((tm, d), row), pl.BlockSpec((tm, oa.shape[1]), row),
                  pl.BlockSpec((tm, obc.shape[1]), row),
                  pl.BlockSpec((None, tm, p.shape[-1]), lambda i: (layer, i, 0))]
        + [_layer_spec(c, layer) for c in consts],
        out_specs=pl.BlockSpec((tm, d), row),
        out_shape=jax.ShapeDtypeStruct((n, d), F32),
        compiler_params=pltpu.CompilerParams(dimension_semantics=("parallel",),
                                             vmem_limit_bytes=V7X_VMEM_LIMIT),
        name="post_mixer_tokens",
    )(x, oa, obc, p, *consts)


def _qk_scores(q_ref, kcat, s_out, r0, n_q, k0, n_k):
    scale = HEAD_DIM ** -0.5 * LOG2E
    low = lax.broadcasted_iota(jnp.int32, (1, V7X_LANES), 1) < HEAD_DIM
    for hp in range(q_ref.shape[1] // V7X_LANES):
        ls = slice(hp * V7X_LANES, (hp + 1) * V7X_LANES)
        q = q_ref[r0:r0 + n_q, ls] * scale
        lhs = jnp.concatenate([jnp.where(low, q, 0.0), jnp.where(low, 0.0, q)], axis=0).astype(BF16)
        s_out[2 * hp * n_q:2 * (hp + 1) * n_q, :] = _dot_nt(lhs, kcat[k0:k0 + n_k, ls])


def _softmax_weights(s_in, bias_ref, key_valid, p_out):
    s = s_in[...] + bias_ref[...]
    if key_valid is not None:
        s = jnp.where(key_valid, s, NEG_INF)
    p_out[...] = jnp.exp2(s - jnp.max(s, axis=-1, keepdims=True)).astype(BF16)


def _weighted_values(p_in, vcat, o_ref, r0, n_q, k0, n_k):
    low = lax.broadcasted_iota(jnp.int32, (1, V7X_LANES), 1) < HEAD_DIM
    ones = jnp.ones((n_k, V7X_LANES), BF16)
    for hp in range(o_ref.shape[1] // V7X_LANES):
        ls = slice(hp * V7X_LANES, (hp + 1) * V7X_LANES)
        rows = slice(2 * hp * n_q, 2 * (hp + 1) * n_q)
        pv = _dot(p_in[rows, :], jnp.concatenate([vcat[k0:k0 + n_k, ls], ones], axis=1))
        o2 = pv[:, :V7X_LANES] * (1.0 / pv[:, V7X_LANES:])
        o_ref[r0:r0 + n_q, ls] = jnp.where(low, o2[:n_q], o2[n_q:]).astype(o_ref.dtype)


def _build_rel_bias(table_ref, bias_ref, n_q, n_k, offset):
    n_heads = table_ref.shape[0]
    clip = lambda d: min(max(d, -(CHUNK - 1)), REL_MAX) + (CHUNK - 1)
    heads_per_pass = 4
    for j0 in range(0, n_k, V7X_LANES):
        jw = min(V7X_LANES, n_k - j0)
        r_lo, r_hi = clip(offset - (j0 + jw - 1)), clip(offset + n_q - 1 - j0)
        d = (offset - j0 + lax.broadcasted_iota(jnp.int32, (n_q, jw), 0)
             - lax.broadcasted_iota(jnp.int32, (n_q, jw), 1))
        idx = jnp.clip(d, -(CHUNK - 1), REL_MAX) + (CHUNK - 1)
        for h0 in range(0, n_heads, heads_per_pass):
            heads = range(h0, min(h0 + heads_per_pass, n_heads))
            if r_lo == r_hi:
                vals = [jnp.full((n_q, jw), table_ref[h, r_lo], F32) for h in heads]
            else:
                def pick(r, acc, heads=heads, idx=idx):
                    hit = idx == r
                    return tuple(jnp.where(hit, table_ref[h, r], a) for h, a in zip(heads, acc))
                vals = lax.fori_loop(r_lo, r_hi + 1, pick, tuple(jnp.zeros((n_q, jw), F32) for _ in heads))
            for h, v in zip(heads, vals):
                bias_ref[h * n_q:(h + 1) * n_q, j0:j0 + jw] = v * LOG2E


def _attn_prompt_kernel(qb_rows, table_ref, q_ref, kp_ref, kc_ref, vp_ref, vc_ref, o_ref,
                        kcat, vcat, bias_ref, s_buf, p_buf):
    i = pl.program_id(1)
    band = (A_BACK + 1) * CHUNK
    past = A_BACK * CHUNK
    n_chunks = qb_rows // CHUNK

    @pl.when((pl.program_id(0) == 0) & (i == 0))
    def _():
        _build_rel_bias(table_ref, bias_ref, CHUNK, band, past)

    def cast_past(j, carry):
        r = pl.multiple_of(j * CHUNK, CHUNK)
        kcat[pl.ds(r, CHUNK), :] = kp_ref[pl.ds(r, CHUNK), :].astype(BF16)
        vcat[pl.ds(r, CHUNK), :] = vp_ref[pl.ds(r, CHUNK), :].astype(BF16)
        return carry

    def cast_own(j, carry):
        r = pl.multiple_of(j * CHUNK, CHUNK)
        kcat[pl.ds(past + r, CHUNK), :] = kc_ref[pl.ds(r, CHUNK), :].astype(BF16)
        vcat[pl.ds(past + r, CHUNK), :] = vc_ref[pl.ds(r, CHUNK), :].astype(BF16)
        return carry

    lax.fori_loop(0, past // CHUNK, cast_past, 0)
    lax.fori_loop(0, n_chunks, cast_own, 0)
    key_idx = lax.broadcasted_iota(jnp.int32, (1, band), 1)

    _qk_scores(q_ref, kcat, s_buf.at[0], 0, CHUNK, 0, band)
    for c in range(n_chunks + 1):
        if c + 1 < n_chunks:
            _qk_scores(q_ref, kcat, s_buf.at[(c + 1) % 2], (c + 1) * CHUNK, CHUNK, (c + 1) * CHUNK, band)
        if c < n_chunks:
            key_valid = key_idx >= past - c * CHUNK - i * qb_rows
            _softmax_weights(s_buf.at[c % 2], bias_ref, key_valid, p_buf.at[c % 2])
        if c >= 1:
            _weighted_values(p_buf.at[(c - 1) % 2], vcat, o_ref,
                             (c - 1) * CHUNK, CHUNK, (c - 1) * CHUNK, band)


def _attn_prompt_call(z, table, batch, seq, a_width):
    past = A_BACK * CHUNK
    qb_rows = next(m * past for m in (ATTN_Q_BLOCK_PASTS, 1) if seq % (m * past) == 0)
    assert a_width % V7X_LANES == 0
    nq = seq // qb_rows
    per_q = qb_rows // past
    band = (A_BACK + 1) * CHUNK
    rows = A_HEADS * CHUNK
    cur = lambda col: (lambda b, i: (b * nq + i, col))
    prev = lambda col: (lambda b, i: (b * nq * per_q + jnp.maximum(i * per_q - 1, 0), col))
    blk = (qb_rows, a_width)
    pblk = (past, a_width)
    return pl.pallas_call(
        functools.partial(_attn_prompt_kernel, qb_rows),
        grid=(batch, nq),
        in_specs=[pl.BlockSpec(memory_space=pltpu.SMEM),
                  pl.BlockSpec(blk, cur(0)), pl.BlockSpec(pblk, prev(1)), pl.BlockSpec(blk, cur(1)),
                  pl.BlockSpec(pblk, prev(2)), pl.BlockSpec(blk, cur(2))],
        out_specs=pl.BlockSpec(blk, lambda b, i: (b * nq + i, 0)),
        out_shape=jax.ShapeDtypeStruct((batch * seq, a_width), BF16),
        scratch_shapes=[pltpu.VMEM((past + qb_rows, a_width), BF16), pltpu.VMEM((past + qb_rows, a_width), BF16),
                        pltpu.VMEM((rows, band), F32), pltpu.VMEM((2, rows, band), F32),
                        pltpu.VMEM((2, rows, band), BF16)],
        compiler_params=pltpu.CompilerParams(dimension_semantics=("arbitrary", "arbitrary"),
                                             vmem_limit_bytes=V7X_VMEM_LIMIT),
        name="band_attention_prompt",
    )(table, z, z, z, z, z)


def _attn_sample_kernel(n_seq, table_ref, q_ref, k_ref, v_ref, ck_ref, cv_ref, o_ref, kcat, vcat, bias_ref,
                        s_buf, p_buf):
    lc = ck_ref.shape[1]
    t = q_ref.shape[0] // n_seq
    n_k = lc + t

    @pl.when(pl.program_id(0) == 0)
    def _():
        _build_rel_bias(table_ref, bias_ref, t, n_k, lc)

    for s in range(n_seq):
        new = slice(s * t, (s + 1) * t)
        kcat[s, 0:lc, :] = ck_ref[s].astype(BF16)
        kcat[s, lc:, :] = k_ref[new, :].astype(BF16)
        vcat[s, 0:lc, :] = cv_ref[s].astype(BF16)
        vcat[s, lc:, :] = v_ref[new, :].astype(BF16)

    _qk_scores(q_ref, kcat.at[0], s_buf.at[0], 0, t, 0, n_k)
    for s in range(n_seq + 1):
        if s + 1 < n_seq:
            _qk_scores(q_ref, kcat.at[s + 1], s_buf.at[s + 1], (s + 1) * t, t, 0, n_k)
        if s < n_seq:
            _softmax_weights(s_buf.at[s], bias_ref, None, p_buf.at[s])
        if s >= 1:
            _weighted_values(p_buf.at[s - 1], vcat.at[s - 1], o_ref, (s - 1) * t, t, 0, n_k)


def _attn_sample_call(z, cache_k, cache_v, table, layer, batch, t, a_width):
    lc = cache_k.shape[2]
    n_seq = ATTN_STEP_SEQS if batch % ATTN_STEP_SEQS == 0 else 1
    rows = A_HEADS * t
    blk = (n_seq * t, a_width)
    cblk = (None, n_seq, lc, a_width)
    cidx = lambda b: (layer, b, 0, 0)
    return pl.pallas_call(
        functools.partial(_attn_sample_kernel, n_seq),
        grid=(batch // n_seq,),
        in_specs=[pl.BlockSpec(memory_space=pltpu.SMEM),
                  pl.BlockSpec(blk, lambda b: (b, 0)), pl.BlockSpec(blk, lambda b: (b, 1)),
                  pl.BlockSpec(blk, lambda b: (b, 2)),
                  pl.BlockSpec(cblk, cidx), pl.BlockSpec(cblk, cidx)],
        out_specs=pl.BlockSpec(blk, lambda b: (b, 0)),
        out_shape=jax.ShapeDtypeStruct((batch * t, a_width), BF16),
        scratch_shapes=[pltpu.VMEM((n_seq, lc + t, a_width), BF16), pltpu.VMEM((n_seq, lc + t, a_width), BF16),
                        pltpu.VMEM((rows, lc + t), F32), pltpu.VMEM((n_seq, rows, lc + t), F32),
                        pltpu.VMEM((n_seq, rows, lc + t), BF16)],
        compiler_params=pltpu.CompilerParams(dimension_semantics=("arbitrary",),
                                             vmem_limit_bytes=V7X_VMEM_LIMIT),
        name="band_attention_sample",
    )(table, z, z, z, cache_k, cache_v)


def _split2(x):
    a = x.astype(BF16)
    return a, (x - a.astype(F32)).astype(BF16)


_RecurRefs = collections.namedtuple(
    "_RecurRefs", "u q f i g lbraw normw poolw pscale out state hist dmask inter rows_b rows_k")


def _build_decay_masks(dmask, blk, tm):
    sb = min(SUB_BLOCK, blk)
    ri = lax.broadcasted_iota(jnp.int32, (tm, tm), 0)
    ci = lax.broadcasted_iota(jnp.int32, (tm, tm), 1)
    same_sb = _div(ri, sb) == _div(ci, sb)
    dmask[0] = jnp.where(same_sb & (ci <= ri), 1.0, 0.0).astype(BF16)
    dmask[1] = jnp.where(same_sb, 1.0, 0.0).astype(BF16)


def _recur_tile(layer, blk, pos0, n_heads, R, rows, t_tile, fallbacks):
    tm = rows.stop - rows.start
    width = R.u.shape[1]
    hd = width // n_heads
    sb = min(SUB_BLOCK, blk)
    assert blk // sb in (1, 2) and tm % blk == 0

    raw = R.lbraw[...]
    sm = jnp.exp(raw - jnp.max(raw, axis=0, keepdims=True))
    sm = sm / jnp.sum(sm, axis=0, keepdims=True)
    lb = jnp.zeros((1, width), F32)
    for j in range(1, layer + 1):
        lb = lb + sm[j:j + 1, :]
    forget = lb + (1.0 - lb) * _sigmoid(R.f[rows, :])
    log_f = jnp.log(forget)
    k_in = 1.0 - forget
    qx = R.q[rows, :]
    qf = qx * _sigmoid(qx)
    vb = R.i[rows, :].astype(BF16)

    lf2 = jnp.concatenate(_split2(log_f), axis=1)

    def decay_sum(mask01):
        r = _dot(mask01, lf2)
        return r[:, 0:width] + r[:, width:]

    b_rel = decay_sum(R.dmask[0])
    sb_tot = decay_sum(R.dmask[1])
    yield
    if blk > sb:
        second = (lax.broadcasted_iota(jnp.int32, (tm, 1), 0) & (blk - 1)) >= sb
        prev_tot = pltpu.roll(sb_tot, sb, 0)
        next_tot = pltpu.roll(sb_tot, tm - sb, 0)
        b = b_rel + jnp.where(second, prev_tot, 0.0)
        blk_tot = sb_tot + jnp.where(second, prev_tot, next_tot)
    else:
        b, blk_tot = b_rel, sb_tot

    ext = jnp.concatenate([R.hist[...], R.u[rows, :]], axis=0)
    R.hist[...] = ext[tm:, :]
    s2 = ext + pltpu.roll(ext, 1, 0)
    s4 = s2 + pltpu.roll(s2, 2, 0)
    s8 = s4 + pltpu.roll(s4, 4, 0)
    s16 = s8 + pltpu.roll(s8, 8, 0)
    lane = lax.broadcasted_iota(jnp.int32, (1, width), 1)
    grp = _div(lane, width // len(POOL_WINDOWS))
    wsum = jnp.where(grp == 0, s2, jnp.where(grp == 1, s4, jnp.where(grp == 2, s8, s16)))
    wlen = jnp.where(grp == 0, 2.0, jnp.where(grp == 1, 4.0, jnp.where(grp == 2, 8.0, 16.0)))
    row = lax.broadcasted_iota(jnp.int32, (POOL_PAD + tm, 1), 0)
    pos = (row + (pos0 - POOL_PAD) + t_tile * tm).astype(F32)
    cnt = jnp.maximum(jnp.minimum(pos + 1.0, wlen), 1.0)
    dev = (wsum / cnt - ext)[POOL_PAD:, :]
    o_pool = _dot(dev.astype(BF16), R.poolw[...]) * R.pscale[...]
    R.out[rows, 0:width] = o_pool.astype(R.out.dtype)

    q_rel = qf * jnp.exp(b_rel)
    k_rel = (k_in * jnp.exp(-b_rel)).astype(BF16)
    k_end = (k_in * jnp.exp(sb_tot - b_rel)).astype(BF16)
    q_abs = (qf * jnp.exp(b)).astype(BF16)
    k_tail = (k_in * jnp.exp(blk_tot - b)).astype(BF16)

    hb = n_heads * blk
    n_chunks = tm // blk
    stack_head = _div(lax.broadcasted_iota(jnp.int32, (hb, 1), 0), blk) == _div(lane, hd)
    tq = lax.broadcasted_iota(jnp.int32, (hb, blk), 0) & (blk - 1)
    ts = lax.broadcasted_iota(jnp.int32, (hb, blk), 1)
    m_intra = (_div(tq, sb) == _div(ts, sb)) & (ts <= tq)
    m_cross = _div(ts, sb) < _div(tq, sb)
    bd = (_div(lax.broadcasted_iota(jnp.int32, (width, width), 0), hd)
          == _div(lax.broadcasted_iota(jnp.int32, (width, width), 1), hd))
    ones_bd = jnp.where(bd, 1.0, 0.0).astype(BF16)
    gx = R.g[rows, :]
    out_gate = R.normw[...] * (gx * _sigmoid(gx))

    def write_output(o):
        sq_hi, sq_lo = _split2(o * o)
        ms = (_dot(sq_hi, ones_bd) + _dot(sq_lo, ones_bd)) * (1.0 / hd)
        R.out[rows, width:] = (o * lax.rsqrt(ms + RMS_EPS) * out_gate).astype(R.out.dtype)

    chunks = [slice(c * blk, (c + 1) * blk) for c in range(n_chunks)]
    state_in = [jnp.where(bd, _dot_tn(vb[rs], k_tail[rs]), 0.0) for rs in chunks]
    yield
    two_sets = blk > sb
    if two_sets:
        tq2 = lax.broadcasted_iota(jnp.int32, (hb, 2 * blk), 0) & (blk - 1)
        col = lax.broadcasted_iota(jnp.int32, (hb, 2 * blk), 1)
        ts2 = col & (blk - 1)
        keep = (((col < blk) & (_div(tq2, sb) == _div(ts2, sb)) & (ts2 <= tq2))
                | ((col >= blk) & (_div(ts2, sb) < _div(tq2, sb))))
    else:
        keep = m_intra
    scores = []
    for rs in chunks:
        lhs = jnp.where(stack_head, jnp.concatenate([q_rel[rs]] * n_heads, axis=0), 0.0).astype(BF16)
        keys = jnp.concatenate([k_rel[rs], k_end[rs]], axis=0) if two_sets else k_rel[rs]
        scores.append(jnp.where(keep, _dot_nt(lhs, keys), 0.0).astype(BF16))
    yield
    outs = []
    st = R.state[...]
    for c, rs in enumerate(chunks):
        from_state = _dot_nt(q_abs[rs], st.astype(BF16))
        values = jnp.concatenate([vb[rs], vb[rs]], axis=0) if two_sets else vb[rs]
        stacked = jnp.where(stack_head, _dot(scores[c], values), 0.0)
        o_c = stacked[0:blk]
        for h in range(1, n_heads):
            o_c = o_c + stacked[h * blk:(h + 1) * blk]
        R.inter[pl.ds(rows.start + c * blk, blk), :] = from_state
        outs.append(o_c + from_state)
        decay_end = jnp.exp(blk_tot[c * blk:c * blk + 1, :])
        st = st * decay_end + state_in[c]
        yield
    R.state[...] = st
    write_output(outs[0] if len(outs) == 1 else jnp.concatenate(outs, axis=0))

    def fallback():
        @pl.when(jnp.min(sb_tot) <= DECAY_SAFE_LOG)
        def _():
            R.rows_b[...] = b
            R.rows_k[...] = k_in
            t_row = lax.broadcasted_iota(jnp.int32, (blk, 1), 0)

            def add_key_row(s, acc):
                parts = []
                for c, rs in enumerate(chunks):
                    r = c * blk + s
                    rel = jnp.minimum(b[rs] - R.rows_b[pl.ds(r, 1), :], 0.0)
                    term = jnp.where(t_row >= s, qf[rs] * jnp.exp(rel) * R.rows_k[pl.ds(r, 1), :], 0.0)
                    parts.append(_dot(term.astype(BF16), ones_bd) * R.i[pl.ds(rows.start + r, 1), :])
                return acc + (parts[0] if n_chunks == 1 else jnp.concatenate(parts, axis=0))

            write_output(lax.fori_loop(0, blk, add_key_row, R.inter[rows, :]))

    fallbacks.append(fallback)
    yield


def _run_staggered(chains, start_rounds):
    live = list(zip(start_rounds, chains))
    rnd = 0
    while live:
        for item in list(live):
            start, ch = item
            if rnd >= start:
                try:
                    next(ch)
                except StopIteration:
                    live.remove(item)
        rnd += 1


def _recur_kernel(layer, blk, pos0, n_heads, rtm, n_seq, u_ref, q_ref, f_ref, i_ref, g_ref, hist0_ref, st0_ref,
                  lbraw_ref, normw_ref, poolw_ref, pscale_ref, o_ref, st_ref, state, hist, dmask, inter,
                  rows_b, rows_k):
    t_idx = pl.program_id(1)
    seq_rows = u_ref.shape[0] // n_seq
    tiles = seq_rows // rtm

    @pl.when(t_idx == 0)
    def _():
        state[...] = st0_ref[...]
        hist[...] = hist0_ref[...]
        _build_decay_masks(dmask, blk, rtm)

    fallbacks, chains, starts = [], [], []
    for s in range(n_seq):
        R = _RecurRefs(u_ref, q_ref, f_ref, i_ref, g_ref, lbraw_ref, normw_ref, poolw_ref, pscale_ref, o_ref,
                       state.at[s], hist.at[s], dmask, inter, rows_b, rows_k)
        for k in range(tiles):
            r0 = s * seq_rows + k * rtm
            chains.append(_recur_tile(layer, blk, pos0, n_heads, R, slice(r0, r0 + rtm), t_idx * tiles + k, fallbacks))
            starts.append(k * (rtm // blk))
    _run_staggered(chains, starts)
    for fb in fallbacks:
        fb()

    @pl.when(t_idx == pl.num_programs(1) - 1)
    def _():
        st_ref[...] = state[...]


RECUR_TILE = 256
RECUR_STEP_ROWS = 1024
RECUR_STEP_SEQS = 16


def _recur_scratch(n_seq, rows, tm, width):
    return [pltpu.VMEM((n_seq, width, width), F32), pltpu.VMEM((n_seq, POOL_PAD, width), F32),
            pltpu.VMEM((2, tm, tm), BF16), pltpu.VMEM((rows, width), F32), pltpu.VMEM((tm, width), F32),
            pltpu.VMEM((tm, width), F32)]


def _recur_call(z, hist0, st0, lbraw, normw, poolw, pscale, layer, state_layer, batch, seq, blk, pos0, col0):
    width = st0.shape[-1]
    rtm = min(RECUR_TILE, seq)
    tm = min(RECUR_STEP_ROWS, seq)
    n_seq = 1 if tm < seq else min(RECUR_STEP_SEQS, max(1, RECUR_STEP_ROWS // seq), batch)
    assert seq % tm == 0 and tm % rtm == 0 and rtm % blk == 0 and batch % n_seq == 0
    nt = seq // tm
    col = lambda c: (lambda b, t: (b * nt + t, c))
    per_batch = lambda b, t: (state_layer, b, 0, 0)
    consts = (normw, poolw, pscale)
    return pl.pallas_call(
        functools.partial(_recur_kernel, layer, blk, pos0, C_HEADS, rtm, n_seq),
        grid=(batch // n_seq, nt),
        in_specs=[pl.BlockSpec((n_seq * tm, width), col(col0 + j)) for j in range(5)]
        + [pl.BlockSpec((None, n_seq, POOL_PAD, width), per_batch),
           pl.BlockSpec((None, n_seq, width, width), per_batch),
           pl.BlockSpec(lbraw.shape, lambda b, t: (0, 0), pipeline_mode=pl.Buffered(1))]
        + [_layer_spec(c, layer) for c in consts],
        out_specs=[pl.BlockSpec((n_seq * tm, 2 * width), lambda b, t: (b * nt + t, 0)),
                   pl.BlockSpec((n_seq, width, width), lambda b, t: (b, 0, 0))],
        out_shape=[jax.ShapeDtypeStruct((batch * seq, 2 * width), BF16),
                   jax.ShapeDtypeStruct((batch, width, width), F32)],
        scratch_shapes=_recur_scratch(n_seq, n_seq * tm, rtm, width),
        compiler_params=pltpu.CompilerParams(dimension_semantics=("parallel", "arbitrary"),
                                             vmem_limit_bytes=V7X_VMEM_LIMIT),
        name="pool_hgrn_mixer",
    )(z, z, z, z, z, hist0, st0, lbraw, *consts)


def _block_diag(blocks):
    g = blocks.shape[-3]
    zero = jnp.zeros_like(blocks[..., 0, :, :])
    rows = [jnp.concatenate([blocks[..., h, :, :] if j == h else zero for j in range(g)], axis=-1)
            for h in range(g)]
    return jnp.concatenate(rows, axis=-2)


def _state_from_kernel(st, n_heads):
    b, w, _ = st.shape
    hd = w // n_heads
    blocks = jnp.stack([st[:, h * hd:(h + 1) * hd, h * hd:(h + 1) * hd] for h in range(n_heads)], axis=1)
    return jnp.swapaxes(blocks, -1, -2)


def kernel(x_prompt, x_sample, p_prompt, p_sample, cache_attn_k, cache_attn_v, state_pool, state_hgrn, ffn1_w_gu, ffn1_w_down, w_in, attn_rel_bias, pool_w, pool_scale, hgrn_lower_bounds, hgrn_norm_w, w_out, ffn2_w_gu, ffn2_w_down, ple_w_gate, ple_w_proj, ln_g, ln_b):
    depth = w_in.shape[0]
    alpha = float((2 * depth) ** 0.25)
    nb, seq, d = x_prompt.shape
    db, dt, _ = x_sample.shape
    a_width = A_HEADS * HEAD_DIM
    c_width = C_HEADS * HEAD_DIM
    in_width = w_in.shape[-1]
    col0 = 3 * a_width // c_width
    past_rows = min(A_BACK * CHUNK, seq)
    assert dt >= POOL_HIST and seq >= POOL_HIST and PAST_LEN >= POOL_HIST

    bf = lambda w: w.astype(BF16)
    w1gu, w1dn, win, wout = bf(ffn1_w_gu), bf(ffn1_w_down), bf(w_in), bf(w_out)
    w2gu, w2dn, wg, wp = bf(ffn2_w_gu), bf(ffn2_w_down), bf(ple_w_gate), bf(ple_w_proj)
    poolw = bf(_block_diag(pool_w))
    pscale = pool_scale.astype(F32)[:, None, :]
    normw = jnp.tile(hgrn_norm_w.astype(F32), (1, C_HEADS))[:, None, :]
    lbraw = hgrn_lower_bounds.astype(F32)
    pp = p_prompt.reshape(depth, nb * seq, -1)
    ps = p_sample.reshape(depth, db * dt, -1)

    zero_hist = jnp.zeros((1, nb, POOL_PAD, c_width), F32)
    zero_state = jnp.zeros((1, nb, c_width, c_width), F32)
    hist_s = jnp.pad(state_pool.astype(F32), ((0, 0), (0, 0), (POOL_PAD - POOL_HIST, 0), (0, 0)))
    state_s = _block_diag(jnp.swapaxes(state_hgrn.astype(F32), -1, -2))

    lc = cache_attn_k.shape[2]
    cache_k = cache_attn_k.reshape(depth, db, lc, a_width)
    cache_v = cache_attn_v.reshape(depth, db, lc, a_width)

    xp = x_prompt.reshape(nb * seq, d)
    xs = x_sample.reshape(db * dt, d)
    outs = [[] for _ in range(8)]

    for i in range(depth):
        rel_table = attn_rel_bias[i].astype(F32)

        xp, zp = _pre_call(xp, w1gu, w1dn, ln_g, ln_b, win, i, alpha)
        oa = _attn_prompt_call(zp, rel_table, nb, seq, a_width)
        obc, st = _recur_call(zp, zero_hist, zero_state, lbraw, normw, poolw, pscale, i, 0, nb, seq, CHUNK, 0, col0)
        xp = _post_call(xp, oa, obc, pp, wout, w2gu, w2dn, wg, wp, ln_g, ln_b, i, alpha)
        zp3 = zp.reshape(nb, seq, in_width)
        outs[0].append(zp3[:, seq - past_rows:, a_width:2 * a_width].reshape(nb, past_rows, A_HEADS, HEAD_DIM))
        outs[1].append(zp3[:, seq - past_rows:, 2 * a_width:3 * a_width].reshape(nb, past_rows, A_HEADS, HEAD_DIM))
        outs[2].append(zp3[:, seq - POOL_HIST:, 3 * a_width:3 * a_width + c_width])
        outs[3].append(_state_from_kernel(st, C_HEADS))

        xs, zs = _pre_call(xs, w1gu, w1dn, ln_g, ln_b, win, i, alpha)
        oa = _attn_sample_call(zs, cache_k, cache_v, rel_table, i, db, dt, a_width)
        obc, st = _recur_call(zs, hist_s, state_s, lbraw, normw, poolw, pscale, i, i, db, dt, dt, PAST_LEN, col0)
        xs = _post_call(xs, oa, obc, ps, wout, w2gu, w2dn, wg, wp, ln_g, ln_b, i, alpha)
        zs3 = zs.reshape(db, dt, in_width)
        outs[4].append(zs3[:, :, a_width:2 * a_width].reshape(db, dt, A_HEADS, HEAD_DIM))
        outs[5].append(zs3[:, :, 2 * a_width:3 * a_width].reshape(db, dt, A_HEADS, HEAD_DIM))
        outs[6].append(zs3[:, dt - POOL_HIST:, 3 * a_width:3 * a_width + c_width])
        outs[7].append(_state_from_kernel(st, C_HEADS))

    stacked = [jnp.stack(o, axis=0) for o in outs]
    return (xp.reshape(nb, seq, d), xs.reshape(db, dt, d), *stacked)
```

```python
import collections
import functools

import jax
import jax.numpy as jnp
from jax import lax
from jax.experimental import pallas as pl
from jax.experimental.pallas import tpu as pltpu

F32 = jnp.float32
BF16 = jnp.bfloat16

PAST_LEN = 2048
CHUNK = 64
A_BACK = 8
HEAD_DIM = 64
A_HEADS = 8
C_HEADS = 4
REL_MAX = 128
POOL_WINDOWS = (2, 4, 8, 16)
POOL_HIST = max(POOL_WINDOWS) - 1
POOL_PAD = POOL_HIST + 1
SUB_BLOCK = 32
DECAY_SAFE_LOG = -60.0
TOKEN_SUB_ROWS = 256
ATTN_STEP_SEQS = 4
ATTN_Q_BLOCK_PASTS = 2
LN_EPS = 1e-5
RMS_EPS = 1e-6
NEG_INF = -1e30
LOG2E = 1.4426950408889634

V7X_LANES = 128
V7X_MXU_WIDTH = 256
V7X_VMEM_LIMIT = 56 * 1024 * 1024


def _dot(a, b):
    return jnp.dot(a, b, preferred_element_type=F32)


def _dot_nt(a, b):
    return lax.dot_general(a, b, (((1,), (1,)), ((), ())), preferred_element_type=F32)


def _dot_tn(a, b):
    return lax.dot_general(a, b, (((0,), (0,)), ((), ())), preferred_element_type=F32)


def _layer_norm(y, g, b):
    mu = jnp.mean(y, axis=-1, keepdims=True)
    d = y - mu
    var = jnp.mean(d * d, axis=-1, keepdims=True)
    return d * lax.rsqrt(var + LN_EPS) * g + b


def _sigmoid(x):
    return 1.0 / (1.0 + jnp.exp(-x))


def _log2(n):
    assert n > 0 and n & (n - 1) == 0, n
    return n.bit_length() - 1


def _div(x, n):
    return x >> _log2(n)


def _swiglu_stages(xb, wgu_ref, wdn_ref, n_split):
    hidden = wdn_ref.shape[0]
    groups = _hidden_groups(hidden, n_split)
    gate_up = []
    for lo, hi in groups:
        gate_up.append((_dot(xb, wgu_ref[:, lo:hi]), _dot(xb, wgu_ref[:, hidden + lo:hidden + hi])))
        yield
    acc = None
    for (lo, hi), (gate, up) in zip(groups, gate_up):
        act = (gate * _sigmoid(gate) * up).astype(BF16)
        part = _dot(act, wdn_ref[lo:hi, :])
        acc = part if acc is None else acc + part
        yield
    return acc


def _hidden_groups(hidden, n_split):
    if hidden % V7X_MXU_WIDTH:
        return [(0, hidden)]
    tiles = hidden // V7X_MXU_WIDTH
    bounds = [V7X_MXU_WIDTH * ((tiles * g + n_split - 1) // n_split) for g in range(n_split + 1)]
    return [(lo, hi) for lo, hi in zip(bounds[:-1], bounds[1:]) if hi > lo]


def _run_interleaved(chains):
    live = list(chains)
    while live:
        for ch in list(live):
            if ch not in live:
                continue
            try:
                next(ch)
            except StopIteration:
                live = [c for c in live if c is not ch]


def _sub_tiles(n_rows):
    sub = min(TOKEN_SUB_ROWS, n_rows)
    return [slice(r, r + sub) for r in range(0, n_rows, sub)]


def _pre_kernel(alpha, n_split, x_ref, wgu_ref, wdn_ref, lng_ref, lnb_ref, win_ref, xo_ref, z_ref):
    def chain(rows):
        x = x_ref[rows, :]
        ffn = yield from _swiglu_stages(x.astype(BF16), wgu_ref, wdn_ref, n_split)
        xn = _layer_norm(alpha * x + 0.5 * ffn, lng_ref[0:1, :], lnb_ref[0:1, :])
        xo_ref[rows, :] = xn
        z_ref[rows, :] = _dot(xn.astype(BF16), win_ref[...])
        yield

    _run_interleaved([chain(rows) for rows in _sub_tiles(x_ref.shape[0])])


def _post_chain(alpha, n_split, rows, x_ref, oa_ref, obc_ref, p_ref, wout_ref, wgu_ref, wdn_ref, wg_ref, wp_ref,
                lng_ref, lnb_ref, xo_ref):
    mixed = _dot(jnp.concatenate([oa_ref[rows, :], obc_ref[rows, :]], axis=1), wout_ref[...])
    proj = _dot(p_ref[rows, :].astype(BF16), wp_ref[...])
    yield
    x = _layer_norm(alpha * x_ref[rows, :] + mixed, lng_ref[1:2, :], lnb_ref[1:2, :])
    ffn = yield from _swiglu_stages(x.astype(BF16), wgu_ref, wdn_ref, n_split)
    x = _layer_norm(alpha * x + 0.5 * ffn, lng_ref[2:3, :], lnb_ref[2:3, :])
    emb = _sigmoid(_dot(x.astype(BF16), wg_ref[...])) * proj
    yield
    xo_ref[rows, :] = _layer_norm(alpha * x + emb, lng_ref[3:4, :], lnb_ref[3:4, :])


def _post_kernel(alpha, n_split, x_ref, oa_ref, obc_ref, p_ref, wout_ref, wgu_ref, wdn_ref, wg_ref, wp_ref,
                 lng_ref, lnb_ref, xo_ref):
    _run_interleaved([_post_chain(alpha, n_split, rows, x_ref, oa_ref, obc_ref, p_ref, wout_ref, wgu_ref, wdn_ref,
                                  wg_ref, wp_ref, lng_ref, lnb_ref, xo_ref) for rows in _sub_tiles(x_ref.shape[0])])


def _layer_spec(stacked, layer):
    rest = stacked.shape[1:]
    return pl.BlockSpec((None,) + rest, lambda *_: (layer,) + (0,) * len(rest), pipeline_mode=pl.Buffered(1))


def _token_tile(n):
    for tm in (2 * TOKEN_SUB_ROWS, TOKEN_SUB_ROWS, 128, 64, 32, 16, 8):
        if n % tm == 0:
            return tm
    raise ValueError(f"unsupported token count {n}")


def _pre_call(x, wgu, wdn, lng, lnb, win, layer, alpha):
    n, d = x.shape
    zw = win.shape[-1]
    tm = _token_tile(n)
    row = lambda i: (i, 0)
    consts = (wgu, wdn, lng, lnb, win)
    return pl.pallas_call(
        functools.partial(_pre_kernel, alpha, 2),
        grid=(n // tm,),
        in_specs=[pl.BlockSpec((tm, d), row)] + [_layer_spec(c, layer) for c in consts],
        out_specs=[pl.BlockSpec((tm, d), row), pl.BlockSpec((tm, zw), row)],
        out_shape=[jax.ShapeDtypeStruct((n, d), F32), jax.ShapeDtypeStruct((n, zw), F32)],
        compiler_params=pltpu.CompilerParams(dimension_semantics=("parallel",),
                                             vmem_limit_bytes=V7X_VMEM_LIMIT),
        name="pre_mixer_tokens",
    )(x, *consts)


def _post_call(x, oa, obc, p, wout, wgu, wdn, wg, wp, lng, lnb, layer, alpha):
    n, d = x.shape
    tm = _token_tile(n)
    row = lambda i: (i, 0)
    consts = (wout, wgu, wdn, wg, wp, lng, lnb)
    return pl.pallas_call(
        functools.partial(_post_kernel, alpha, 2),
        grid=(n // tm,),
        in_specs=[pl.BlockSpec((tm, d), row), pl.BlockSpec((tm, oa.shape[1]), row),
                  pl.BlockSpec((tm, obc.shape[1]), row),
                  pl.BlockSpec((None, tm, p.shape[-1]), lambda i: (layer, i, 0))]
        + [_layer_spec(c, layer) for c in consts],
        out_specs=pl.BlockSpec((tm, d), row),
        out_shape=jax.ShapeDtypeStruct((n, d), F32),
        compiler_params=pltpu.CompilerParams(dimension_semantics=("parallel",),
                                             vmem_limit_bytes=V7X_VMEM_LIMIT),
        name="post_mixer_tokens",
    )(x, oa, obc, p, *consts)


def _qk_scores(q_ref, kcat, s_out, r0, n_q, k0, n_k):
    scale = HEAD_DIM ** -0.5 * LOG2E
    low = lax.broadcasted_iota(jnp.int32, (1, V7X_LANES), 1) < HEAD_DIM
    for hp in range(q_ref.shape[1] // V7X_LANES):
        ls = slice(hp * V7X_LANES, (hp + 1) * V7X_LANES)
        q = q_ref[r0:r0 + n_q, ls] * scale
        lhs = jnp.concatenate([jnp.where(low, q, 0.0), jnp.where(low, 0.0, q)], axis=0).astype(BF16)
        s_out[2 * hp * n_q:2 * (hp + 1) * n_q, :] = _dot_nt(lhs, kcat[k0:k0 + n_k, ls])


def _softmax_weights(s_in, bias_ref, key_valid, p_out):
    s = s_in[...] + bias_ref[...]
    if key_valid is not None:
        s = jnp.where(key_valid, s, NEG_INF)
    p_out[...] = jnp.exp2(s - jnp.max(s, axis=-1, keepdims=True)).astype(BF16)


def _weighted_values(p_in, vcat, o_ref, r0, n_q, k0, n_k):
    low = lax.broadcasted_iota(jnp.int32, (1, V7X_LANES), 1) < HEAD_DIM
    ones = jnp.ones((n_k, V7X_LANES), BF16)
    for hp in range(o_ref.shape[1] // V7X_LANES):
        ls = slice(hp * V7X_LANES, (hp + 1) * V7X_LANES)
        rows = slice(2 * hp * n_q, 2 * (hp + 1) * n_q)
        pv = _dot(p_in[rows, :], jnp.concatenate([vcat[k0:k0 + n_k, ls], ones], axis=1))
        o2 = pv[:, :V7X_LANES] * (1.0 / pv[:, V7X_LANES:])
        o_ref[r0:r0 + n_q, ls] = jnp.where(low, o2[:n_q], o2[n_q:]).astype(o_ref.dtype)


def _build_rel_bias(table_ref, bias_ref, n_q, n_k, offset):
    n_heads = table_ref.shape[0]
    clip = lambda d: min(max(d, -(CHUNK - 1)), REL_MAX) + (CHUNK - 1)
    heads_per_pass = 4
    for j0 in range(0, n_k, V7X_LANES):
        jw = min(V7X_LANES, n_k - j0)
        r_lo, r_hi = clip(offset - (j0 + jw - 1)), clip(offset + n_q - 1 - j0)
        d = (offset - j0 + lax.broadcasted_iota(jnp.int32, (n_q, jw), 0)
             - lax.broadcasted_iota(jnp.int32, (n_q, jw), 1))
        idx = jnp.clip(d, -(CHUNK - 1), REL_MAX) + (CHUNK - 1)
        for h0 in range(0, n_heads, heads_per_pass):
            heads = range(h0, min(h0 + heads_per_pass, n_heads))
            if r_lo == r_hi:
                vals = [jnp.full((n_q, jw), table_ref[h, r_lo], F32) for h in heads]
            else:
                def pick(r, acc, heads=heads, idx=idx):
                    hit = idx == r
                    return tuple(jnp.where(hit, table_ref[h, r], a) for h, a in zip(heads, acc))
                vals = lax.fori_loop(r_lo, r_hi + 1, pick, tuple(jnp.zeros((n_q, jw), F32) for _ in heads))
            for h, v in zip(heads, vals):
                bias_ref[h * n_q:(h + 1) * n_q, j0:j0 + jw] = v * LOG2E


def _attn_prompt_kernel(qb_rows, table_ref, q_ref, kp_ref, kc_ref, vp_ref, vc_ref, o_ref,
                        kcat, vcat, bias_ref, s_buf, p_buf):
    i = pl.program_id(1)
    band = (A_BACK + 1) * CHUNK
    past = A_BACK * CHUNK
    n_chunks = qb_rows // CHUNK

    @pl.when((pl.program_id(0) == 0) & (i == 0))
    def _():
        _build_rel_bias(table_ref, bias_ref, CHUNK, band, past)

    def cast_past(j, carry):
        r = pl.multiple_of(j * CHUNK, CHUNK)
        kcat[pl.ds(r, CHUNK), :] = kp_ref[pl.ds(r, CHUNK), :].astype(BF16)
        vcat[pl.ds(r, CHUNK), :] = vp_ref[pl.ds(r, CHUNK), :].astype(BF16)
        return carry

    def cast_own(j, carry):
        r = pl.multiple_of(j * CHUNK, CHUNK)
        kcat[pl.ds(past + r, CHUNK), :] = kc_ref[pl.ds(r, CHUNK), :].astype(BF16)
        vcat[pl.ds(past + r, CHUNK), :] = vc_ref[pl.ds(r, CHUNK), :].astype(BF16)
        return carry

    lax.fori_loop(0, past // CHUNK, cast_past, 0)
    lax.fori_loop(0, n_chunks, cast_own, 0)
    key_idx = lax.broadcasted_iota(jnp.int32, (1, band), 1)

    _qk_scores(q_ref, kcat, s_buf.at[0], 0, CHUNK, 0, band)
    for c in range(n_chunks + 1):
        if c + 1 < n_chunks:
            _qk_scores(q_ref, kcat, s_buf.at[(c + 1) % 2], (c + 1) * CHUNK, CHUNK, (c + 1) * CHUNK, band)
        if c < n_chunks:
            key_valid = key_idx >= past - c * CHUNK - i * qb_rows
            _softmax_weights(s_buf.at[c % 2], bias_ref, key_valid, p_buf.at[c % 2])
        if c >= 1:
            _weighted_values(p_buf.at[(c - 1) % 2], vcat, o_ref,
                             (c - 1) * CHUNK, CHUNK, (c - 1) * CHUNK, band)


def _attn_prompt_call(z, table, batch, seq, a_width):
    past = A_BACK * CHUNK
    qb_rows = next(m * past for m in (ATTN_Q_BLOCK_PASTS, 1) if seq % (m * past) == 0)
    assert a_width % V7X_LANES == 0
    nq = seq // qb_rows
    per_q = qb_rows // past
    band = (A_BACK + 1) * CHUNK
    rows = A_HEADS * CHUNK
    cur = lambda col: (lambda b, i: (b * nq + i, col))
    prev = lambda col: (lambda b, i: (b * nq * per_q + jnp.maximum(i * per_q - 1, 0), col))
    blk = (qb_rows, a_width)
    pblk = (past, a_width)
    return pl.pallas_call(
        functools.partial(_attn_prompt_kernel, qb_rows),
        grid=(batch, nq),
        in_specs=[pl.BlockSpec(memory_space=pltpu.SMEM),
                  pl.BlockSpec(blk, cur(0)), pl.BlockSpec(pblk, prev(1)), pl.BlockSpec(blk, cur(1)),
                  pl.BlockSpec(pblk, prev(2)), pl.BlockSpec(blk, cur(2))],
        out_specs=pl.BlockSpec(blk, lambda b, i: (b * nq + i, 0)),
        out_shape=jax.ShapeDtypeStruct((batch * seq, a_width), BF16),
        scratch_shapes=[pltpu.VMEM((past + qb_rows, a_width), BF16), pltpu.VMEM((past + qb_rows, a_width), BF16),
                        pltpu.VMEM((rows, band), F32), pltpu.VMEM((2, rows, band), F32),
                        pltpu.VMEM((2, rows, band), BF16)],
        compiler_params=pltpu.CompilerParams(dimension_semantics=("arbitrary", "arbitrary"),
                                             vmem_limit_bytes=V7X_VMEM_LIMIT),
        name="band_attention_prompt",
    )(table, z, z, z, z, z)


def _attn_sample_kernel(n_seq, table_ref, q_ref, k_ref, v_ref, ck_ref, cv_ref, o_ref, kcat, vcat, bias_ref,
                        s_buf, p_buf):
    lc = ck_ref.shape[1]
    t = q_ref.shape[0] // n_seq
    n_k = lc + t

    @pl.when(pl.program_id(0) == 0)
    def _():
        _build_rel_bias(table_ref, bias_ref, t, n_k, lc)

    for s in range(n_seq):
        new = slice(s * t, (s + 1) * t)
        kcat[s, 0:lc, :] = ck_ref[s].astype(BF16)
        kcat[s, lc:, :] = k_ref[new, :].astype(BF16)
        vcat[s, 0:lc, :] = cv_ref[s].astype(BF16)
        vcat[s, lc:, :] = v_ref[new, :].astype(BF16)

    _qk_scores(q_ref, kcat.at[0], s_buf.at[0], 0, t, 0, n_k)
    for s in range(n_seq + 1):
        if s + 1 < n_seq:
            _qk_scores(q_ref, kcat.at[s + 1], s_buf.at[s + 1], (s + 1) * t, t, 0, n_k)
        if s < n_seq:
            _softmax_weights(s_buf.at[s], bias_ref, None, p_buf.at[s])
        if s >= 1:
            _weighted_values(p_buf.at[s - 1], vcat.at[s - 1], o_ref, (s - 1) * t, t, 0, n_k)


def _attn_sample_call(z, cache_k, cache_v, table, layer, batch, t, a_width):
    lc = cache_k.shape[2]
    n_seq = ATTN_STEP_SEQS if batch % ATTN_STEP_SEQS == 0 else 1
    rows = A_HEADS * t
    blk = (n_seq * t, a_width)
    cblk = (None, n_seq, lc, a_width)
    cidx = lambda b: (layer, b, 0, 0)
    return pl.pallas_call(
        functools.partial(_attn_sample_kernel, n_seq),
        grid=(batch // n_seq,),
        in_specs=[pl.BlockSpec(memory_space=pltpu.SMEM),
                  pl.BlockSpec(blk, lambda b: (b, 0)), pl.BlockSpec(blk, lambda b: (b, 1)),
                  pl.BlockSpec(blk, lambda b: (b, 2)),
                  pl.BlockSpec(cblk, cidx), pl.BlockSpec(cblk, cidx)],
        out_specs=pl.BlockSpec(blk, lambda b: (b, 0)),
        out_shape=jax.ShapeDtypeStruct((batch * t, a_width), BF16),
        scratch_shapes=[pltpu.VMEM((n_seq, lc + t, a_width), BF16), pltpu.VMEM((n_seq, lc + t, a_width), BF16),
                        pltpu.VMEM((rows, lc + t), F32), pltpu.VMEM((n_seq, rows, lc + t), F32),
                        pltpu.VMEM((n_seq, rows, lc + t), BF16)],
        compiler_params=pltpu.CompilerParams(dimension_semantics=("arbitrary",),
                                             vmem_limit_bytes=V7X_VMEM_LIMIT),
        name="band_attention_sample",
    )(table, z, z, z, cache_k, cache_v)


def _split2(x):
    a = x.astype(BF16)
    return a, (x - a.astype(F32)).astype(BF16)


_RecurRefs = collections.namedtuple(
    "_RecurRefs", "u q f i g lbraw normw poolw pscale out state hist dmask inter rows_b rows_k")


def _build_decay_masks(dmask, blk, tm):
    sb = min(SUB_BLOCK, blk)
    ri = lax.broadcasted_iota(jnp.int32, (tm, tm), 0)
    ci = lax.broadcasted_iota(jnp.int32, (tm, tm), 1)
    same_sb = _div(ri, sb) == _div(ci, sb)
    dmask[0] = jnp.where(same_sb & (ci <= ri), 1.0, 0.0).astype(BF16)
    dmask[1] = jnp.where(same_sb, 1.0, 0.0).astype(BF16)


def _recur_tile(layer, blk, pos0, n_heads, R, rows, t_tile, fallbacks):
    tm = rows.stop - rows.start
    width = R.u.shape[1]
    hd = width // n_heads
    sb = min(SUB_BLOCK, blk)
    assert blk // sb in (1, 2) and tm % blk == 0

    raw = R.lbraw[...]
    sm = jnp.exp(raw - jnp.max(raw, axis=0, keepdims=True))
    sm = sm / jnp.sum(sm, axis=0, keepdims=True)
    lb = jnp.zeros((1, width), F32)
    for j in range(1, layer + 1):
        lb = lb + sm[j:j + 1, :]
    forget = lb + (1.0 - lb) * _sigmoid(R.f[rows, :])
    log_f = jnp.log(forget)
    k_in = 1.0 - forget
    qx = R.q[rows, :]
    qf = qx * _sigmoid(qx)
    vb = R.i[rows, :].astype(BF16)

    lf2 = jnp.concatenate(_split2(log_f), axis=1)

    def decay_sum(mask01):
        r = _dot(mask01, lf2)
        return r[:, 0:width] + r[:, width:]

    b_rel = decay_sum(R.dmask[0])
    sb_tot = decay_sum(R.dmask[1])
    yield
    if blk > sb:
        second = (lax.broadcasted_iota(jnp.int32, (tm, 1), 0) & (blk - 1)) >= sb
        prev_tot = pltpu.roll(sb_tot, sb, 0)
        next_tot = pltpu.roll(sb_tot, tm - sb, 0)
        b = b_rel + jnp.where(second, prev_tot, 0.0)
        blk_tot = sb_tot + jnp.where(second, prev_tot, next_tot)
    else:
        b, blk_tot = b_rel, sb_tot

    ext = jnp.concatenate([R.hist[...], R.u[rows, :]], axis=0)
    R.hist[...] = ext[tm:, :]
    s2 = ext + pltpu.roll(ext, 1, 0)
    s4 = s2 + pltpu.roll(s2, 2, 0)
    s8 = s4 + pltpu.roll(s4, 4, 0)
    s16 = s8 + pltpu.roll(s8, 8, 0)
    lane = lax.broadcasted_iota(jnp.int32, (1, width), 1)
    grp = _div(lane, width // len(POOL_WINDOWS))
    wsum = jnp.where(grp == 0, s2, jnp.where(grp == 1, s4, jnp.where(grp == 2, s8, s16)))
    wlen = jnp.where(grp == 0, 2.0, jnp.where(grp == 1, 4.0, jnp.where(grp == 2, 8.0, 16.0)))
    row = lax.broadcasted_iota(jnp.int32, (POOL_PAD + tm, 1), 0)
    pos = (row + (pos0 - POOL_PAD) + t_tile * tm).astype(F32)
    cnt = jnp.maximum(jnp.minimum(pos + 1.0, wlen), 1.0)
    dev = (wsum / cnt - ext)[POOL_PAD:, :]
    o_pool = _dot(dev.astype(BF16), R.poolw[...]) * R.pscale[...]
    R.out[rows, 0:width] = o_pool.astype(R.out.dtype)

    q_rel = qf * jnp.exp(b_rel)
    k_rel = (k_in * jnp.exp(-b_rel)).astype(BF16)
    k_end = (k_in * jnp.exp(sb_tot - b_rel)).astype(BF16)
    q_abs = (qf * jnp.exp(b)).astype(BF16)
    k_tail = (k_in * jnp.exp(blk_tot - b)).astype(BF16)

    hb = n_heads * blk
    n_chunks = tm // blk
    stack_head = _div(lax.broadcasted_iota(jnp.int32, (hb, 1), 0), blk) == _div(lane, hd)
    tq = lax.broadcasted_iota(jnp.int32, (hb, blk), 0) & (blk - 1)
    ts = lax.broadcasted_iota(jnp.int32, (hb, blk), 1)
    m_intra = (_div(tq, sb) == _div(ts, sb)) & (ts <= tq)
    m_cross = _div(ts, sb) < _div(tq, sb)
    bd = (_div(lax.broadcasted_iota(jnp.int32, (width, width), 0), hd)
          == _div(lax.broadcasted_iota(jnp.int32, (width, width), 1), hd))
    ones_bd = jnp.where(bd, 1.0, 0.0).astype(BF16)
    gx = R.g[rows, :]
    out_gate = R.normw[...] * (gx * _sigmoid(gx))

    def write_output(o):
        sq_hi, sq_lo = _split2(o * o)
        ms = (_dot(sq_hi, ones_bd) + _dot(sq_lo, ones_bd)) * (1.0 / hd)
        R.out[rows, width:] = (o * lax.rsqrt(ms + RMS_EPS) * out_gate).astype(R.out.dtype)

    chunks = [slice(c * blk, (c + 1) * blk) for c in range(n_chunks)]
    state_in = [jnp.where(bd, _dot_tn(vb[rs], k_tail[rs]), 0.0) for rs in chunks]
    yield
    two_sets = blk > sb
    if two_sets:
        tq2 = lax.broadcasted_iota(jnp.int32, (hb, 2 * blk), 0) & (blk - 1)
        col = lax.broadcasted_iota(jnp.int32, (hb, 2 * blk), 1)
        ts2 = col & (blk - 1)
        keep = (((col < blk) & (_div(tq2, sb) == _div(ts2, sb)) & (ts2 <= tq2))
                | ((col >= blk) & (_div(ts2, sb) < _div(tq2, sb))))
    else:
        keep = m_intra
    scores = []
    for rs in chunks:
        lhs = jnp.where(stack_head, jnp.concatenate([q_rel[rs]] * n_heads, axis=0), 0.0).astype(BF16)
        keys = jnp.concatenate([k_rel[rs], k_end[rs]], axis=0) if two_sets else k_rel[rs]
        scores.append(jnp.where(keep, _dot_nt(lhs, keys), 0.0).astype(BF16))
    yield
    outs = []
    st = R.state[...]
    for c, rs in enumerate(chunks):
        from_state = _dot_nt(q_abs[rs], st.astype(BF16))
        values = jnp.concatenate([vb[rs], vb[rs]], axis=0) if two_sets else vb[rs]
        stacked = jnp.where(stack_head, _dot(scores[c], values), 0.0)
        o_c = stacked[0:blk]
        for h in range(1, n_heads):
            o_c = o_c + stacked[h * blk:(h + 1) * blk]
        R.inter[pl.ds(rows.start + c * blk, blk), :] = from_state
        outs.append(o_c + from_state)
        decay_end = jnp.exp(blk_tot[c * blk:c * blk + 1, :])
        st = st * decay_end + state_in[c]
        yield
    R.state[...] = st
    write_output(outs[0] if len(outs) == 1 else jnp.concatenate(outs, axis=0))

    def fallback():
        @pl.when(jnp.min(sb_tot) <= DECAY_SAFE_LOG)
        def _():
            R.rows_b[...] = b
            R.rows_k[...] = k_in
            t_row = lax.broadcasted_iota(jnp.int32, (blk, 1), 0)

            def add_key_row(s, acc):
                parts = []
                for c, rs in enumerate(chunks):
                    r = c * blk + s
                    rel = jnp.minimum(b[rs] - R.rows_b[pl.ds(r, 1), :], 0.0)
                    term = jnp.where(t_row >= s, qf[rs] * jnp.exp(rel) * R.rows_k[pl.ds(r, 1), :], 0.0)
                    parts.append(_dot(term.astype(BF16), ones_bd) * R.i[pl.ds(rows.start + r, 1), :])
                return acc + (parts[0] if n_chunks == 1 else jnp.concatenate(parts, axis=0))

            write_output(lax.fori_loop(0, blk, add_key_row, R.inter[rows, :]))

    fallbacks.append(fallback)
    yield


def _run_staggered(chains, start_rounds):
    live = list(zip(start_rounds, chains))
    rnd = 0
    while live:
        for item in list(live):
            start, ch = item
            if rnd >= start:
                try:
                    next(ch)
                except StopIteration:
                    live.remove(item)
        rnd += 1


def _recur_kernel(layer, blk, pos0, n_heads, rtm, n_seq, u_ref, q_ref, f_ref, i_ref, g_ref, hist0_ref, st0_ref,
                  lbraw_ref, normw_ref, poolw_ref, pscale_ref, o_ref, st_ref, state, hist, dmask, inter,
                  rows_b, rows_k):
    t_idx = pl.program_id(1)
    seq_rows = u_ref.shape[0] // n_seq
    tiles = seq_rows // rtm

    @pl.when(t_idx == 0)
    def _():
        state[...] = st0_ref[...]
        hist[...] = hist0_ref[...]
        _build_decay_masks(dmask, blk, rtm)

    fallbacks, chains, starts = [], [], []
    for s in range(n_seq):
        R = _RecurRefs(u_ref, q_ref, f_ref, i_ref, g_ref, lbraw_ref, normw_ref, poolw_ref, pscale_ref, o_ref,
                       state.at[s], hist.at[s], dmask, inter, rows_b, rows_k)
        for k in range(tiles):
            r0 = s * seq_rows + k * rtm
            chains.append(_recur_tile(layer, blk, pos0, n_heads, R, slice(r0, r0 + rtm), t_idx * tiles + k, fallbacks))
            starts.append(k * (rtm // blk))
    _run_staggered(chains, starts)
    for fb in fallbacks:
        fb()

    @pl.when(t_idx == pl.num_programs(1) - 1)
    def _():
        st_ref[...] = state[...]


RECUR_TILE = 256
RECUR_STEP_ROWS = 1024
RECUR_STEP_SEQS = 8


def _recur_scratch(n_seq, rows, tm, width):
    return [pltpu.VMEM((n_seq, width, width), F32), pltpu.VMEM((n_seq, POOL_PAD, width), F32),
            pltpu.VMEM((2, tm, tm), BF16), pltpu.VMEM((rows, width), F32), pltpu.VMEM((tm, width), F32),
            pltpu.VMEM((tm, width), F32)]


def _recur_call(z, hist0, st0, lbraw, normw, poolw, pscale, layer, state_layer, batch, seq, blk, pos0, col0):
    width = st0.shape[-1]
    rtm = min(RECUR_TILE, seq)
    tm = min(RECUR_STEP_ROWS, seq)
    n_seq = 1 if tm < seq else min(RECUR_STEP_SEQS, max(1, RECUR_STEP_ROWS // seq), batch)
    assert seq % tm == 0 and tm % rtm == 0 and rtm % blk == 0 and batch % n_seq == 0
    nt = seq // tm
    col = lambda c: (lambda b, t: (b * nt + t, c))
    per_batch = lambda b, t: (state_layer, b, 0, 0)
    consts = (normw, poolw, pscale)
    return pl.pallas_call(
        functools.partial(_recur_kernel, layer, blk, pos0, C_HEADS, rtm, n_seq),
        grid=(batch // n_seq, nt),
        in_specs=[pl.BlockSpec((n_seq * tm, width), col(col0 + j)) for j in range(5)]
        + [pl.BlockSpec((None, n_seq, POOL_PAD, width), per_batch),
           pl.BlockSpec((None, n_seq, width, width), per_batch),
           pl.BlockSpec(lbraw.shape, lambda b, t: (0, 0), pipeline_mode=pl.Buffered(1))]
        + [_layer_spec(c, layer) for c in consts],
        out_specs=[pl.BlockSpec((n_seq * tm, 2 * width), lambda b, t: (b * nt + t, 0)),
                   pl.BlockSpec((n_seq, width, width), lambda b, t: (b, 0, 0))],
        out_shape=[jax.ShapeDtypeStruct((batch * seq, 2 * width), BF16),
                   jax.ShapeDtypeStruct((batch, width, width), F32)],
        scratch_shapes=_recur_scratch(n_seq, n_seq * tm, rtm, width),
        compiler_params=pltpu.CompilerParams(dimension_semantics=("parallel", "arbitrary"),
                                             vmem_limit_bytes=V7X_VMEM_LIMIT),
        name="pool_hgrn_mixer",
    )(z, z, z, z, z, hist0, st0, lbraw, *consts)


def _block_diag(blocks):
    g = blocks.shape[-3]
    zero = jnp.zeros_like(blocks[..., 0, :, :])
    rows = [jnp.concatenate([blocks[..., h, :, :] if j == h else zero for j in range(g)], axis=-1)
            for h in range(g)]
    return jnp.concatenate(rows, axis=-2)


def _state_from_kernel(st, n_heads):
    b, w, _ = st.shape
    hd = w // n_heads
    blocks = jnp.stack([st[:, h * hd:(h + 1) * hd, h * hd:(h + 1) * hd] for h in range(n_heads)], axis=1)
    return jnp.swapaxes(blocks, -1, -2)


def kernel(x_prompt, x_sample, p_prompt, p_sample, cache_attn_k, cache_attn_v, state_pool, state_hgrn, ffn1_w_gu, ffn1_w_down, w_in, attn_rel_bias, pool_w, pool_scale, hgrn_lower_bounds, hgrn_norm_w, w_out, ffn2_w_gu, ffn2_w_down, ple_w_gate, ple_w_proj, ln_g, ln_b):
    depth = w_in.shape[0]
    alpha = float((2 * depth) ** 0.25)
    nb, seq, d = x_prompt.shape
    db, dt, _ = x_sample.shape
    a_width = A_HEADS * HEAD_DIM
    c_width = C_HEADS * HEAD_DIM
    in_width = w_in.shape[-1]
    col0 = 3 * a_width // c_width
    past_rows = min(A_BACK * CHUNK, seq)
    assert dt >= POOL_HIST and seq >= POOL_HIST and PAST_LEN >= POOL_HIST

    bf = lambda w: w.astype(BF16)
    w1gu, w1dn, win, wout = bf(ffn1_w_gu), bf(ffn1_w_down), bf(w_in), bf(w_out)
    w2gu, w2dn, wg, wp = bf(ffn2_w_gu), bf(ffn2_w_down), bf(ple_w_gate), bf(ple_w_proj)
    poolw = bf(_block_diag(pool_w))
    pscale = pool_scale.astype(F32)[:, None, :]
    normw = jnp.tile(hgrn_norm_w.astype(F32), (1, C_HEADS))[:, None, :]
    lbraw = hgrn_lower_bounds.astype(F32)
    pp = p_prompt.reshape(depth, nb * seq, -1)
    ps = p_sample.reshape(depth, db * dt, -1)

    zero_hist = jnp.zeros((1, nb, POOL_PAD, c_width), F32)
    zero_state = jnp.zeros((1, nb, c_width, c_width), F32)
    hist_s = jnp.pad(state_pool.astype(F32), ((0, 0), (0, 0), (POOL_PAD - POOL_HIST, 0), (0, 0)))
    state_s = _block_diag(jnp.swapaxes(state_hgrn.astype(F32), -1, -2))

    lc = cache_attn_k.shape[2]
    cache_k = cache_attn_k.reshape(depth, db, lc, a_width)
    cache_v = cache_attn_v.reshape(depth, db, lc, a_width)

    xp = x_prompt.reshape(nb * seq, d)
    xs = x_sample.reshape(db * dt, d)
    outs = [[] for _ in range(8)]

    for i in range(depth):
        rel_table = attn_rel_bias[i].astype(F32)

        xp, zp = _pre_call(xp, w1gu, w1dn, ln_g, ln_b, win, i, alpha)
        oa = _attn_prompt_call(zp, rel_table, nb, seq, a_width)
        obc, st = _recur_call(zp, zero_hist, zero_state, lbraw, normw, poolw, pscale, i, 0, nb, seq, CHUNK, 0, col0)
        xp = _post_call(xp, oa, obc, pp, wout, w2gu, w2dn, wg, wp, ln_g, ln_b, i, alpha)
        zp3 = zp.reshape(nb, seq, in_width)
        outs[0].append(zp3[:, seq - past_rows:, a_width:2 * a_width].reshape(nb, past_rows, A_HEADS, HEAD_DIM))
        outs[1].append(zp3[:, seq - past_rows:, 2 * a_width:3 * a_width].reshape(nb, past_rows, A_HEADS, HEAD_DIM))
        outs[2].append(zp3[:, seq - POOL_HIST:, 3 * a_width:3 * a_width + c_width])
        outs[3].append(_state_from_kernel(st, C_HEADS))

        xs, zs = _pre_call(xs, w1gu, w1dn, ln_g, ln_b, win, i, alpha)
        oa = _attn_sample_call(zs, cache_k, cache_v, rel_table, i, db, dt, a_width)
        obc, st = _recur_call(zs, hist_s, state_s, lbraw, normw, poolw, pscale, i, i, db, dt, dt, PAST_LEN, col0)
        xs = _post_call(xs, oa, obc, ps, wout, w2gu, w2dn, wg, wp, ln_g, ln_b, i, alpha)
        zs3 = zs.reshape(db, dt, in_width)
        outs[4].append(zs3[:, :, a_width:2 * a_width].reshape(db, dt, A_HEADS, HEAD_DIM))
        outs[5].append(zs3[:, :, 2 * a_width:3 * a_width].reshape(db, dt, A_HEADS, HEAD_DIM))
        outs[6].append(zs3[:, dt - POOL_HIST:, 3 * a_width:3 * a_width + c_width])
        outs[7].append(_state_from_kernel(st, C_HEADS))

    stacked = [jnp.stack(o, axis=0) for o in outs]
    return (xp.reshape(nb, seq, d), xs.reshape(db, dt, d), *stacked)
```
